```python
import jax
import jax.numpy as jnp
from jax import lax
import numpy as np

D_MODEL = 1024
BATCH = 32
SEQ = 2048
DEPTH = 1
DEC_BATCH = 16
DEC_SEQ = 4096
PAST_LEN = 128

GRID_W = 64
NA_HEADS = 8
NA_HEAD_DIM = D_MODEL // 16
NA_WIDTH = NA_HEADS * NA_HEAD_DIM
NA_KH_MAX = 8
NA_KW = 16
GLA_HEADS = 4
GLA_KEY_WIDTH = D_MODEL // 4
GLA_VAL_WIDTH = D_MODEL // 2
GLA_DK = GLA_KEY_WIDTH // GLA_HEADS
GLA_DV = GLA_VAL_WIDTH // GLA_HEADS
GLA_GATE_RANK = 16
GLA_GATE_NORMALIZER = 16.0
GLA_CHUNK = 16
D_FF = 4 * D_MODEL
RMS_EPS = 1e-6
IN_WIDTHS = (NA_WIDTH, NA_WIDTH, NA_WIDTH, GLA_KEY_WIDTH, GLA_KEY_WIDTH, GLA_VAL_WIDTH, GLA_VAL_WIDTH, GLA_GATE_RANK, GLA_GATE_RANK, D_MODEL, D_MODEL)
D_IN = sum(IN_WIDTHS)

kernel_name = 'hybrid_na_gla_encoder'


def rmsnorm(x, g):
    xf = x.astype(jnp.float32)
    xf = xf * lax.rsqrt(jnp.mean(xf * xf, axis=-1, keepdims=True) + RMS_EPS)
    return xf.astype(x.dtype) * g


def to_heads(t, n_heads):
    b, t_len, _ = t.shape
    return t.reshape(b, t_len, n_heads, -1).transpose(0, 2, 1, 3)


def from_heads(t):
    b, h, t_len, d = t.shape
    return t.transpose(0, 2, 1, 3).reshape(b, t_len, h * d)


def neighbourhood_attention(q, k, v, rpb):
    b, h, t_len, d = q.shape
    rows = t_len // GRID_W
    kh = min(NA_KH_MAX, rows)
    r = np.arange(rows)
    row_start = np.clip(r - kh // 2, 0, rows - kh)
    row_idx = row_start[:, None] + np.arange(kh)[None, :]
    dr_idx = row_idx - r[:, None] + (NA_KH_MAX - 1)
    c = np.arange(GRID_W)
    col_start = np.clip(c - NA_KW // 2, 0, GRID_W - NA_KW)
    col_mask = (c[None, :] >= col_start[:, None]) & (c[None, :] < col_start[:, None] + NA_KW)
    dc_idx = np.clip(c[None, :] - c[:, None], -(NA_KW - 1), NA_KW - 1) + (NA_KW - 1)
    qg = q.reshape(b, h, rows, GRID_W, d) * (d ** -0.5)
    kg = k.reshape(b, h, rows, GRID_W, d)[:, :, row_idx]
    vg = v.reshape(b, h, rows, GRID_W, d)[:, :, row_idx]
    bias = rpb[:, dr_idx[:, None, :, None], dc_idx[None, :, None, :]]
    s = jnp.einsum('bhrqd,bhrkcd->bhrqkc', qg, kg).astype(jnp.float32) + bias.astype(jnp.float32)
    s = jnp.where(col_mask[:, None, :], s, -jnp.inf)
    p = jax.nn.softmax(s.reshape(b, h, rows, GRID_W, kh * GRID_W), axis=-1)
    p = p.reshape(s.shape).astype(v.dtype)
    o = jnp.einsum('bhrqkc,bhrkcd->bhrqd', p, vg)
    return o.reshape(b, h, t_len, d)


def gla_chunked(q, k, v, log_a):
    b, h, t_len, dk = q.shape
    dv = v.shape[-1]
    n = t_len // GLA_CHUNK
    q, k, log_a = (t.astype(jnp.float32).reshape(b, h, n, GLA_CHUNK, dk) for t in (q, k, log_a))
    v = v.astype(jnp.float32).reshape(b, h, n, GLA_CHUNK, dv)
    cum = jnp.cumsum(log_a, axis=3)
    cum_last = cum[:, :, :, -1:, :]
    causal = np.tril(np.ones((GLA_CHUNK, GLA_CHUNK), dtype=bool))
    diff = cum[:, :, :, :, None, :] - cum[:, :, :, None, :, :]
    decay = jnp.exp(jnp.where(causal[:, :, None], diff, -jnp.inf))
    scores = jnp.einsum('bhnid,bhnjd,bhnijd->bhnij', q, k, decay)
    o_intra = jnp.einsum('bhnij,bhnjv->bhniv', scores, v)
    q_dec = q * jnp.exp(cum)
    k_dec = k * jnp.exp(cum_last - cum)
    chunk_decay = jnp.exp(cum_last[:, :, :, 0, :])

    def step(state, xs):
        qd, kd, vc, cd = xs
        o = jnp.einsum('bhid,bhdv->bhiv', qd, state)
        state = cd[..., None] * state + jnp.einsum('bhjd,bhjv->bhdv', kd, vc)
        return state, o

    xs = tuple(jnp.moveaxis(t, 2, 0) for t in (q_dec, k_dec, v, chunk_decay))
    state0 = jnp.zeros((b, h, dk, dv), jnp.float32)
    _, o_inter = lax.scan(step, state0, xs)
    return (o_intra + jnp.moveaxis(o_inter, 0, 2)).reshape(b, h, t_len, dv)


def gla_bidirectional(q, k, v, log_a_fwd, log_a_bwd):
    o_fwd = gla_chunked(q, k, v, log_a_fwd)
    flip = lambda t: jnp.flip(t, axis=2)
    o_bwd = flip(gla_chunked(flip(q), flip(k), flip(v), flip(log_a_bwd)))
    return o_fwd + o_bwd


def hybrid_mixer(u, w_in, b_in, na_rpb, gk_fwd_w, gk_fwd_b, gk_bwd_w, gk_bwd_b, gla_norm_g, w_br_na, w_br_gla, w_out):
    proj = jnp.einsum('btd,de->bte', u, w_in) + b_in
    split_points = np.cumsum(IN_WIDTHS)[:-1].tolist()
    (na_q, na_k, na_v, gla_q, gla_k, gla_v, gla_g, lr_fwd, lr_bwd, gate_na, gate_gla) = jnp.split(proj, split_points, axis=-1)
    na_o = neighbourhood_attention(to_heads(na_q, NA_HEADS), to_heads(na_k, NA_HEADS), to_heads(na_v, NA_HEADS), na_rpb)
    na_out = jnp.einsum('bte,ed->btd', from_heads(na_o), w_br_na)
    log_a_fwd = jax.nn.log_sigmoid((jnp.einsum('btr,rk->btk', lr_fwd, gk_fwd_w) + gk_fwd_b).astype(jnp.float32)) / GLA_GATE_NORMALIZER
    log_a_bwd = jax.nn.log_sigmoid((jnp.einsum('btr,rk->btk', lr_bwd, gk_bwd_w) + gk_bwd_b).astype(jnp.float32)) / GLA_GATE_NORMALIZER
    gla_o = gla_bidirectional(to_heads(gla_q, GLA_HEADS) * (GLA_DK ** -0.5), to_heads(gla_k, GLA_HEADS), to_heads(gla_v, GLA_HEADS), to_heads(log_a_fwd, GLA_HEADS), to_heads(log_a_bwd, GLA_HEADS))
    gla_o = rmsnorm(gla_o.astype(u.dtype), gla_norm_g)
    gla_o = from_heads(gla_o) * jax.nn.silu(gla_g)
    gla_out = jnp.einsum('bte,ed->btd', gla_o, w_br_gla)
    merged = jax.nn.sigmoid(gate_na) * na_out + jax.nn.sigmoid(gate_gla) * gla_out
    return jnp.einsum('btd,de->bte', merged, w_out)


def sq_relu_mlp(u, w_up, w_down):
    hdn = jnp.square(jax.nn.relu(jnp.einsum('btd,df->btf', u, w_up)))
    return jnp.einsum('btf,fd->btd', hdn, w_down)


def trunk(x, norm_mix_g, w_in, b_in, na_rpb, gk_fwd_w, gk_fwd_b, gk_bwd_w, gk_bwd_b, gla_norm_g, w_br_na, w_br_gla, w_out, norm_mlp_g, w_up, w_down, norm_final_g):
    h = x
    for l in range(DEPTH):
        h = h + hybrid_mixer(rmsnorm(h, norm_mix_g[l]), w_in[l], b_in[l], na_rpb[l], gk_fwd_w[l], gk_fwd_b[l], gk_bwd_w[l], gk_bwd_b[l], gla_norm_g[l], w_br_na[l], w_br_gla[l], w_out[l])
        h = h + sq_relu_mlp(rmsnorm(h, norm_mlp_g[l]), w_up[l], w_down[l])
    return rmsnorm(h, norm_final_g)


def setup_inputs(seed: int = 0) -> dict:
    key = jax.random.key(seed)
    ks = jax.random.split(key, 20)

    def nrm(k, shape, scale):
        return jax.random.normal(k, shape, jnp.float32) * scale

    return {
        'x_prompt': nrm(ks[0], (BATCH, SEQ, D_MODEL), 1.0),
        'x_sample': nrm(ks[1], (DEC_BATCH, DEC_SEQ, D_MODEL), 1.0),
        'norm_mix_g': 1.0 + nrm(ks[2], (DEPTH, D_MODEL), 0.01),
        'w_in': nrm(ks[3], (DEPTH, D_MODEL, D_IN), D_MODEL ** -0.5),
        'b_in': nrm(ks[4], (DEPTH, D_IN), 0.01),
        'na_rpb': nrm(ks[5], (DEPTH, NA_HEADS, 2 * NA_KH_MAX - 1, 2 * NA_KW - 1), 0.02),
        'gk_fwd_w': nrm(ks[6], (DEPTH, GLA_GATE_RANK, GLA_KEY_WIDTH), GLA_GATE_RANK ** -0.5),
        'gk_fwd_b': nrm(ks[7], (DEPTH, GLA_KEY_WIDTH), 0.01),
        'gk_bwd_w': nrm(ks[8], (DEPTH, GLA_GATE_RANK, GLA_KEY_WIDTH), GLA_GATE_RANK ** -0.5),
        'gk_bwd_b': nrm(ks[9], (DEPTH, GLA_KEY_WIDTH), 0.01),
        'gla_norm_g': 1.0 + nrm(ks[10], (DEPTH, GLA_DV), 0.01),
        'w_br_na': nrm(ks[11], (DEPTH, NA_WIDTH, D_MODEL), NA_WIDTH ** -0.5),
        'w_br_gla': nrm(ks[12], (DEPTH, GLA_VAL_WIDTH, D_MODEL), GLA_VAL_WIDTH ** -0.5),
        'w_out': nrm(ks[13], (DEPTH, D_MODEL, D_MODEL), D_MODEL ** -0.5),
        'norm_mlp_g': 1.0 + nrm(ks[14], (DEPTH, D_MODEL), 0.01),
        'w_up': nrm(ks[15], (DEPTH, D_MODEL, D_FF), D_MODEL ** -0.5),
        'w_down': nrm(ks[16], (DEPTH, D_FF, D_MODEL), D_FF ** -0.5),
        'norm_final_g': 1.0 + nrm(ks[17], (D_MODEL,), 0.01),
    }


def reference(x_prompt, x_sample, norm_mix_g, w_in, b_in, na_rpb, gk_fwd_w, gk_fwd_b, gk_bwd_w, gk_bwd_b, gla_norm_g, w_br_na, w_br_gla, w_out, norm_mlp_g, w_up, w_down, norm_final_g):
    y_prompt = trunk(x_prompt, norm_mix_g, w_in, b_in, na_rpb, gk_fwd_w, gk_fwd_b, gk_bwd_w, gk_bwd_b, gla_norm_g, w_br_na, w_br_gla, w_out, norm_mlp_g, w_up, w_down, norm_final_g)
    y_sample = trunk(x_sample, norm_mix_g, w_in, b_in, na_rpb, gk_fwd_w, gk_fwd_b, gk_bwd_w, gk_bwd_b, gla_norm_g, w_br_na, w_br_gla, w_out, norm_mlp_g, w_up, w_down, norm_final_g)
    return (y_prompt, y_sample)
```

```python
import functools

import numpy as np
import jax
import jax.numpy as jnp
from jax import lax
from jax.experimental import pallas as pl
from jax.experimental.pallas import tpu as pltpu

D_MODEL = 1024
GRID_W = 64
NA_HEADS = 8
NA_HEAD_DIM = 64
NA_WIDTH = NA_HEADS * NA_HEAD_DIM
NA_KH = 8
NA_KW = 16
GLA_HEADS = 4
GLA_KEY_WIDTH = 256
GLA_VAL_WIDTH = 512
GLA_DK = 64
GLA_DV = 128
GLA_GATE_RANK = 16
GLA_GATE_NORMALIZER = 16.0
D_FF = 4 * D_MODEL
RMS_EPS = 1e-6
IN_WIDTHS = (NA_WIDTH, NA_WIDTH, NA_WIDTH, GLA_KEY_WIDTH, GLA_KEY_WIDTH, GLA_VAL_WIDTH, GLA_VAL_WIDTH,
             GLA_GATE_RANK, GLA_GATE_RANK, D_MODEL, D_MODEL)

LANES = 128
GLA_CHUNK = 128
GLA_LEVELS = 7
LR_PAD = LANES
VMEM_LIMIT = 56 * 1024 * 1024

BF16 = jnp.bfloat16
F32 = jnp.float32


def _dot(a, b):
    return jnp.dot(a, b, preferred_element_type=F32)


def _dot_nt(a, b):
    return lax.dot_general(a, b, (((1,), (1,)), ((), ())), preferred_element_type=F32)


def _dot_tn(a, b):
    return lax.dot_general(a, b, (((0,), (0,)), ((), ())), preferred_element_type=F32)


def _sigmoid(z):
    return 1.0 / (1.0 + jnp.exp(-z))


def _rms(x):
    return x * lax.rsqrt(jnp.mean(x * x, axis=-1, keepdims=True) + RMS_EPS)


def _proj_kernel(x_ref, g_ref, wna_ref, bna_ref, wgla_ref, bgla_ref, wgate_ref, bgate_ref,
                 wlr_ref, blr_ref, gkw_ref, gkb_ref, na_ref, gla_ref, gate_ref, loga_ref):
    xb = (_rms(x_ref[...]) * g_ref[...]).astype(BF16)
    na_ref[...] = (_dot(xb, wna_ref[...]) + bna_ref[...]).astype(BF16)
    gla_ref[...] = (_dot(xb, wgla_ref[...]) + bgla_ref[...]).astype(BF16)
    gate_ref[...] = _sigmoid(_dot(xb, wgate_ref[...]) + bgate_ref[...]).astype(BF16)
    lr = (_dot(xb, wlr_ref[...]) + blr_ref[...]).astype(BF16)
    z = _dot(lr, gkw_ref[...]) + gkb_ref[...]
    log_sig = jnp.minimum(z, 0.0) - jnp.log1p(jnp.exp(-jnp.abs(z)))
    loga_ref[...] = log_sig * (1.0 / GLA_GATE_NORMALIZER)


def _const_spec(shape):
    nd = len(shape)
    return pl.BlockSpec(shape, lambda *_: (0,) * nd, pipeline_mode=pl.Buffered(1))


def _proj_call(x2, g, wna, bna, wgla, bgla, wgate, bgate, wlr, blr, gkw, gkb, *, tm):
    n = x2.shape[0]
    row = lambda w: pl.BlockSpec((tm, w), lambda i: (i, 0))
    consts = (g, wna, bna, wgla, bgla, wgate, bgate, wlr, blr, gkw, gkb)
    return pl.pallas_call(
        _proj_kernel,
        grid=(n // tm,),
        in_specs=[row(D_MODEL)] + [_const_spec(c.shape) for c in consts],
        out_specs=[row(3 * NA_WIDTH), row(3 * GLA_VAL_WIDTH), row(2 * D_MODEL), row(2 * GLA_KEY_WIDTH)],
        out_shape=[jax.ShapeDtypeStruct((n, 3 * NA_WIDTH), BF16),
                   jax.ShapeDtypeStruct((n, 3 * GLA_VAL_WIDTH), BF16),
                   jax.ShapeDtypeStruct((n, 2 * D_MODEL), BF16),
                   jax.ShapeDtypeStruct((n, 2 * GLA_KEY_WIDTH), F32)],
        compiler_params=pltpu.CompilerParams(dimension_semantics=("parallel",), vmem_limit_bytes=VMEM_LIMIT),
        name="proj",
    )(x2, *consts)


def _na_bias_table(rpb):
    c = np.arange(GRID_W)
    col_start = np.clip(c - NA_KW // 2, 0, GRID_W - NA_KW)
    col_mask = (c[None, :] >= col_start[:, None]) & (c[None, :] < col_start[:, None] + NA_KW)
    dc_idx = np.clip(c[None, :] - c[:, None], -(NA_KW - 1), NA_KW - 1) + (NA_KW - 1)
    off = np.arange(NA_KH)
    dr_idx = np.arange(NA_KH)[None, :] - off[:, None] + (NA_KH - 1)
    bias = rpb[:, dr_idx[:, None, :, None], dc_idx[None, :, None, :]]
    bias = jnp.where(col_mask[None, None, :, None, :], bias.astype(F32), -jnp.inf)
    return bias.reshape(rpb.shape[0], NA_KH, GRID_W, NA_KH * GRID_W)


def _na_kernel(q_ref, k_ref, v_ref, bias_ref, o_ref, *, rows):
    lane = lax.broadcasted_iota(jnp.int32, (GRID_W, LANES), 1)
    first = lane < NA_HEAD_DIM
    win = NA_KH * GRID_W

    def body(r, carry):
        start = jnp.clip(r - NA_KH // 2, 0, rows - NA_KH)
        off = r - start
        q = q_ref[0, pl.ds(pl.multiple_of(r * GRID_W, GRID_W), GRID_W), :] * (NA_HEAD_DIM ** -0.5)
        kw = k_ref[0, pl.ds(pl.multiple_of(start * GRID_W, GRID_W), win), :]
        vw = v_ref[0, pl.ds(pl.multiple_of(start * GRID_W, GRID_W), win), :]
        zero = jnp.zeros_like(q)
        qs = jnp.concatenate([jnp.where(first, q, zero), jnp.where(first, zero, q)], axis=0)
        s = _dot_nt(qs, kw) + bias_ref[:, off].reshape(2 * GRID_W, win)
        p = jnp.exp(s - jnp.max(s, axis=-1, keepdims=True))
        inv = 1.0 / jnp.sum(p, axis=-1, keepdims=True)
        o = _dot(p.astype(BF16), vw) * inv
        o_ref[0, pl.ds(pl.multiple_of(r * GRID_W, GRID_W), GRID_W), :] = jnp.where(
            first, o[:GRID_W], o[GRID_W:]).astype(BF16)
        return carry

    lax.fori_loop(0, rows, body, 0)


def _na_call(na_qkv, bias, *, batch, seq):
    rows = seq // GRID_W
    assert rows >= NA_KH and seq % GRID_W == 0
    groups = NA_WIDTH // LANES
    qkv = na_qkv.reshape(batch, seq, 3 * NA_WIDTH)
    spec = lambda part: pl.BlockSpec((1, seq, LANES), lambda b, p: (b, 0, part * groups + p))
    return pl.pallas_call(
        functools.partial(_na_kernel, rows=rows),
        grid=(batch, groups),
        in_specs=[spec(0), spec(1), spec(2),
                  pl.BlockSpec((2, NA_KH, GRID_W, NA_KH * GRID_W), lambda b, p: (p, 0, 0, 0))],
        out_specs=pl.BlockSpec((1, seq, LANES), lambda b, p: (b, 0, p)),
        out_shape=jax.ShapeDtypeStruct((batch, seq, NA_WIDTH), BF16),
        compiler_params=pltpu.CompilerParams(dimension_semantics=("parallel", "arbitrary"),
                                             vmem_limit_bytes=VMEM_LIMIT),
        name="na",
    )(qkv, qkv, qkv, bias)


def _gla_constants(reverse):
    c = GLA_CHUNK
    t = np.arange(c)
    cum = (t[None, :] <= t[:, None]).astype(np.float32)
    blocks, masks = [cum], [np.eye(c, dtype=np.float32)]
    for lvl in range(GLA_LEVELS):
        m = 1 << lvl
        base = (t // (2 * m)) * (2 * m)
        upper = (t % (2 * m)) >= m
        s = t[None, :]
        q_rows = upper[:, None] & (s >= (base + m)[:, None]) & (s <= t[:, None])
        k_rows = (~upper)[:, None] & (s > t[:, None]) & (s <= (base + m - 1)[:, None])
        blocks.append((q_rows | k_rows).astype(np.float32))
        masks.append((upper[:, None] & (~upper)[None, :] & (base[:, None] == base[None, :])).astype(np.float32))
    if reverse:
        blocks = [b[::-1, ::-1] for b in blocks]
        masks = [mk[::-1, ::-1] for mk in masks]
    sums = np.concatenate(blocks, axis=0)
    mask = np.stack([np.concatenate([mk, mk], axis=0) for mk in masks])
    return jnp.asarray(sums, BF16), jnp.asarray(mask, F32)


def _gla_direction(q_ref, k_ref, v_ref, la_ref, sums_ref, mask_ref, state_ref, o_ref, last_row):
    c = GLA_CHUNK
    la = la_ref[0]
    la_hi = la.astype(BF16)
    la_lo = (la - la_hi.astype(F32)).astype(BF16)
    sums = _dot(sums_ref[...], jnp.concatenate([la_hi, la_lo], axis=1))
    sums = sums[:, :GLA_KEY_WIDTH] + sums[:, GLA_KEY_WIDTH:]
    cum = sums[:c]
    cum_last = cum[last_row:last_row + 1]
    q = q_ref[0].astype(F32) * (GLA_DK ** -0.5)
    k = k_ref[0].astype(F32)
    v = v_ref[0]
    lane = lax.broadcasted_iota(jnp.int32, (c, LANES), 1)
    first = lane < GLA_DK

    def stack_heads(x):
        zero = jnp.zeros_like(x)
        return jnp.concatenate([jnp.where(first, x, zero), jnp.where(first, zero, x)], axis=0)

    q_dec = (q * jnp.exp(cum)).astype(BF16)
    k_dec = (k * jnp.exp(cum_last - cum)).astype(BF16)
    decay = jnp.exp(cum_last)
    for pair in range(GLA_HEADS // 2):
        ks = slice(pair * LANES, (pair + 1) * LANES)
        vs = slice(pair * 2 * GLA_DV, (pair + 1) * 2 * GLA_DV)
        scores = mask_ref[0] * _dot_nt(stack_heads(q[:, ks].astype(BF16)), k[:, ks].astype(BF16))
        for lvl in range(GLA_LEVELS):
            e = jnp.exp(sums[(lvl + 1) * c:(lvl + 2) * c, ks])
            scores += mask_ref[lvl + 1] * _dot_nt(stack_heads((q[:, ks] * e).astype(BF16)),
                                                  (k[:, ks] * e).astype(BF16))
        state = state_ref[pair]
        o = _dot(scores.astype(BF16), v[:, vs]) + _dot_nt(stack_heads(q_dec[:, ks]), state.astype(BF16))
        o_ref[0, :, pair * 2 * GLA_DV:pair * 2 * GLA_DV + GLA_DV] = o[:c, :GLA_DV]
        o_ref[0, :, pair * 2 * GLA_DV + GLA_DV:(pair + 1) * 2 * GLA_DV] = o[c:, GLA_DV:]
        upd = _dot_tn(v[:, vs], k_dec[:, ks])
        row = lax.broadcasted_iota(jnp.int32, upd.shape, 0)
        col = lax.broadcasted_iota(jnp.int32, upd.shape, 1)
        own = (row < GLA_DV) == (col < GLA_DK)
        state_ref[pair] = state * decay[:, ks] + jnp.where(own, upd, 0.0)


def _gla_kernel(qf_ref, kf_ref, vf_ref, laf_ref, qb_ref, kb_ref, vb_ref, lab_ref,
                sums_f_ref, mask_f_ref, sums_b_ref, mask_b_ref, of_ref, ob_ref, state_f_ref, state_b_ref):
    @pl.when(pl.program_id(1) == 0)
    def _():
        state_f_ref[...] = jnp.zeros_like(state_f_ref)
        state_b_ref[...] = jnp.zeros_like(state_b_ref)

    _gla_direction(qf_ref, kf_ref, vf_ref, laf_ref, sums_f_ref, mask_f_ref, state_f_ref, of_ref, GLA_CHUNK - 1)
    _gla_direction(qb_ref, kb_ref, vb_ref, lab_ref, sums_b_ref, mask_b_ref, state_b_ref, ob_ref, 0)


def _gla_call(gla, loga, *, batch, seq):
    c = GLA_CHUNK
    assert seq % c == 0
    n = seq // c
    gla3 = gla.reshape(batch, seq, 3 * GLA_VAL_WIDTH)
    loga3 = loga.reshape(batch, seq, 2 * GLA_KEY_WIDTH)
    fwd = lambda b, i: i
    bwd = lambda b, i: n - 1 - i
    sums_f, mask_f = _gla_constants(False)
    sums_b, mask_b = _gla_constants(True)

    def specs(chunk):
        return [pl.BlockSpec((1, c, GLA_KEY_WIDTH), lambda b, i: (b, chunk(b, i), 0)),
                pl.BlockSpec((1, c, GLA_KEY_WIDTH), lambda b, i: (b, chunk(b, i), 1)),
                pl.BlockSpec((1, c, GLA_VAL_WIDTH), lambda b, i: (b, chunk(b, i), 1))]

    la_spec = lambda chunk, part: pl.BlockSpec((1, c, GLA_KEY_WIDTH), lambda b, i: (b, chunk(b, i), part))
    out_spec = lambda chunk: pl.BlockSpec((1, c, GLA_VAL_WIDTH), lambda b, i: (b, chunk(b, i), 0))
    state = pltpu.VMEM((GLA_HEADS // 2, 2 * GLA_DV, LANES), F32)
    return pl.pallas_call(
        _gla_kernel,
        grid=(batch, n),
        in_specs=specs(fwd) + [la_spec(fwd, 0)] + specs(bwd) + [la_spec(bwd, 1)]
        + [_const_spec(a.shape) for a in (sums_f, mask_f, sums_b, mask_b)],
        out_specs=[out_spec(fwd), out_spec(bwd)],
        out_shape=[jax.ShapeDtypeStruct((batch, seq, GLA_VAL_WIDTH), F32)] * 2,
        scratch_shapes=[state, state],
        compiler_params=pltpu.CompilerParams(dimension_semantics=("parallel", "arbitrary"),
                                             vmem_limit_bytes=VMEM_LIMIT),
        name="gla",
    )(gla3, gla3, gla3, loga3, gla3, gla3, gla3, loga3, sums_f, mask_f, sums_b, mask_b)


def _merge_kernel(x_ref, nao_ref, of_ref, ob_ref, g_ref, gate_ref, gng_ref, wna_ref, wgla_ref, wout_ref, h_ref):
    na_out = _dot(nao_ref[...], wna_ref[...])
    o = of_ref[...] + ob_ref[...]
    heads = []
    for h in range(GLA_HEADS):
        heads.append(_rms(o[:, h * GLA_DV:(h + 1) * GLA_DV]) * gng_ref[...])
    g = g_ref[...].astype(F32)
    gla_o = jnp.concatenate(heads, axis=1) * (g * _sigmoid(g))
    gla_out = _dot(gla_o.astype(BF16), wgla_ref[...])
    gate = gate_ref[...].astype(F32)
    merged = gate[:, :D_MODEL] * na_out + gate[:, D_MODEL:] * gla_out
    h_ref[...] = x_ref[...] + _dot(merged.astype(BF16), wout_ref[...])


def _merge_call(x2, na_o, o_f, o_b, gla, gate, gng, wna, wgla, wout, *, tm):
    n = x2.shape[0]
    row = lambda w, j=0: pl.BlockSpec((tm, w), lambda i: (i, j))
    consts = (gng, wna, wgla, wout)
    return pl.pallas_call(
        _merge_kernel,
        grid=(n // tm,),
        in_specs=[row(D_MODEL), row(NA_WIDTH), row(GLA_VAL_WIDTH), row(GLA_VAL_WIDTH),
                  row(GLA_VAL_WIDTH, 2), row(2 * D_MODEL)] + [_const_spec(c.shape) for c in consts],
        out_specs=row(D_MODEL),
        out_shape=jax.ShapeDtypeStruct((n, D_MODEL), F32),
        compiler_params=pltpu.CompilerParams(dimension_semantics=("parallel",), vmem_limit_bytes=VMEM_LIMIT),
        name="merge",
    )(x2, na_o, o_f, o_b, gla, gate, *consts)


def _mlp_kernel(h_ref, g_ref, wup_ref, wdown_ref, gf_ref, y_ref):
    h = h_ref[...]
    u = (_rms(h) * g_ref[...]).astype(BF16)
    hdn = jnp.square(jnp.maximum(_dot(u, wup_ref[...]), 0.0)).astype(BF16)
    h = h + _dot(hdn, wdown_ref[...])
    y_ref[...] = _rms(h) * gf_ref[...]


def _mlp_call(h, g, wup, wdown, gf, *, tm):
    n = h.shape[0]
    row = pl.BlockSpec((tm, D_MODEL), lambda i: (i, 0))
    consts = (g, wup, wdown, gf)
    return pl.pallas_call(
        _mlp_kernel,
        grid=(n // tm,),
        in_specs=[row, _const_spec(g.shape), _const_spec(wup.shape), _const_spec(wdown.shape), _const_spec(gf.shape)],
        out_specs=row,
        out_shape=jax.ShapeDtypeStruct((n, D_MODEL), F32),
        compiler_params=pltpu.CompilerParams(dimension_semantics=("parallel",), vmem_limit_bytes=VMEM_LIMIT),
        name="mlp",
    )(h, g, wup, wdown, gf)


def _prepare(norm_mix_g, w_in, b_in, na_rpb, gk_fwd_w, gk_fwd_b, gk_bwd_w, gk_bwd_b, gla_norm_g,
             w_br_na, w_br_gla, w_out, norm_mlp_g, w_up, w_down, norm_final_g):
    edges = np.concatenate([[0], np.cumsum(IN_WIDTHS)])
    wcols = lambda a, b: w_in[:, edges[a]:edges[b]]
    bcols = lambda a, b: b_in[None, edges[a]:edges[b]]
    r = GLA_GATE_RANK
    pad = LR_PAD - 2 * r
    wlr = jnp.pad(wcols(7, 9), ((0, 0), (0, pad)))
    blr = jnp.pad(bcols(7, 9), ((0, 0), (0, pad)))
    gkw = jnp.zeros((LR_PAD, 2 * GLA_KEY_WIDTH), F32)
    gkw = gkw.at[:r, :GLA_KEY_WIDTH].set(gk_fwd_w).at[r:2 * r, GLA_KEY_WIDTH:].set(gk_bwd_w)
    gkb = jnp.concatenate([gk_fwd_b, gk_bwd_b])[None]
    return dict(
        proj=(norm_mix_g[None], wcols(0, 3).astype(BF16), bcols(0, 3), wcols(3, 7).astype(BF16), bcols(3, 7),
              wcols(9, 11).astype(BF16), bcols(9, 11), wlr.astype(BF16), blr, gkw.astype(BF16), gkb),
        na_bias=_na_bias_table(na_rpb),
        merge=(gla_norm_g[None], w_br_na.astype(BF16), w_br_gla.astype(BF16), w_out.astype(BF16)),
        mlp=(norm_mlp_g[None], w_up.astype(BF16), w_down.astype(BF16), norm_final_g[None]),
    )


def _trunk(x, prep, *, tm):
    batch, seq, _ = x.shape
    x2 = x.reshape(batch * seq, D_MODEL)
    na_qkv, gla, gate, loga = _proj_call(x2, *prep["proj"], tm=tm)
    na_o = _na_call(na_qkv, prep["na_bias"], batch=batch, seq=seq)
    o_f, o_b = _gla_call(gla, loga, batch=batch, seq=seq)
    h = _merge_call(x2, na_o.reshape(batch * seq, NA_WIDTH), o_f.reshape(batch * seq, GLA_VAL_WIDTH),
                    o_b.reshape(batch * seq, GLA_VAL_WIDTH), gla, gate, *prep["merge"], tm=tm)
    y = _mlp_call(h, *prep["mlp"], tm=tm)
    return y.reshape(batch, seq, D_MODEL)


def kernel(x_prompt, x_sample, norm_mix_g, w_in, b_in, na_rpb, gk_fwd_w, gk_fwd_b, gk_bwd_w, gk_bwd_b, gla_norm_g, w_br_na, w_br_gla, w_out, norm_mlp_g, w_up, w_down, norm_final_g):
    assert norm_mix_g.shape[0] == 1, "single-layer trunk"
    prep = _prepare(norm_mix_g[0], w_in[0], b_in[0], na_rpb[0], gk_fwd_w[0], gk_fwd_b[0], gk_bwd_w[0],
                    gk_bwd_b[0], gla_norm_g[0], w_br_na[0], w_br_gla[0], w_out[0], norm_mlp_g[0], w_up[0],
                    w_down[0], norm_final_g)
    return _trunk(x_prompt, prep, tm=512), _trunk(x_sample, prep, tm=512)
```

```python
import functools

import numpy as np
import jax
import jax.numpy as jnp
from jax import lax
from jax.experimental import pallas as pl
from jax.experimental.pallas import tpu as pltpu

D_MODEL = 1024
GRID_W = 64
NA_HEADS = 8
NA_HEAD_DIM = 64
NA_WIDTH = NA_HEADS * NA_HEAD_DIM
NA_KH = 8
NA_KW = 16
GLA_HEADS = 4
GLA_KEY_WIDTH = 256
GLA_VAL_WIDTH = 512
GLA_DK = 64
GLA_DV = 128
GLA_GATE_RANK = 16
GLA_GATE_NORMALIZER = 16.0
D_FF = 4 * D_MODEL
RMS_EPS = 1e-6
IN_WIDTHS = (NA_WIDTH, NA_WIDTH, NA_WIDTH, GLA_KEY_WIDTH, GLA_KEY_WIDTH, GLA_VAL_WIDTH, GLA_VAL_WIDTH,
             GLA_GATE_RANK, GLA_GATE_RANK, D_MODEL, D_MODEL)

LANES = 128
GLA_CHUNK = 128
GLA_LEVELS = 7
LR_PAD = LANES
NA_ROWS_PER_STEP = 8
VMEM_LIMIT = 56 * 1024 * 1024

BF16 = jnp.bfloat16
F32 = jnp.float32


def _dot(a, b):
    return jnp.dot(a, b, preferred_element_type=F32)


def _dot_nt(a, b):
    return lax.dot_general(a, b, (((1,), (1,)), ((), ())), preferred_element_type=F32)


def _dot_tn(a, b):
    return lax.dot_general(a, b, (((0,), (0,)), ((), ())), preferred_element_type=F32)


def _sigmoid(z):
    return 1.0 / (1.0 + jnp.exp(-z))


def _rms(x):
    return x * lax.rsqrt(jnp.mean(x * x, axis=-1, keepdims=True) + RMS_EPS)


def _proj_kernel(x_ref, g_ref, wna_ref, bna_ref, wgla_ref, bgla_ref, wgate_ref, bgate_ref,
                 wlr_ref, blr_ref, gkw_ref, gkb_ref, na_ref, gla_ref, gate_ref, loga_ref):
    xb = (_rms(x_ref[...]) * g_ref[...]).astype(BF16)
    na_ref[...] = (_dot(xb, wna_ref[...]) + bna_ref[...]).astype(BF16)
    gla_ref[...] = (_dot(xb, wgla_ref[...]) + bgla_ref[...]).astype(BF16)
    gate_ref[...] = _sigmoid(_dot(xb, wgate_ref[...]) + bgate_ref[...]).astype(BF16)
    lr = (_dot(xb, wlr_ref[...]) + blr_ref[...]).astype(BF16)
    z = _dot(lr, gkw_ref[...]) + gkb_ref[...]
    log_sig = jnp.minimum(z, 0.0) - jnp.log1p(jnp.exp(-jnp.abs(z)))
    loga_ref[...] = log_sig * (1.0 / GLA_GATE_NORMALIZER)


def _const_spec(shape):
    nd = len(shape)
    return pl.BlockSpec(shape, lambda *_: (0,) * nd, pipeline_mode=pl.Buffered(1))


def _proj_call(x2, g, wna, bna, wgla, bgla, wgate, bgate, wlr, blr, gkw, gkb, *, tm):
    n = x2.shape[0]
    row = lambda w: pl.BlockSpec((tm, w), lambda i: (i, 0))
    consts = (g, wna, bna, wgla, bgla, wgate, bgate, wlr, blr, gkw, gkb)
    return pl.pallas_call(
        _proj_kernel,
        grid=(n // tm,),
        in_specs=[row(D_MODEL)] + [_const_spec(c.shape) for c in consts],
        out_specs=[row(3 * NA_WIDTH), row(3 * GLA_VAL_WIDTH), row(2 * D_MODEL), row(2 * GLA_KEY_WIDTH)],
        out_shape=[jax.ShapeDtypeStruct((n, 3 * NA_WIDTH), BF16),
                   jax.ShapeDtypeStruct((n, 3 * GLA_VAL_WIDTH), BF16),
                   jax.ShapeDtypeStruct((n, 2 * D_MODEL), BF16),
                   jax.ShapeDtypeStruct((n, 2 * GLA_KEY_WIDTH), F32)],
        compiler_params=pltpu.CompilerParams(dimension_semantics=("parallel",), vmem_limit_bytes=VMEM_LIMIT),
        name="proj",
    )(x2, *consts)


def _na_bias_table(rpb):
    c = np.arange(GRID_W)
    col_start = np.clip(c - NA_KW // 2, 0, GRID_W - NA_KW)
    col_mask = (c[None, :] >= col_start[:, None]) & (c[None, :] < col_start[:, None] + NA_KW)
    dc_idx = np.clip(c[None, :] - c[:, None], -(NA_KW - 1), NA_KW - 1) + (NA_KW - 1)
    onehot = (dc_idx[None] == np.arange(2 * NA_KW - 1)[:, None, None]) & col_mask[None]
    cols = jnp.einsum('hrd,dqk->hrqk', rpb.astype(F32), jnp.asarray(onehot, F32),
                      precision=lax.Precision.HIGHEST)
    cols = jnp.where(col_mask[None, None], cols, -jnp.inf)
    bias = jnp.stack([cols[:, NA_KH - 1 - off:2 * NA_KH - 1 - off] for off in range(NA_KH)], axis=1)
    bias = bias.transpose(0, 1, 3, 2, 4)
    return bias.reshape(rpb.shape[0], NA_KH, GRID_W, NA_KH * GRID_W)


def _na_kernel(q_ref, k_ref, v_ref, bias_ref, o_ref, *, rows):
    lane = lax.broadcasted_iota(jnp.int32, (GRID_W, LANES), 1)
    first = lane < NA_HEAD_DIM
    win = NA_KH * GRID_W

    def window(r):
        start = jnp.clip(r - NA_KH // 2, 0, rows - NA_KH)
        return pl.ds(pl.multiple_of(start * GRID_W, GRID_W), win), r - start

    def scores(r):
        keys, off = window(r)
        q = q_ref[0, pl.ds(pl.multiple_of(r * GRID_W, GRID_W), GRID_W), :] * (NA_HEAD_DIM ** -0.5)
        zero = jnp.zeros_like(q)
        qs = jnp.concatenate([jnp.where(first, q, zero), jnp.where(first, zero, q)], axis=0)
        return _dot_nt(qs, k_ref[0, keys, :]) + bias_ref[:, off].reshape(2 * GRID_W, win)

    def attend(r, s):
        keys, _ = window(r)
        p = jnp.exp(s - jnp.max(s, axis=-1, keepdims=True))
        inv = 1.0 / jnp.sum(p, axis=-1, keepdims=True)
        o = _dot(p.astype(BF16), v_ref[0, keys, :]) * inv
        o_ref[0, pl.ds(pl.multiple_of(r * GRID_W, GRID_W), GRID_W), :] = jnp.where(
            first, o[:GRID_W], o[GRID_W:]).astype(BF16)

    def body(i, carry):
        rs = [i * NA_ROWS_PER_STEP + u for u in range(NA_ROWS_PER_STEP)]
        ss = [scores(r) for r in rs]
        for r, s in zip(rs, ss):
            attend(r, s)
        return carry

    lax.fori_loop(0, rows // NA_ROWS_PER_STEP, body, 0)


def _na_call(na_qkv, bias, *, batch, seq):
    rows = seq // GRID_W
    assert rows >= NA_KH and seq % GRID_W == 0 and rows % NA_ROWS_PER_STEP == 0
    groups = NA_WIDTH // LANES
    qkv = na_qkv.reshape(batch, seq, 3 * NA_WIDTH)
    spec = lambda part: pl.BlockSpec((1, seq, LANES), lambda b, p: (b, 0, part * groups + p))
    return pl.pallas_call(
        functools.partial(_na_kernel, rows=rows),
        grid=(batch, groups),
        in_specs=[spec(0), spec(1), spec(2),
                  pl.BlockSpec((2, NA_KH, GRID_W, NA_KH * GRID_W), lambda b, p: (p, 0, 0, 0))],
        out_specs=pl.BlockSpec((1, seq, LANES), lambda b, p: (b, 0, p)),
        out_shape=jax.ShapeDtypeStruct((batch, seq, NA_WIDTH), BF16),
        compiler_params=pltpu.CompilerParams(dimension_semantics=("parallel", "arbitrary"),
                                             vmem_limit_bytes=VMEM_LIMIT),
        name="na",
    )(qkv, qkv, qkv, bias)


def _gla_constants(reverse):
    c = GLA_CHUNK
    t = np.arange(c)
    cum = (t[None, :] <= t[:, None]).astype(np.float32)
    blocks, masks = [cum], [np.eye(c, dtype=np.float32)]
    for lvl in range(GLA_LEVELS):
        m = 1 << lvl
        base = (t // (2 * m)) * (2 * m)
        upper = (t % (2 * m)) >= m
        s = t[None, :]
        q_rows = upper[:, None] & (s >= (base + m)[:, None]) & (s <= t[:, None])
        k_rows = (~upper)[:, None] & (s > t[:, None]) & (s <= (base + m - 1)[:, None])
        blocks.append((q_rows | k_rows).astype(np.float32))
        masks.append((upper[:, None] & (~upper)[None, :] & (base[:, None] == base[None, :])).astype(np.float32))
    if reverse:
        blocks = [b[::-1, ::-1] for b in blocks]
        masks = [mk[::-1, ::-1] for mk in masks]
    sums = np.concatenate(blocks, axis=0)
    mask = np.stack([np.concatenate([mk, mk], axis=0) for mk in masks])
    return jnp.asarray(sums, BF16), jnp.asarray(mask, F32)


def _gla_direction(q_ref, k_ref, v_ref, la_ref, sums_ref, mask_ref, state_ref, o_ref, last_row):
    c = GLA_CHUNK
    la = la_ref[0]
    la_hi = la.astype(BF16)
    la_lo = (la - la_hi.astype(F32)).astype(BF16)
    sums = _dot(sums_ref[...], jnp.concatenate([la_hi, la_lo], axis=1))
    sums = sums[:, :GLA_KEY_WIDTH] + sums[:, GLA_KEY_WIDTH:]
    cum = sums[:c]
    cum_last = cum[last_row:last_row + 1]
    q = q_ref[0].astype(F32) * (GLA_DK ** -0.5)
    k = k_ref[0].astype(F32)
    v = v_ref[0]
    lane = lax.broadcasted_iota(jnp.int32, (c, LANES), 1)
    first = lane < GLA_DK

    def stack_heads(x):
        zero = jnp.zeros_like(x)
        return jnp.concatenate([jnp.where(first, x, zero), jnp.where(first, zero, x)], axis=0)

    q_dec = (q * jnp.exp(cum)).astype(BF16)
    k_dec = (k * jnp.exp(cum_last - cum)).astype(BF16)
    decay = jnp.exp(cum_last)
    for pair in range(GLA_HEADS // 2):
        ks = slice(pair * LANES, (pair + 1) * LANES)
        vs = slice(pair * 2 * GLA_DV, (pair + 1) * 2 * GLA_DV)
        scores = mask_ref[0] * _dot_nt(stack_heads(q[:, ks].astype(BF16)), k[:, ks].astype(BF16))
        for lvl in range(GLA_LEVELS):
            e = jnp.exp(sums[(lvl + 1) * c:(lvl + 2) * c, ks])
            scores += mask_ref[lvl + 1] * _dot_nt(stack_heads((q[:, ks] * e).astype(BF16)),
                                                  (k[:, ks] * e).astype(BF16))
        state = state_ref[pair]
        o = _dot(scores.astype(BF16), v[:, vs]) + _dot_nt(stack_heads(q_dec[:, ks]), state.astype(BF16))
        o_ref[0, :, pair * 2 * GLA_DV:pair * 2 * GLA_DV + GLA_DV] = o[:c, :GLA_DV]
        o_ref[0, :, pair * 2 * GLA_DV + GLA_DV:(pair + 1) * 2 * GLA_DV] = o[c:, GLA_DV:]
        upd = _dot_tn(v[:, vs], k_dec[:, ks])
        row = lax.broadcasted_iota(jnp.int32, upd.shape, 0)
        col = lax.broadcasted_iota(jnp.int32, upd.shape, 1)
        own = (row < GLA_DV) == (col < GLA_DK)
        state_ref[pair] = state * decay[:, ks] + jnp.where(own, upd, 0.0)


def _gla_kernel(qf_ref, kf_ref, vf_ref, laf_ref, qb_ref, kb_ref, vb_ref, lab_ref,
                sums_f_ref, mask_f_ref, sums_b_ref, mask_b_ref, of_ref, ob_ref, state_f_ref, state_b_ref):
    @pl.when(pl.program_id(1) == 0)
    def _():
        state_f_ref[...] = jnp.zeros_like(state_f_ref)
        state_b_ref[...] = jnp.zeros_like(state_b_ref)

    _gla_direction(qf_ref, kf_ref, vf_ref, laf_ref, sums_f_ref, mask_f_ref, state_f_ref, of_ref, GLA_CHUNK - 1)
    _gla_direction(qb_ref, kb_ref, vb_ref, lab_ref, sums_b_ref, mask_b_ref, state_b_ref, ob_ref, 0)


def _gla_call(gla, loga, *, batch, seq):
    c = GLA_CHUNK
    assert seq % c == 0
    n = seq // c
    gla3 = gla.reshape(batch, seq, 3 * GLA_VAL_WIDTH)
    loga3 = loga.reshape(batch, seq, 2 * GLA_KEY_WIDTH)
    fwd = lambda b, i: i
    bwd = lambda b, i: n - 1 - i
    sums_f, mask_f = _gla_constants(False)
    sums_b, mask_b = _gla_constants(True)

    def specs(chunk):
        return [pl.BlockSpec((1, c, GLA_KEY_WIDTH), lambda b, i: (b, chunk(b, i), 0)),
                pl.BlockSpec((1, c, GLA_KEY_WIDTH), lambda b, i: (b, chunk(b, i), 1)),
                pl.BlockSpec((1, c, GLA_VAL_WIDTH), lambda b, i: (b, chunk(b, i), 1))]

    la_spec = lambda chunk, part: pl.BlockSpec((1, c, GLA_KEY_WIDTH), lambda b, i: (b, chunk(b, i), part))
    out_spec = lambda chunk: pl.BlockSpec((1, c, GLA_VAL_WIDTH), lambda b, i: (b, chunk(b, i), 0))
    state = pltpu.VMEM((GLA_HEADS // 2, 2 * GLA_DV, LANES), F32)
    return pl.pallas_call(
        _gla_kernel,
        grid=(batch, n),
        in_specs=specs(fwd) + [la_spec(fwd, 0)] + specs(bwd) + [la_spec(bwd, 1)]
        + [_const_spec(a.shape) for a in (sums_f, mask_f, sums_b, mask_b)],
        out_specs=[out_spec(fwd), out_spec(bwd)],
        out_shape=[jax.ShapeDtypeStruct((batch, seq, GLA_VAL_WIDTH), F32)] * 2,
        scratch_shapes=[state, state],
        compiler_params=pltpu.CompilerParams(dimension_semantics=("parallel", "arbitrary"),
                                             vmem_limit_bytes=VMEM_LIMIT),
        name="gla",
    )(gla3, gla3, gla3, loga3, gla3, gla3, gla3, loga3, sums_f, mask_f, sums_b, mask_b)


def _merge_kernel(x_ref, nao_ref, of_ref, ob_ref, g_ref, gate_ref, gng_ref, wna_ref, wgla_ref, wout_ref, h_ref):
    na_out = _dot(nao_ref[...], wna_ref[...])
    o = of_ref[...] + ob_ref[...]
    heads = []
    for h in range(GLA_HEADS):
        heads.append(_rms(o[:, h * GLA_DV:(h + 1) * GLA_DV]) * gng_ref[...])
    g = g_ref[...].astype(F32)
    gla_o = jnp.concatenate(heads, axis=1) * (g * _sigmoid(g))
    gla_out = _dot(gla_o.astype(BF16), wgla_ref[...])
    gate = gate_ref[...].astype(F32)
    merged = gate[:, :D_MODEL] * na_out + gate[:, D_MODEL:] * gla_out
    h_ref[...] = x_ref[...] + _dot(merged.astype(BF16), wout_ref[...])


def _merge_call(x2, na_o, o_f, o_b, gla, gate, gng, wna, wgla, wout, *, tm):
    n = x2.shape[0]
    row = lambda w, j=0: pl.BlockSpec((tm, w), lambda i: (i, j))
    consts = (gng, wna, wgla, wout)
    return pl.pallas_call(
        _merge_kernel,
        grid=(n // tm,),
        in_specs=[row(D_MODEL), row(NA_WIDTH), row(GLA_VAL_WIDTH), row(GLA_VAL_WIDTH),
                  row(GLA_VAL_WIDTH, 2), row(2 * D_MODEL)] + [_const_spec(c.shape) for c in consts],
        out_specs=row(D_MODEL),
        out_shape=jax.ShapeDtypeStruct((n, D_MODEL), F32),
        compiler_params=pltpu.CompilerParams(dimension_semantics=("parallel",), vmem_limit_bytes=VMEM_LIMIT),
        name="merge",
    )(x2, na_o, o_f, o_b, gla, gate, *consts)


def _mlp_kernel(h_ref, g_ref, wup_ref, wdown_ref, gf_ref, y_ref):
    h = h_ref[...]
    u = (_rms(h) * g_ref[...]).astype(BF16)
    hdn = jnp.square(jnp.maximum(_dot(u, wup_ref[...]), 0.0)).astype(BF16)
    h = h + _dot(hdn, wdown_ref[...])
    y_ref[...] = _rms(h) * gf_ref[...]


def _mlp_call(h, g, wup, wdown, gf, *, tm):
    n = h.shape[0]
    row = pl.BlockSpec((tm, D_MODEL), lambda i: (i, 0))
    consts = (g, wup, wdown, gf)
    return pl.pallas_call(
        _mlp_kernel,
        grid=(n // tm,),
        in_specs=[row, _const_spec(g.shape), _const_spec(wup.shape), _const_spec(wdown.shape), _const_spec(gf.shape)],
        out_specs=row,
        out_shape=jax.ShapeDtypeStruct((n, D_MODEL), F32),
        compiler_params=pltpu.CompilerParams(dimension_semantics=("parallel",), vmem_limit_bytes=VMEM_LIMIT),
        name="mlp",
    )(h, g, wup, wdown, gf)


def _prepare(norm_mix_g, w_in, b_in, na_rpb, gk_fwd_w, gk_fwd_b, gk_bwd_w, gk_bwd_b, gla_norm_g,
             w_br_na, w_br_gla, w_out, norm_mlp_g, w_up, w_down, norm_final_g):
    edges = np.concatenate([[0], np.cumsum(IN_WIDTHS)])
    wcols = lambda a, b: w_in[:, edges[a]:edges[b]]
    bcols = lambda a, b: b_in[None, edges[a]:edges[b]]
    r = GLA_GATE_RANK
    pad = LR_PAD - 2 * r
    wlr = jnp.pad(wcols(7, 9), ((0, 0), (0, pad)))
    blr = jnp.pad(bcols(7, 9), ((0, 0), (0, pad)))
    gkw = jnp.zeros((LR_PAD, 2 * GLA_KEY_WIDTH), F32)
    gkw = gkw.at[:r, :GLA_KEY_WIDTH].set(gk_fwd_w).at[r:2 * r, GLA_KEY_WIDTH:].set(gk_bwd_w)
    gkb = jnp.concatenate([gk_fwd_b, gk_bwd_b])[None]
    return dict(
        proj=(norm_mix_g[None], wcols(0, 3).astype(BF16), bcols(0, 3), wcols(3, 7).astype(BF16), bcols(3, 7),
              wcols(9, 11).astype(BF16), bcols(9, 11), wlr.astype(BF16), blr, gkw.astype(BF16), gkb),
        na_bias=_na_bias_table(na_rpb),
        merge=(gla_norm_g[None], w_br_na.astype(BF16), w_br_gla.astype(BF16), w_out.astype(BF16)),
        mlp=(norm_mlp_g[None], w_up.astype(BF16), w_down.astype(BF16), norm_final_g[None]),
    )


def _trunk(x, prep, *, tm):
    batch, seq, _ = x.shape
    x2 = x.reshape(batch * seq, D_MODEL)
    na_qkv, gla, gate, loga = _proj_call(x2, *prep["proj"], tm=tm)
    na_o = _na_call(na_qkv, prep["na_bias"], batch=batch, seq=seq)
    o_f, o_b = _gla_call(gla, loga, batch=batch, seq=seq)
    h = _merge_call(x2, na_o.reshape(batch * seq, NA_WIDTH), o_f.reshape(batch * seq, GLA_VAL_WIDTH),
                    o_b.reshape(batch * seq, GLA_VAL_WIDTH), gla, gate, *prep["merge"], tm=tm)
    y = _mlp_call(h, *prep["mlp"], tm=tm)
    return y.reshape(batch, seq, D_MODEL)


def kernel(x_prompt, x_sample, norm_mix_g, w_in, b_in, na_rpb, gk_fwd_w, gk_fwd_b, gk_bwd_w, gk_bwd_b, gla_norm_g, w_br_na, w_br_gla, w_out, norm_mlp_g, w_up, w_down, norm_final_g):
    assert norm_mix_g.shape[0] == 1, "single-layer trunk"
    prep = _prepare(norm_mix_g[0], w_in[0], b_in[0], na_rpb[0], gk_fwd_w[0], gk_fwd_b[0], gk_bwd_w[0],
                    gk_bwd_b[0], gla_norm_g[0], w_br_na[0], w_br_gla[0], w_out[0], norm_mlp_g[0], w_up[0],
                    w_down[0], norm_final_g)
    return _trunk(x_prompt, prep, tm=512), _trunk(x_sample, prep, tm=512)
```

```python
import functools

import numpy as np
import jax
import jax.numpy as jnp
from jax import lax
from jax.experimental import pallas as pl
from jax.experimental.pallas import tpu as pltpu

D_MODEL = 1024
GRID_W = 64
NA_HEADS = 8
NA_HEAD_DIM = 64
NA_WIDTH = NA_HEADS * NA_HEAD_DIM
NA_KH = 8
NA_KW = 16
GLA_HEADS = 4
GLA_KEY_WIDTH = 256
GLA_VAL_WIDTH = 512
GLA_DK = 64
GLA_DV = 128
GLA_GATE_RANK = 16
GLA_GATE_NORMALIZER = 16.0
D_FF = 4 * D_MODEL
RMS_EPS = 1e-6
IN_WIDTHS = (NA_WIDTH, NA_WIDTH, NA_WIDTH, GLA_KEY_WIDTH, GLA_KEY_WIDTH, GLA_VAL_WIDTH, GLA_VAL_WIDTH,
             GLA_GATE_RANK, GLA_GATE_RANK, D_MODEL, D_MODEL)

LANES = 128
GLA_CHUNK = 128
GLA_LEVELS = 7
GLA_FINE_LEVELS = 3
LR_PAD = LANES
NA_ROWS_PER_STEP = 8
MERGE_SUBTILES = 2
VMEM_LIMIT = 56 * 1024 * 1024

BF16 = jnp.bfloat16
F32 = jnp.float32


def _dot(a, b):
    return jnp.dot(a, b, preferred_element_type=F32)


def _dot_nt(a, b):
    return lax.dot_general(a, b, (((1,), (1,)), ((), ())), preferred_element_type=F32)


def _sigmoid(z):
    return 1.0 / (1.0 + jnp.exp(-z))


def _rms(x):
    return x * lax.rsqrt(jnp.mean(x * x, axis=-1, keepdims=True) + RMS_EPS)


def _proj_kernel(x_ref, g_ref, wna_ref, bna_ref, wgla_ref, bgla_ref, wgate_ref, bgate_ref,
                 wlr_ref, blr_ref, gkw_ref, gkb_ref, na_ref, gla_ref, gate_ref, loga_ref):
    xb = (_rms(x_ref[...]) * g_ref[...]).astype(BF16)
    na_ref[...] = (_dot(xb, wna_ref[...]) + bna_ref[...]).astype(BF16)
    gla_ref[...] = (_dot(xb, wgla_ref[...]) + bgla_ref[...]).astype(BF16)
    gate_ref[...] = _sigmoid(_dot(xb, wgate_ref[...]) + bgate_ref[...]).astype(BF16)
    lr = (_dot(xb, wlr_ref[...]) + blr_ref[...]).astype(BF16)
    z = _dot(lr, gkw_ref[...]) + gkb_ref[...]
    log_sig = jnp.minimum(z, 0.0) - jnp.log1p(jnp.exp(-jnp.abs(z)))
    loga_ref[...] = log_sig * (1.0 / GLA_GATE_NORMALIZER)


def _const_spec(shape):
    nd = len(shape)
    return pl.BlockSpec(shape, lambda *_: (0,) * nd, pipeline_mode=pl.Buffered(1))


def _proj_call(x2, g, wna, bna, wgla, bgla, wgate, bgate, wlr, blr, gkw, gkb, *, tm):
    n = x2.shape[0]
    row = lambda w: pl.BlockSpec((tm, w), lambda i: (i, 0))
    consts = (g, wna, bna, wgla, bgla, wgate, bgate, wlr, blr, gkw, gkb)
    return pl.pallas_call(
        _proj_kernel,
        grid=(n // tm,),
        in_specs=[row(D_MODEL)] + [_const_spec(c.shape) for c in consts],
        out_specs=[row(3 * NA_WIDTH), row(3 * GLA_VAL_WIDTH), row(2 * D_MODEL), row(2 * GLA_KEY_WIDTH)],
        out_shape=[jax.ShapeDtypeStruct((n, 3 * NA_WIDTH), BF16),
                   jax.ShapeDtypeStruct((n, 3 * GLA_VAL_WIDTH), BF16),
                   jax.ShapeDtypeStruct((n, 2 * D_MODEL), BF16),
                   jax.ShapeDtypeStruct((n, 2 * GLA_KEY_WIDTH), F32)],
        compiler_params=pltpu.CompilerParams(dimension_semantics=("parallel",), vmem_limit_bytes=VMEM_LIMIT),
        name="proj",
    )(x2, *consts)


def _na_bias_table(rpb):
    c = np.arange(GRID_W)
    col_start = np.clip(c - NA_KW // 2, 0, GRID_W - NA_KW)
    col_mask = (c[None, :] >= col_start[:, None]) & (c[None, :] < col_start[:, None] + NA_KW)
    dc_idx = np.clip(c[None, :] - c[:, None], -(NA_KW - 1), NA_KW - 1) + (NA_KW - 1)
    onehot = (dc_idx[None] == np.arange(2 * NA_KW - 1)[:, None, None]) & col_mask[None]
    cols = jnp.einsum('hrd,dqk->hrqk', rpb.astype(F32), jnp.asarray(onehot, F32),
                      precision=lax.Precision.HIGHEST)
    cols = jnp.where(col_mask[None, None], cols, -jnp.inf)
    bias = jnp.stack([cols[:, NA_KH - 1 - off:2 * NA_KH - 1 - off] for off in range(NA_KH)], axis=1)
    bias = bias.transpose(0, 1, 3, 2, 4)
    return bias.reshape(rpb.shape[0], NA_KH, GRID_W, NA_KH * GRID_W)


def _na_kernel(q_ref, k_ref, v_ref, bias_ref, o_ref, *, rows):
    lane = lax.broadcasted_iota(jnp.int32, (GRID_W, LANES), 1)
    first = lane < NA_HEAD_DIM
    win = NA_KH * GRID_W

    def window(r):
        start = jnp.clip(r - NA_KH // 2, 0, rows - NA_KH)
        return pl.ds(pl.multiple_of(start * GRID_W, GRID_W), win), r - start

    def scores(r):
        keys, off = window(r)
        q = q_ref[0, pl.ds(pl.multiple_of(r * GRID_W, GRID_W), GRID_W), :] * (NA_HEAD_DIM ** -0.5)
        zero = jnp.zeros_like(q)
        qs = jnp.concatenate([jnp.where(first, q, zero), jnp.where(first, zero, q)], axis=0)
        return _dot_nt(qs, k_ref[0, keys, :]) + bias_ref[:, off].reshape(2 * GRID_W, win)

    def attend(r, s):
        keys, _ = window(r)
        p = jnp.exp(s - jnp.max(s, axis=-1, keepdims=True))
        inv = 1.0 / jnp.sum(p, axis=-1, keepdims=True)
        o = _dot(p.astype(BF16), v_ref[0, keys, :]) * inv
        o_ref[0, pl.ds(pl.multiple_of(r * GRID_W, GRID_W), GRID_W), :] = jnp.where(
            first, o[:GRID_W], o[GRID_W:]).astype(BF16)

    def body(i, carry):
        rs = [i * NA_ROWS_PER_STEP + u for u in range(NA_ROWS_PER_STEP)]
        ss = [scores(r) for r in rs]
        for r, s in zip(rs, ss):
            attend(r, s)
        return carry

    lax.fori_loop(0, rows // NA_ROWS_PER_STEP, body, 0)


def _na_call(na_qkv, bias, *, batch, seq):
    rows = seq // GRID_W
    assert rows >= NA_KH and seq % GRID_W == 0 and rows % NA_ROWS_PER_STEP == 0
    groups = NA_WIDTH // LANES
    qkv = na_qkv.reshape(batch, seq, 3 * NA_WIDTH)
    spec = lambda part: pl.BlockSpec((1, seq, LANES), lambda b, p: (b, 0, part * groups + p))
    return pl.pallas_call(
        functools.partial(_na_kernel, rows=rows),
        grid=(batch, groups),
        in_specs=[spec(0), spec(1), spec(2),
                  pl.BlockSpec((2, NA_KH, GRID_W, NA_KH * GRID_W), lambda b, p: (p, 0, 0, 0))],
        out_specs=pl.BlockSpec((1, seq, LANES), lambda b, p: (b, 0, p)),
        out_shape=jax.ShapeDtypeStruct((batch, seq, NA_WIDTH), BF16),
        compiler_params=pltpu.CompilerParams(dimension_semantics=("parallel", "arbitrary"),
                                             vmem_limit_bytes=VMEM_LIMIT),
        name="na",
    )(qkv, qkv, qkv, bias)


def _gla_constants(reverse):
    c = GLA_CHUNK
    t = np.arange(c)
    flip = (lambda a: a[::-1, ::-1]) if reverse else (lambda a: a)
    sums = [flip((t[None, :] <= t[:, None]).astype(np.float32))]
    fine = [np.eye(c, dtype=np.float32)]
    coarse = []
    for lvl in range(GLA_LEVELS):
        m = 1 << lvl
        base = (t // (2 * m)) * (2 * m)
        upper = (t % (2 * m)) >= m
        s = t[None, :]
        mask = flip((upper[:, None] & (~upper)[None, :] & (base[:, None] == base[None, :])).astype(np.float32))
        if lvl < GLA_FINE_LEVELS:
            q_rows = upper[:, None] & (s >= (base + m)[:, None]) & (s <= t[:, None])
            k_rows = (~upper)[:, None] & (s > t[:, None]) & (s <= (base + m - 1)[:, None])
            sums.append(flip((q_rows | k_rows).astype(np.float32)))
            fine.append(mask)
        else:
            coarse.append(mask[mask.any(axis=1)])
    two_heads = lambda mk: np.concatenate([mk, mk], axis=0)
    return (jnp.asarray(np.concatenate(sums, axis=0), BF16),
            jnp.asarray(np.stack([two_heads(mk) for mk in fine]), F32),
            jnp.asarray(np.stack([two_heads(mk) for mk in coarse]), F32))


def _stack_heads(x):
    first = lax.broadcasted_iota(jnp.int32, x.shape, 1) < GLA_DK
    zero = jnp.zeros_like(x)
    return jnp.concatenate([jnp.where(first, x, zero), jnp.where(first, zero, x)], axis=0)


class _GlaDirection:
    def __init__(self, q_ref, k_ref, v_ref, la_ref, sums_ref, fine_ref, coarse_ref, state_ref, o_ref, reverse):
        self.refs = (fine_ref, coarse_ref, state_ref, o_ref)
        self.reverse = reverse
        c, w = GLA_CHUNK, GLA_KEY_WIDTH
        la = la_ref[0]
        la_hi = la.astype(BF16)
        la_lo = (la - la_hi.astype(F32)).astype(BF16)
        sums = _dot(sums_ref[...], jnp.concatenate([la_hi, la_lo], axis=1))
        sums = sums[:, :w] + sums[:, w:]
        cum = sums[:c]
        last_row = 0 if reverse else c - 1
        cum_last = cum[last_row:last_row + 1]
        q = q_ref[0].astype(F32) * (GLA_DK ** -0.5)
        k = k_ref[0].astype(F32)
        self.v = v_ref[0]
        self.q_lvl = [q.astype(BF16)]
        self.k_lvl = [k.astype(BF16)]
        for lvl in range(GLA_FINE_LEVELS):
            e = jnp.exp(sums[(lvl + 1) * c:(lvl + 2) * c])
            self.q_lvl.append((q * e).astype(BF16))
            self.k_lvl.append((k * e).astype(BF16))
        for lvl in range(GLA_FINE_LEVELS, GLA_LEVELS):
            m = 1 << lvl
            split = lambda x: x.reshape(c // (2 * m), 2 * m, w)
            cum3, q3, k3 = split(cum), split(q), split(k)
            lo, hi = slice(0, m), slice(m, 2 * m)
            qh, kh = (lo, hi) if reverse else (hi, lo)
            ref = cum3[:, m:m + 1] if reverse else cum3[:, m - 1:m]
            self.q_lvl.append((q3[:, qh] * jnp.exp(cum3[:, qh] - ref)).reshape(c // 2, w).astype(BF16))
            k_scaled = k3[:, kh] * jnp.exp(ref - cum3[:, kh])
            k_parts = [k3[:, lo], k_scaled] if reverse else [k_scaled, k3[:, hi]]
            self.k_lvl.append(jnp.concatenate(k_parts, axis=1).reshape(c, w).astype(BF16))
        self.q_dec = (q * jnp.exp(cum)).astype(BF16)
        self.k_dec = (k * jnp.exp(cum_last - cum)).astype(BF16)
        self.decay = jnp.exp(cum_last)

    def scores(self, pair):
        fine_ref, coarse_ref, _, _ = self.refs
        c = GLA_CHUNK
        ks = slice(pair * LANES, (pair + 1) * LANES)
        level = lambda lvl: _dot_nt(_stack_heads(self.q_lvl[lvl][:, ks]), self.k_lvl[lvl][:, ks])
        scores = fine_ref[0] * level(0)
        for lvl in range(GLA_FINE_LEVELS):
            scores += fine_ref[lvl + 1] * level(lvl + 1)
        for lvl in range(GLA_FINE_LEVELS, GLA_LEVELS):
            m = 1 << lvl
            x = (coarse_ref[lvl - GLA_FINE_LEVELS] * level(lvl + 1)).reshape(2 * c // (2 * m), m, c)
            zero = jnp.zeros_like(x)
            scores += jnp.concatenate([x, zero] if self.reverse else [zero, x], axis=1).reshape(2 * c, c)
        return scores.astype(BF16)

    def update_state(self, pair):
        _, _, state_ref, _ = self.refs
        ks = slice(pair * LANES, (pair + 1) * LANES)
        vs = slice(pair * 2 * GLA_DV, (pair + 1) * 2 * GLA_DV)
        state = state_ref[pair]
        v_t = self.v[:, vs].T
        upd = _dot(v_t, self.k_dec[:, ks])
        row = lax.broadcasted_iota(jnp.int32, upd.shape, 0)
        col = lax.broadcasted_iota(jnp.int32, upd.shape, 1)
        own = (row < GLA_DV) == (col < GLA_DK)
        state_ref[pair] = state * self.decay[:, ks] + jnp.where(own, upd, 0.0)
        return jnp.concatenate([v_t, state.astype(BF16)], axis=1)

    def output(self, pair, scores, rhs):
        _, _, _, o_ref = self.refs
        c = GLA_CHUNK
        ks = slice(pair * LANES, (pair + 1) * LANES)
        lhs = jnp.concatenate([scores, _stack_heads(self.q_dec[:, ks])], axis=1)
        o = _dot_nt(lhs, rhs)
        o_ref[0, :, pair * 2 * GLA_DV:pair * 2 * GLA_DV + GLA_DV] = o[:c, :GLA_DV]
        o_ref[0, :, pair * 2 * GLA_DV + GLA_DV:(pair + 1) * 2 * GLA_DV] = o[c:, GLA_DV:]


def _gla_kernel(qf_ref, kf_ref, vf_ref, laf_ref, qb_ref, kb_ref, vb_ref, lab_ref,
                sums_f_ref, fine_f_ref, coarse_f_ref, sums_b_ref, fine_b_ref, coarse_b_ref,
                of_ref, ob_ref, state_f_ref, state_b_ref):
    @pl.when(pl.program_id(1) == 0)
    def _():
        state_f_ref[...] = jnp.zeros_like(state_f_ref)
        state_b_ref[...] = jnp.zeros_like(state_b_ref)

    dirs = (_GlaDirection(qf_ref, kf_ref, vf_ref, laf_ref, sums_f_ref, fine_f_ref, coarse_f_ref,
                          state_f_ref, of_ref, False),
            _GlaDirection(qb_ref, kb_ref, vb_ref, lab_ref, sums_b_ref, fine_b_ref, coarse_b_ref,
                          state_b_ref, ob_ref, True))
    units = [(d, pair) for d in dirs for pair in range(GLA_HEADS // 2)]
    rhs = [d.update_state(pair) for d, pair in units]
    scores = [d.scores(pair) for d, pair in units]
    for (d, pair), s, r in zip(units, scores, rhs):
        d.output(pair, s, r)


def _gla_call(gla, loga, *, batch, seq):
    c = GLA_CHUNK
    assert seq % c == 0
    n = seq // c
    gla3 = gla.reshape(batch, seq, 3 * GLA_VAL_WIDTH)
    loga3 = loga.reshape(batch, seq, 2 * GLA_KEY_WIDTH)
    fwd = lambda b, i: i
    bwd = lambda b, i: n - 1 - i
    consts = _gla_constants(False) + _gla_constants(True)

    def specs(chunk):
        return [pl.BlockSpec((1, c, GLA_KEY_WIDTH), lambda b, i: (b, chunk(b, i), 0)),
                pl.BlockSpec((1, c, GLA_KEY_WIDTH), lambda b, i: (b, chunk(b, i), 1)),
                pl.BlockSpec((1, c, GLA_VAL_WIDTH), lambda b, i: (b, chunk(b, i), 1))]

    la_spec = lambda chunk, part: pl.BlockSpec((1, c, GLA_KEY_WIDTH), lambda b, i: (b, chunk(b, i), part))
    out_spec = lambda chunk: pl.BlockSpec((1, c, GLA_VAL_WIDTH), lambda b, i: (b, chunk(b, i), 0))
    state = pltpu.VMEM((GLA_HEADS // 2, 2 * GLA_DV, LANES), F32)
    return pl.pallas_call(
        _gla_kernel,
        grid=(batch, n),
        in_specs=specs(fwd) + [la_spec(fwd, 0)] + specs(bwd) + [la_spec(bwd, 1)]
        + [_const_spec(a.shape) for a in consts],
        out_specs=[out_spec(fwd), out_spec(bwd)],
        out_shape=[jax.ShapeDtypeStruct((batch, seq, GLA_VAL_WIDTH), F32)] * 2,
        scratch_shapes=[state, state],
        compiler_params=pltpu.CompilerParams(dimension_semantics=("parallel", "arbitrary"),
                                             vmem_limit_bytes=VMEM_LIMIT),
        name="gla",
    )(gla3, gla3, gla3, loga3, gla3, gla3, gla3, loga3, *consts)


def _merge_kernel(x_ref, nao_ref, of_ref, ob_ref, g_ref, gate_ref, gng_ref, wna_ref, wgla_ref, wout_ref, h_ref):
    sub = x_ref.shape[0] // MERGE_SUBTILES
    for i in range(MERGE_SUBTILES):
        rows = slice(i * sub, (i + 1) * sub)
        na_out = _dot(nao_ref[rows, :], wna_ref[...])
        o = of_ref[rows, :] + ob_ref[rows, :]
        heads = []
        for h in range(GLA_HEADS):
            heads.append(_rms(o[:, h * GLA_DV:(h + 1) * GLA_DV]) * gng_ref[...])
        g = g_ref[rows, :].astype(F32)
        gla_o = jnp.concatenate(heads, axis=1) * (g * _sigmoid(g))
        gla_out = _dot(gla_o.astype(BF16), wgla_ref[...])
        gate = gate_ref[rows, :].astype(F32)
        merged = gate[:, :D_MODEL] * na_out + gate[:, D_MODEL:] * gla_out
        h_ref[rows, :] = x_ref[rows, :] + _dot(merged.astype(BF16), wout_ref[...])


def _merge_call(x2, na_o, o_f, o_b, gla, gate, gng, wna, wgla, wout, *, tm):
    n = x2.shape[0]
    row = lambda w, j=0: pl.BlockSpec((tm, w), lambda i: (i, j))
    consts = (gng, wna, wgla, wout)
    return pl.pallas_call(
        _merge_kernel,
        grid=(n // tm,),
        in_specs=[row(D_MODEL), row(NA_WIDTH), row(GLA_VAL_WIDTH), row(GLA_VAL_WIDTH),
                  row(GLA_VAL_WIDTH, 2), row(2 * D_MODEL)] + [_const_spec(c.shape) for c in consts],
        out_specs=row(D_MODEL),
        out_shape=jax.ShapeDtypeStruct((n, D_MODEL), F32),
        compiler_params=pltpu.CompilerParams(dimension_semantics=("parallel",), vmem_limit_bytes=VMEM_LIMIT),
        name="merge",
    )(x2, na_o, o_f, o_b, gla, gate, *consts)


def _mlp_kernel(h_ref, g_ref, wup_ref, wdown_ref, gf_ref, y_ref):
    h = h_ref[...]
    u = (_rms(h) * g_ref[...]).astype(BF16)
    hdn = jnp.square(jnp.maximum(_dot(u, wup_ref[...]), 0.0)).astype(BF16)
    h = h + _dot(hdn, wdown_ref[...])
    y_ref[...] = _rms(h) * gf_ref[...]


def _mlp_call(h, g, wup, wdown, gf, *, tm):
    n = h.shape[0]
    row = pl.BlockSpec((tm, D_MODEL), lambda i: (i, 0))
    consts = (g, wup, wdown, gf)
    return pl.pallas_call(
        _mlp_kernel,
        grid=(n // tm,),
        in_specs=[row, _const_spec(g.shape), _const_spec(wup.shape), _const_spec(wdown.shape), _const_spec(gf.shape)],
        out_specs=row,
        out_shape=jax.ShapeDtypeStruct((n, D_MODEL), F32),
        compiler_params=pltpu.CompilerParams(dimension_semantics=("parallel",), vmem_limit_bytes=VMEM_LIMIT),
        name="mlp",
    )(h, g, wup, wdown, gf)


def _prepare(norm_mix_g, w_in, b_in, na_rpb, gk_fwd_w, gk_fwd_b, gk_bwd_w, gk_bwd_b, gla_norm_g,
             w_br_na, w_br_gla, w_out, norm_mlp_g, w_up, w_down, norm_final_g):
    edges = np.concatenate([[0], np.cumsum(IN_WIDTHS)])
    wcols = lambda a, b: w_in[:, edges[a]:edges[b]]
    bcols = lambda a, b: b_in[None, edges[a]:edges[b]]
    r = GLA_GATE_RANK
    pad = LR_PAD - 2 * r
    wlr = jnp.pad(wcols(7, 9), ((0, 0), (0, pad)))
    blr = jnp.pad(bcols(7, 9), ((0, 0), (0, pad)))
    gkw = jnp.zeros((LR_PAD, 2 * GLA_KEY_WIDTH), F32)
    gkw = gkw.at[:r, :GLA_KEY_WIDTH].set(gk_fwd_w).at[r:2 * r, GLA_KEY_WIDTH:].set(gk_bwd_w)
    gkb = jnp.concatenate([gk_fwd_b, gk_bwd_b])[None]
    return dict(
        proj=(norm_mix_g[None], wcols(0, 3).astype(BF16), bcols(0, 3), wcols(3, 7).astype(BF16), bcols(3, 7),
              wcols(9, 11).astype(BF16), bcols(9, 11), wlr.astype(BF16), blr, gkw.astype(BF16), gkb),
        na_bias=_na_bias_table(na_rpb),
        merge=(gla_norm_g[None], w_br_na.astype(BF16), w_br_gla.astype(BF16), w_out.astype(BF16)),
        mlp=(norm_mlp_g[None], w_up.astype(BF16), w_down.astype(BF16), norm_final_g[None]),
    )


def _trunk(x, prep, *, tm):
    batch, seq, _ = x.shape
    x2 = x.reshape(batch * seq, D_MODEL)
    na_qkv, gla, gate, loga = _proj_call(x2, *prep["proj"], tm=tm)
    na_o = _na_call(na_qkv, prep["na_bias"], batch=batch, seq=seq)
    o_f, o_b = _gla_call(gla, loga, batch=batch, seq=seq)
    h = _merge_call(x2, na_o.reshape(batch * seq, NA_WIDTH), o_f.reshape(batch * seq, GLA_VAL_WIDTH),
                    o_b.reshape(batch * seq, GLA_VAL_WIDTH), gla, gate, *prep["merge"], tm=tm)
    y = _mlp_call(h, *prep["mlp"], tm=tm)
    return y.reshape(batch, seq, D_MODEL)


def kernel(x_prompt, x_sample, norm_mix_g, w_in, b_in, na_rpb, gk_fwd_w, gk_fwd_b, gk_bwd_w, gk_bwd_b, gla_norm_g, w_br_na, w_br_gla, w_out, norm_mlp_g, w_up, w_down, norm_final_g):
    assert norm_mix_g.shape[0] == 1, "single-layer trunk"
    prep = _prepare(norm_mix_g[0], w_in[0], b_in[0], na_rpb[0], gk_fwd_w[0], gk_fwd_b[0], gk_bwd_w[0],
                    gk_bwd_b[0], gla_norm_g[0], w_br_na[0], w_br_gla[0], w_out[0], norm_mlp_g[0], w_up[0],
                    w_down[0], norm_final_g)
    return _trunk(x_prompt, prep, tm=512), _trunk(x_sample, prep, tm=512)
```

```python
import functools

import numpy as np
import jax
import jax.numpy as jnp
from jax import lax
from jax.experimental import pallas as pl
from jax.experimental.pallas import tpu as pltpu

D_MODEL = 1024
GRID_W = 64
NA_HEADS = 8
NA_HEAD_DIM = 64
NA_WIDTH = NA_HEADS * NA_HEAD_DIM
NA_KH = 8
NA_KW = 16
GLA_HEADS = 4
GLA_KEY_WIDTH = 256
GLA_VAL_WIDTH = 512
GLA_DK = 64
GLA_DV = 128
GLA_GATE_RANK = 16
GLA_GATE_NORMALIZER = 16.0
D_FF = 4 * D_MODEL
RMS_EPS = 1e-6
IN_WIDTHS = (NA_WIDTH, NA_WIDTH, NA_WIDTH, GLA_KEY_WIDTH, GLA_KEY_WIDTH, GLA_VAL_WIDTH, GLA_VAL_WIDTH,
             GLA_GATE_RANK, GLA_GATE_RANK, D_MODEL, D_MODEL)

LANES = 128
BF16_ROWS = 16
GLA_CHUNK = 128
GLA_LEVELS = 7
GLA_CHUNKS_PER_STEP = 4
GLA_FINE_LEVELS = 3
LR_PAD = LANES
NA_COL_BLOCKS = GRID_W // NA_KW
NA_QGROUP = 8
NA_ROWS_PER_STEP = 16
MLP_SLICES = 4
VMEM_LIMIT = 56 * 1024 * 1024

BF16 = jnp.bfloat16
F32 = jnp.float32


def _dot(a, b):
    return jnp.dot(a, b, preferred_element_type=F32)


def _dot_nt(a, b):
    return lax.dot_general(a, b, (((1,), (1,)), ((), ())), preferred_element_type=F32)


def _sigmoid(z):
    return 1.0 / (1.0 + jnp.exp(-z))


def _rms(x):
    return x * lax.rsqrt(jnp.mean(x * x, axis=-1, keepdims=True) + RMS_EPS)


def _proj_kernel(x_ref, g_ref, wq_ref, bq_ref, wkv_ref, bkv_ref, wgla_ref, bgla_ref, wgate_ref, bgate_ref,
                 wlr_ref, blr_ref, gkw_ref, gkb_ref, q_ref, kv_ref, gla_ref, gate_ref, loga_ref):
    xb = (_rms(x_ref[...]) * g_ref[...]).astype(BF16)
    q_ref[...] = (_dot(xb, wq_ref[...]) + bq_ref[...]).astype(BF16)
    kv = (_dot(xb, wkv_ref[...]) + bkv_ref[...]).astype(BF16)
    kv = kv.reshape(kv.shape[0] // GRID_W, NA_COL_BLOCKS, NA_KW, kv.shape[1])
    for cb in range(NA_COL_BLOCKS):
        kv_ref[0, cb] = kv[:, cb].reshape(-1, kv.shape[3])
    gla_ref[...] = (_dot(xb, wgla_ref[...]) + bgla_ref[...]).astype(BF16)
    gate_ref[...] = _sigmoid(_dot(xb, wgate_ref[...]) + bgate_ref[...]).astype(BF16)
    lr = (_dot(xb, wlr_ref[...]) + blr_ref[...]).astype(BF16)
    z = _dot(lr, gkw_ref[...]) + gkb_ref[...]
    log_sig = jnp.minimum(z, 0.0) - jnp.log1p(jnp.exp(-jnp.abs(z)))
    loga_ref[...] = log_sig * (1.0 / GLA_GATE_NORMALIZER)


def _const_spec(shape):
    nd = len(shape)
    return pl.BlockSpec(shape, lambda *_: (0,) * nd, pipeline_mode=pl.Buffered(1))


def _proj_call(x2, g, wq, bq, wkv, bkv, wgla, bgla, wgate, bgate, wlr, blr, gkw, gkb, *, tm, batch, seq):
    n = x2.shape[0]
    assert seq % tm == 0 and tm % GRID_W == 0
    tiles = seq // tm
    row = lambda w: pl.BlockSpec((tm, w), lambda i: (i, 0))
    consts = (g, wq, bq, wkv, bkv, wgla, bgla, wgate, bgate, wlr, blr, gkw, gkb)
    kv_spec = pl.BlockSpec((1, NA_COL_BLOCKS, tm // NA_COL_BLOCKS, 2 * NA_WIDTH),
                           lambda i: (i // tiles, 0, i % tiles, 0))
    return pl.pallas_call(
        _proj_kernel,
        grid=(n // tm,),
        in_specs=[row(D_MODEL)] + [_const_spec(c.shape) for c in consts],
        out_specs=[row(NA_WIDTH), kv_spec, row(3 * GLA_VAL_WIDTH), row(2 * D_MODEL), row(2 * GLA_KEY_WIDTH)],
        out_shape=[jax.ShapeDtypeStruct((n, NA_WIDTH), BF16),
                   jax.ShapeDtypeStruct((batch, NA_COL_BLOCKS, seq // NA_COL_BLOCKS, 2 * NA_WIDTH), BF16),
                   jax.ShapeDtypeStruct((n, 3 * GLA_VAL_WIDTH), BF16),
                   jax.ShapeDtypeStruct((n, 2 * D_MODEL), BF16),
                   jax.ShapeDtypeStruct((n, 2 * GLA_KEY_WIDTH), F32)],
        compiler_params=pltpu.CompilerParams(dimension_semantics=("parallel",), vmem_limit_bytes=VMEM_LIMIT),
        name="proj",
    )(x2, *consts)


def _na_needed_blocks():
    need = []
    for g in range(GRID_W // NA_QGROUP):
        c = np.arange(g * NA_QGROUP, (g + 1) * NA_QGROUP)
        start = np.clip(c - NA_KW // 2, 0, GRID_W - NA_KW)
        need.append(sorted(set((start // NA_KW).tolist()) | set(((start + NA_KW - 1) // NA_KW).tolist())))
    return need


NA_NEED = _na_needed_blocks()
NA_HALF_ROWS = []
for _half in range(2):
    _groups = [g for g, blocks in enumerate(NA_NEED) if any(b // 2 == _half for b in blocks)]
    _lo = (min(_groups) * NA_QGROUP) // BF16_ROWS * BF16_ROWS
    _hi = -(-((max(_groups) + 1) * NA_QGROUP) // BF16_ROWS) * BF16_ROWS
    NA_HALF_ROWS.append((_lo, _hi))


def _na_bias_table(rpb):
    c = np.arange(GRID_W)
    col_start = np.clip(c - NA_KW // 2, 0, GRID_W - NA_KW)
    col_mask = (c[None, :] >= col_start[:, None]) & (c[None, :] < col_start[:, None] + NA_KW)
    dc_idx = np.clip(c[None, :] - c[:, None], -(NA_KW - 1), NA_KW - 1) + (NA_KW - 1)
    onehot = (dc_idx[None] == np.arange(2 * NA_KW - 1)[:, None, None]) & col_mask[None]
    cols = jnp.einsum('hrd,dqk->hrqk', rpb.astype(F32), jnp.asarray(onehot, F32),
                      precision=lax.Precision.HIGHEST)
    cols = jnp.where(col_mask[None, None], cols, -jnp.inf)
    bias = jnp.stack([cols[:, NA_KH - 1 - off:2 * NA_KH - 1 - off] for off in range(NA_KH)], axis=1)
    heads = rpb.shape[0]
    bias = bias.reshape(heads, NA_KH, NA_KH, GRID_W, NA_COL_BLOCKS, NA_KW)
    bias = bias.transpose(0, 1, 3, 4, 2, 5)
    return bias.reshape(heads, NA_KH, GRID_W, NA_KH * GRID_W)


def _na_kernel(q_ref, k_ref, v_ref, bias_ref, o_ref, *, rows):
    lane = lax.broadcasted_iota(jnp.int32, (GRID_W, LANES), 1)
    first = lane < NA_HEAD_DIM
    win = NA_KH * NA_KW
    groups = GRID_W // NA_QGROUP

    def window(r):
        start = jnp.clip(r - NA_KH // 2, 0, rows - NA_KH)
        return pl.ds(pl.multiple_of(start * NA_KW, NA_KW), win), r - start

    def tile(ref, keys, half):
        return jnp.concatenate([ref[0, 2 * half, keys, :], ref[0, 2 * half + 1, keys, :]], axis=0)

    def half_rows(x, half):
        lo, hi = NA_HALF_ROWS[half]
        return jnp.concatenate([x[lo:hi], x[GRID_W + lo:GRID_W + hi]], axis=0)

    def scores(r):
        keys, _ = window(r)
        q = q_ref[0, pl.ds(pl.multiple_of(r * GRID_W, GRID_W), GRID_W), :] * (NA_HEAD_DIM ** -0.5)
        zero = jnp.zeros_like(q)
        qs = jnp.concatenate([jnp.where(first, q, zero), jnp.where(first, zero, q)], axis=0)
        return [_dot_nt(half_rows(qs, half), tile(k_ref, keys, half)) for half in range(2)]

    def attend(r, s):
        keys, off = window(r)
        p_rows = [[], []]
        inv = []
        for head in range(2):
            for g in range(groups):
                rows8 = slice(g * NA_QGROUP, (g + 1) * NA_QGROUP)
                pieces = {}
                for cb in NA_NEED[g]:
                    half = cb // 2
                    lo, hi = NA_HALF_ROWS[half]
                    at = head * (hi - lo) + g * NA_QGROUP - lo
                    pieces[cb] = (s[half][at:at + NA_QGROUP, (cb % 2) * LANES:(cb % 2 + 1) * LANES]
                                  + bias_ref[head, off, rows8, cb * LANES:(cb + 1) * LANES])
                m = functools.reduce(jnp.maximum, pieces.values())
                m = jnp.max(m, axis=-1, keepdims=True)
                pieces = {cb: jnp.exp(x - m) for cb, x in pieces.items()}
                total = functools.reduce(jnp.add, pieces.values())
                inv.append(1.0 / jnp.sum(total, axis=-1, keepdims=True))
                for half in range(2):
                    lo, hi = NA_HALF_ROWS[half]
                    if lo <= g * NA_QGROUP < hi:
                        zero = jnp.zeros((NA_QGROUP, LANES), F32)
                        p_rows[half].append(jnp.concatenate(
                            [pieces.get(2 * half, zero), pieces.get(2 * half + 1, zero)], axis=1))
        outs = []
        for half in range(2):
            p = jnp.concatenate(p_rows[half], axis=0).astype(BF16)
            outs.append(_dot(p, tile(v_ref, keys, half)))
        heads_o = []
        for head in range(2):
            rows_o = []
            for g in range(groups):
                acc = None
                for half in range(2):
                    lo, hi = NA_HALF_ROWS[half]
                    if lo <= g * NA_QGROUP < hi:
                        at = head * (hi - lo) + g * NA_QGROUP - lo
                        part = outs[half][at:at + NA_QGROUP]
                        acc = part if acc is None else acc + part
                rows_o.append(acc * inv[head * groups + g])
            heads_o.append(jnp.concatenate(rows_o, axis=0))
        o_ref[0, pl.ds(pl.multiple_of(r * GRID_W, GRID_W), GRID_W), :] = jnp.where(
            first, heads_o[0], heads_o[1]).astype(BF16)

    def body(i, carry):
        rs = [i * NA_ROWS_PER_STEP + u for u in range(NA_ROWS_PER_STEP)]
        ss = [scores(r) for r in rs]
        for r, s in zip(rs, ss):
            attend(r, s)
        return carry

    lax.fori_loop(0, rows // NA_ROWS_PER_STEP, body, 0)


def _na_call(na_q, na_kv, bias, *, batch, seq):
    rows = seq // GRID_W
    assert rows >= NA_KH and seq % GRID_W == 0 and rows % NA_ROWS_PER_STEP == 0
    pairs = NA_WIDTH // LANES
    q3 = na_q.reshape(batch, seq, NA_WIDTH)
    kv_spec = lambda part: pl.BlockSpec((1, NA_COL_BLOCKS, seq // NA_COL_BLOCKS, LANES),
                                        lambda b, p: (b, 0, 0, part * pairs + p))
    return pl.pallas_call(
        functools.partial(_na_kernel, rows=rows),
        grid=(batch, pairs),
        in_specs=[pl.BlockSpec((1, seq, LANES), lambda b, p: (b, 0, p)), kv_spec(0), kv_spec(1),
                  pl.BlockSpec((2, NA_KH, GRID_W, NA_KH * GRID_W), lambda b, p: (p, 0, 0, 0))],
        out_specs=pl.BlockSpec((1, seq, LANES), lambda b, p: (b, 0, p)),
        out_shape=jax.ShapeDtypeStruct((batch, seq, NA_WIDTH), BF16),
        compiler_params=pltpu.CompilerParams(dimension_semantics=("parallel", "arbitrary"),
                                             vmem_limit_bytes=VMEM_LIMIT),
        name="na",
    )(q3, na_kv, na_kv, bias)


def _gla_constants(reverse):
    c = GLA_CHUNK
    t = np.arange(c)
    flip = (lambda a: a[::-1, ::-1]) if reverse else (lambda a: a)
    sums = [flip((t[None, :] <= t[:, None]).astype(np.float32))]
    fine = [np.eye(c, dtype=np.float32)]
    coarse = []
    for lvl in range(GLA_LEVELS):
        m = 1 << lvl
        base = (t // (2 * m)) * (2 * m)
        upper = (t % (2 * m)) >= m
        s = t[None, :]
        mask = flip((upper[:, None] & (~upper)[None, :] & (base[:, None] == base[None, :])).astype(np.float32))
        if lvl < GLA_FINE_LEVELS:
            q_rows = upper[:, None] & (s >= (base + m)[:, None]) & (s <= t[:, None])
            k_rows = (~upper)[:, None] & (s > t[:, None]) & (s <= (base + m - 1)[:, None])
            sums.append(flip((q_rows | k_rows).astype(np.float32)))
            fine.append(mask)
        else:
            coarse.append(mask[mask.any(axis=1)])
    two_heads = lambda mk: np.concatenate([mk, mk], axis=0)
    return (jnp.asarray(np.concatenate(sums, axis=0), BF16),
            jnp.asarray(np.stack([two_heads(mk) for mk in fine]), F32),
            jnp.asarray(np.stack([two_heads(mk) for mk in coarse]), F32))


def _stack_heads(x):
    first = lax.broadcasted_iota(jnp.int32, x.shape, 1) < GLA_DK
    zero = jnp.zeros_like(x)
    return jnp.concatenate([jnp.where(first, x, zero), jnp.where(first, zero, x)], axis=0)


class _GlaDirection:
    def __init__(self, q_ref, k_ref, v_ref, la_ref, sums_ref, fine_ref, coarse_ref, state_ref, o_ref, reverse,
                 chunk):
        self.refs = (fine_ref, coarse_ref, state_ref, o_ref)
        self.reverse = reverse
        c, w = GLA_CHUNK, GLA_KEY_WIDTH
        self.rows = rows = slice(chunk * c, (chunk + 1) * c)
        la = la_ref[0, rows, :]
        la_hi = la.astype(BF16)
        la_lo = (la - la_hi.astype(F32)).astype(BF16)
        sums = _dot(sums_ref[...], jnp.concatenate([la_hi, la_lo], axis=1))
        sums = sums[:, :w] + sums[:, w:]
        cum = sums[:c]
        last_row = 0 if reverse else c - 1
        cum_last = cum[last_row:last_row + 1]
        q = q_ref[0, rows, :].astype(F32) * (GLA_DK ** -0.5)
        k = k_ref[0, rows, :].astype(F32)
        self.v = v_ref[0, rows, :]
        self.q_lvl = [q.astype(BF16)]
        self.k_lvl = [k.astype(BF16)]
        for lvl in range(GLA_FINE_LEVELS):
            e = jnp.exp(sums[(lvl + 1) * c:(lvl + 2) * c])
            self.q_lvl.append((q * e).astype(BF16))
            self.k_lvl.append((k * e).astype(BF16))
        for lvl in range(GLA_FINE_LEVELS, GLA_LEVELS):
            m = 1 << lvl
            split = lambda x: x.reshape(c // (2 * m), 2 * m, w)
            cum3, q3, k3 = split(cum), split(q), split(k)
            lo, hi = slice(0, m), slice(m, 2 * m)
            qh, kh = (lo, hi) if reverse else (hi, lo)
            ref = cum3[:, m:m + 1] if reverse else cum3[:, m - 1:m]
            self.q_lvl.append((q3[:, qh] * jnp.exp(cum3[:, qh] - ref)).reshape(c // 2, w).astype(BF16))
            k_scaled = k3[:, kh] * jnp.exp(ref - cum3[:, kh])
            k_parts = [k3[:, lo], k_scaled] if reverse else [k_scaled, k3[:, hi]]
            self.k_lvl.append(jnp.concatenate(k_parts, axis=1).reshape(c, w).astype(BF16))
        self.q_dec = (q * jnp.exp(cum)).astype(BF16)
        self.k_dec = (k * jnp.exp(cum_last - cum)).astype(BF16)
        self.decay = jnp.exp(cum_last)

    def scores(self, pair):
        fine_ref, coarse_ref, _, _ = self.refs
        c = GLA_CHUNK
        ks = slice(pair * LANES, (pair + 1) * LANES)
        level = lambda lvl: _dot_nt(_stack_heads(self.q_lvl[lvl][:, ks]), self.k_lvl[lvl][:, ks])
        scores = fine_ref[0] * level(0)
        for lvl in range(GLA_FINE_LEVELS):
            scores += fine_ref[lvl + 1] * level(lvl + 1)
        for lvl in range(GLA_FINE_LEVELS, GLA_LEVELS):
            m = 1 << lvl
            x = (coarse_ref[lvl - GLA_FINE_LEVELS] * level(lvl + 1)).reshape(2 * c // (2 * m), m, c)
            zero = jnp.zeros_like(x)
            scores += jnp.concatenate([x, zero] if self.reverse else [zero, x], axis=1).reshape(2 * c, c)
        return scores.astype(BF16)

    def update_state(self, pair):
        _, _, state_ref, _ = self.refs
        ks = slice(pair * LANES, (pair + 1) * LANES)
        vs = slice(pair * 2 * GLA_DV, (pair + 1) * 2 * GLA_DV)
        state = state_ref[pair]
        v_t = self.v[:, vs].T
        upd = _dot(v_t, self.k_dec[:, ks])
        row = lax.broadcasted_iota(jnp.int32, upd.shape, 0)
        col = lax.broadcasted_iota(jnp.int32, upd.shape, 1)
        own = (row < GLA_DV) == (col < GLA_DK)
        state_ref[pair] = state * self.decay[:, ks] + jnp.where(own, upd, 0.0)
        return jnp.concatenate([v_t, state.astype(BF16)], axis=1)

    def output(self, pair, scores, rhs):
        _, _, _, o_ref = self.refs
        c = GLA_CHUNK
        ks = slice(pair * LANES, (pair + 1) * LANES)
        lhs = jnp.concatenate([scores, _stack_heads(self.q_dec[:, ks])], axis=1)
        o = _dot_nt(lhs, rhs)
        o_ref[0, self.rows, pair * 2 * GLA_DV:pair * 2 * GLA_DV + GLA_DV] = o[:c, :GLA_DV].astype(o_ref.dtype)
        o_ref[0, self.rows, pair * 2 * GLA_DV + GLA_DV:(pair + 1) * 2 * GLA_DV] = o[c:, GLA_DV:].astype(o_ref.dtype)


def _gla_kernel(qf_ref, kf_ref, vf_ref, laf_ref, qb_ref, kb_ref, vb_ref, lab_ref,
                sums_f_ref, fine_f_ref, coarse_f_ref, sums_b_ref, fine_b_ref, coarse_b_ref,
                of_ref, ob_ref, state_f_ref, state_b_ref):
    @pl.when(pl.program_id(1) == 0)
    def _():
        state_f_ref[...] = jnp.zeros_like(state_f_ref)
        state_b_ref[...] = jnp.zeros_like(state_b_ref)

    pairs = range(GLA_HEADS // 2)
    groups = []
    for j in range(GLA_CHUNKS_PER_STEP):
        groups.append((_GlaDirection(qf_ref, kf_ref, vf_ref, laf_ref, sums_f_ref, fine_f_ref, coarse_f_ref,
                                     state_f_ref, of_ref, False, j),
                       _GlaDirection(qb_ref, kb_ref, vb_ref, lab_ref, sums_b_ref, fine_b_ref, coarse_b_ref,
                                     state_b_ref, ob_ref, True, GLA_CHUNKS_PER_STEP - 1 - j)))
    scores = [[[d.scores(pair) for pair in pairs] for d in group] for group in groups]
    for group, group_scores in zip(groups, scores):
        rhs = [[d.update_state(pair) for pair in pairs] for d in group]
        for d, d_scores, d_rhs in zip(group, group_scores, rhs):
            for pair in pairs:
                d.output(pair, d_scores[pair], d_rhs[pair])


def _gla_call(gla, loga, *, batch, seq):
    c = GLA_CHUNK * GLA_CHUNKS_PER_STEP
    assert seq % c == 0
    n = seq // c
    gla3 = gla.reshape(batch, seq, 3 * GLA_VAL_WIDTH)
    loga3 = loga.reshape(batch, seq, 2 * GLA_KEY_WIDTH)
    fwd = lambda b, i: i
    bwd = lambda b, i: n - 1 - i
    consts = _gla_constants(False) + _gla_constants(True)

    def specs(chunk):
        return [pl.BlockSpec((1, c, GLA_KEY_WIDTH), lambda b, i: (b, chunk(b, i), 0)),
                pl.BlockSpec((1, c, GLA_KEY_WIDTH), lambda b, i: (b, chunk(b, i), 1)),
                pl.BlockSpec((1, c, GLA_VAL_WIDTH), lambda b, i: (b, chunk(b, i), 1))]

    la_spec = lambda chunk, part: pl.BlockSpec((1, c, GLA_KEY_WIDTH), lambda b, i: (b, chunk(b, i), part))
    out_spec = lambda chunk: pl.BlockSpec((1, c, GLA_VAL_WIDTH), lambda b, i: (b, chunk(b, i), 0))
    state = pltpu.VMEM((GLA_HEADS // 2, 2 * GLA_DV, LANES), F32)
    return pl.pallas_call(
        _gla_kernel,
        grid=(batch, n),
        in_specs=specs(fwd) + [la_spec(fwd, 0)] + specs(bwd) + [la_spec(bwd, 1)]
        + [_const_spec(a.shape) for a in consts],
        out_specs=[out_spec(fwd), out_spec(bwd)],
        out_shape=[jax.ShapeDtypeStruct((batch, seq, GLA_VAL_WIDTH), BF16)] * 2,
        scratch_shapes=[state, state],
        compiler_params=pltpu.CompilerParams(dimension_semantics=("parallel", "arbitrary"),
                                             vmem_limit_bytes=VMEM_LIMIT),
        name="gla",
    )(gla3, gla3, gla3, loga3, gla3, gla3, gla3, loga3, *consts)


def _tail_kernel(x_ref, nao_ref, of_ref, ob_ref, g_ref, gate_ref, gng_ref, wna_ref, wgla_ref, wout_ref,
                 gm_ref, wup_ref, wdown_ref, gf_ref, y_ref):
    na_out = _dot(nao_ref[...], wna_ref[...])
    o = of_ref[...].astype(F32) + ob_ref[...].astype(F32)
    heads = []
    for h in range(GLA_HEADS):
        heads.append(_rms(o[:, h * GLA_DV:(h + 1) * GLA_DV]) * gng_ref[...])
    g = g_ref[...].astype(F32)
    gla_o = jnp.concatenate(heads, axis=1) * (g * _sigmoid(g))
    gla_out = _dot(gla_o.astype(BF16), wgla_ref[...])
    gate = gate_ref[...].astype(F32)
    merged = gate[:, :D_MODEL] * na_out + gate[:, D_MODEL:] * gla_out
    h = x_ref[...] + _dot(merged.astype(BF16), wout_ref[...])
    u = (_rms(h) * gm_ref[...]).astype(BF16)
    width = D_FF // MLP_SLICES
    for j in range(MLP_SLICES):
        cols = slice(j * width, (j + 1) * width)
        hdn = jnp.square(jnp.maximum(_dot(u, wup_ref[:, cols]), 0.0)).astype(BF16)
        h = h + _dot(hdn, wdown_ref[cols, :])
    y_ref[...] = _rms(h) * gf_ref[...]


def _tail_call(x2, na_o, o_f, o_b, gla, gate, gng, wna, wgla, wout, gm, wup, wdown, gf, *, tm):
    n = x2.shape[0]
    row = lambda w, j=0: pl.BlockSpec((tm, w), lambda i: (i, j))
    consts = (gng, wna, wgla, wout, gm, wup, wdown, gf)
    return pl.pallas_call(
        _tail_kernel,
        grid=(n // tm,),
        in_specs=[row(D_MODEL), row(NA_WIDTH), row(GLA_VAL_WIDTH), row(GLA_VAL_WIDTH),
                  row(GLA_VAL_WIDTH, 2), row(2 * D_MODEL)] + [_const_spec(c.shape) for c in consts],
        out_specs=row(D_MODEL),
        out_shape=jax.ShapeDtypeStruct((n, D_MODEL), F32),
        compiler_params=pltpu.CompilerParams(dimension_semantics=("parallel",), vmem_limit_bytes=VMEM_LIMIT),
        name="tail",
    )(x2, na_o, o_f, o_b, gla, gate, *consts)


def _prepare(norm_mix_g, w_in, b_in, na_rpb, gk_fwd_w, gk_fwd_b, gk_bwd_w, gk_bwd_b, gla_norm_g,
             w_br_na, w_br_gla, w_out, norm_mlp_g, w_up, w_down, norm_final_g):
    edges = np.concatenate([[0], np.cumsum(IN_WIDTHS)])
    wcols = lambda a, b: w_in[:, edges[a]:edges[b]]
    bcols = lambda a, b: b_in[None, edges[a]:edges[b]]
    r = GLA_GATE_RANK
    pad = LR_PAD - 2 * r
    wlr = jnp.pad(wcols(7, 9), ((0, 0), (0, pad)))
    blr = jnp.pad(bcols(7, 9), ((0, 0), (0, pad)))
    gkw = jnp.zeros((LR_PAD, 2 * GLA_KEY_WIDTH), F32)
    gkw = gkw.at[:r, :GLA_KEY_WIDTH].set(gk_fwd_w).at[r:2 * r, GLA_KEY_WIDTH:].set(gk_bwd_w)
    gkb = jnp.concatenate([gk_fwd_b, gk_bwd_b])[None]
    return dict(
        proj=(norm_mix_g[None], wcols(0, 1).astype(BF16), bcols(0, 1), wcols(1, 3).astype(BF16), bcols(1, 3),
              wcols(3, 7).astype(BF16), bcols(3, 7),
              wcols(9, 11).astype(BF16), bcols(9, 11), wlr.astype(BF16), blr, gkw.astype(BF16), gkb),
        na_bias=_na_bias_table(na_rpb),
        tail=(gla_norm_g[None], w_br_na.astype(BF16), w_br_gla.astype(BF16), w_out.astype(BF16),
              norm_mlp_g[None], w_up.astype(BF16), w_down.astype(BF16), norm_final_g[None]),
    )


def _trunk(x, prep, *, tm):
    batch, seq, _ = x.shape
    x2 = x.reshape(batch * seq, D_MODEL)
    na_q, na_kv, gla, gate, loga = _proj_call(x2, *prep["proj"], tm=tm, batch=batch, seq=seq)
    na_o = _na_call(na_q, na_kv, prep["na_bias"], batch=batch, seq=seq)
    o_f, o_b = _gla_call(gla, loga, batch=batch, seq=seq)
    y = _tail_call(x2, na_o.reshape(batch * seq, NA_WIDTH), o_f.reshape(batch * seq, GLA_VAL_WIDTH),
                   o_b.reshape(batch * seq, GLA_VAL_WIDTH), gla, gate, *prep["tail"], tm=tm)
    return y.reshape(batch, seq, D_MODEL)


def kernel(x_prompt, x_sample, norm_mix_g, w_in, b_in, na_rpb, gk_fwd_w, gk_fwd_b, gk_bwd_w, gk_bwd_b, gla_norm_g, w_br_na, w_br_gla, w_out, norm_mlp_g, w_up, w_down, norm_final_g):
    assert norm_mix_g.shape[0] == 1, "single-layer trunk"
    prep = _prepare(norm_mix_g[0], w_in[0], b_in[0], na_rpb[0], gk_fwd_w[0], gk_fwd_b[0], gk_bwd_w[0],
                    gk_bwd_b[0], gla_norm_g[0], w_br_na[0], w_br_gla[0], w_out[0], norm_mlp_g[0], w_up[0],
                    w_down[0], norm_final_g)
    return _trunk(x_prompt, prep, tm=512), _trunk(x_sample, prep, tm=512)
```

```python
import functools

import numpy as np
import jax
import jax.numpy as jnp
from jax import lax
from jax.experimental import pallas as pl
from jax.experimental.pallas import tpu as pltpu

D_MODEL = 1024
GRID_W = 64
NA_HEADS = 8
NA_HEAD_DIM = 64
NA_WIDTH = NA_HEADS * NA_HEAD_DIM
NA_KH = 8
NA_KW = 16
GLA_HEADS = 4
GLA_KEY_WIDTH = 256
GLA_VAL_WIDTH = 512
GLA_DK = 64
GLA_DV = 128
GLA_GATE_RANK = 16
GLA_GATE_NORMALIZER = 16.0
D_FF = 4 * D_MODEL
RMS_EPS = 1e-6
IN_WIDTHS = (NA_WIDTH, NA_WIDTH, NA_WIDTH, GLA_KEY_WIDTH, GLA_KEY_WIDTH, GLA_VAL_WIDTH, GLA_VAL_WIDTH,
             GLA_GATE_RANK, GLA_GATE_RANK, D_MODEL, D_MODEL)

LANES = 128
BF16_ROWS = 16
GLA_CHUNK = 128
GLA_LEVELS = 7
GLA_CHUNKS_PER_STEP = 4
GLA_FINE_LEVELS = 3
LR_PAD = LANES
NA_COL_BLOCKS = GRID_W // NA_KW
NA_QGROUP = 8
NA_ROWS_PER_STEP = 16
MLP_SLICES = 4
VMEM_LIMIT = 56 * 1024 * 1024

BF16 = jnp.bfloat16
F32 = jnp.float32


def _dot(a, b):
    return jnp.dot(a, b, preferred_element_type=F32)


def _dot_nt(a, b):
    return lax.dot_general(a, b, (((1,), (1,)), ((), ())), preferred_element_type=F32)


def _sigmoid(z):
    return 1.0 / (1.0 + jnp.exp(-z))


def _rms(x):
    return x * lax.rsqrt(jnp.mean(x * x, axis=-1, keepdims=True) + RMS_EPS)


def _proj_kernel(x_ref, g_ref, wq_ref, bq_ref, wkv_ref, bkv_ref, wgla_ref, bgla_ref, wgate_ref, bgate_ref,
                 wlr_ref, blr_ref, gkw_ref, gkb_ref,
                 q_ref, kv_ref, gq_ref, gk_ref, gv_ref, gg_ref, gate_ref, laf_ref, lab_ref):
    xb = (_rms(x_ref[...]) * g_ref[...]).astype(BF16)
    q = (_dot(xb, wq_ref[...]) + bq_ref[...]).astype(BF16)
    for p in range(NA_WIDTH // LANES):
        q_ref[0, p] = q[:, p * LANES:(p + 1) * LANES]
    kv = (_dot(xb, wkv_ref[...]) + bkv_ref[...]).astype(BF16)
    kv = kv.reshape(kv.shape[0] // GRID_W, NA_COL_BLOCKS, NA_KW, kv.shape[1])
    for cb in range(NA_COL_BLOCKS):
        block = kv[:, cb].reshape(-1, kv.shape[3])
        for p in range(2 * NA_WIDTH // LANES):
            kv_ref[0, p, cb] = block[:, p * LANES:(p + 1) * LANES]
    gla = (_dot(xb, wgla_ref[...]) + bgla_ref[...]).astype(BF16)
    w = GLA_KEY_WIDTH
    gq_ref[...] = gla[:, :w]
    gk_ref[...] = gla[:, w:2 * w]
    gv_ref[...] = gla[:, 2 * w:2 * w + GLA_VAL_WIDTH]
    gg_ref[...] = gla[:, 2 * w + GLA_VAL_WIDTH:]
    gate_ref[...] = _sigmoid(_dot(xb, wgate_ref[...]) + bgate_ref[...]).astype(BF16)
    lr = (_dot(xb, wlr_ref[...]) + blr_ref[...]).astype(BF16)
    z = _dot(lr, gkw_ref[...]) + gkb_ref[...]
    log_a = (jnp.minimum(z, 0.0) - jnp.log1p(jnp.exp(-jnp.abs(z)))) * (1.0 / GLA_GATE_NORMALIZER)
    laf_ref[...] = log_a[:, :w]
    lab_ref[...] = log_a[:, w:]


def _const_spec(shape):
    nd = len(shape)
    return pl.BlockSpec(shape, lambda *_: (0,) * nd, pipeline_mode=pl.Buffered(1))


def _proj_call(x2, g, wq, bq, wkv, bkv, wgla, bgla, wgate, bgate, wlr, blr, gkw, gkb, *, tm, batch, seq):
    n = x2.shape[0]
    assert seq % tm == 0 and tm % GRID_W == 0
    tiles = seq // tm
    pairs = NA_WIDTH // LANES
    row = lambda w: pl.BlockSpec((tm, w), lambda i: (i, 0))
    consts = (g, wq, bq, wkv, bkv, wgla, bgla, wgate, bgate, wlr, blr, gkw, gkb)
    q_spec = pl.BlockSpec((1, pairs, tm, LANES), lambda i: (i // tiles, 0, i % tiles, 0))
    kv_spec = pl.BlockSpec((1, 2 * pairs, NA_COL_BLOCKS, tm // NA_COL_BLOCKS, LANES),
                           lambda i: (i // tiles, 0, 0, i % tiles, 0))
    widths = (GLA_KEY_WIDTH, GLA_KEY_WIDTH, GLA_VAL_WIDTH, GLA_VAL_WIDTH, 2 * D_MODEL)
    return pl.pallas_call(
        _proj_kernel,
        grid=(n // tm,),
        in_specs=[row(D_MODEL)] + [_const_spec(c.shape) for c in consts],
        out_specs=[q_spec, kv_spec] + [row(w) for w in widths] + [row(GLA_KEY_WIDTH)] * 2,
        out_shape=[jax.ShapeDtypeStruct((batch, pairs, seq, LANES), BF16),
                   jax.ShapeDtypeStruct((batch, 2 * pairs, NA_COL_BLOCKS, seq // NA_COL_BLOCKS, LANES), BF16)]
        + [jax.ShapeDtypeStruct((n, w), BF16) for w in widths]
        + [jax.ShapeDtypeStruct((n, GLA_KEY_WIDTH), F32)] * 2,
        compiler_params=pltpu.CompilerParams(dimension_semantics=("parallel",), vmem_limit_bytes=VMEM_LIMIT),
        name="proj",
    )(x2, *consts)


def _na_needed_blocks():
    need = []
    for g in range(GRID_W // NA_QGROUP):
        c = np.arange(g * NA_QGROUP, (g + 1) * NA_QGROUP)
        start = np.clip(c - NA_KW // 2, 0, GRID_W - NA_KW)
        need.append(sorted(set((start // NA_KW).tolist()) | set(((start + NA_KW - 1) // NA_KW).tolist())))
    return need


NA_NEED = _na_needed_blocks()
NA_HALF_ROWS = []
for _half in range(2):
    _groups = [g for g, blocks in enumerate(NA_NEED) if any(b // 2 == _half for b in blocks)]
    _lo = (min(_groups) * NA_QGROUP) // BF16_ROWS * BF16_ROWS
    _hi = -(-((max(_groups) + 1) * NA_QGROUP) // BF16_ROWS) * BF16_ROWS
    NA_HALF_ROWS.append((_lo, _hi))


def _na_bias_table(rpb):
    c = np.arange(GRID_W)
    col_start = np.clip(c - NA_KW // 2, 0, GRID_W - NA_KW)
    col_mask = (c[None, :] >= col_start[:, None]) & (c[None, :] < col_start[:, None] + NA_KW)
    dc_idx = np.clip(c[None, :] - c[:, None], -(NA_KW - 1), NA_KW - 1) + (NA_KW - 1)
    onehot = (dc_idx[None] == np.arange(2 * NA_KW - 1)[:, None, None]) & col_mask[None]
    cols = jnp.einsum('hrd,dqk->hrqk', rpb.astype(F32), jnp.asarray(onehot, F32),
                      precision=lax.Precision.HIGHEST)
    cols = jnp.where(col_mask[None, None], cols, -jnp.inf)
    bias = jnp.stack([cols[:, NA_KH - 1 - off:2 * NA_KH - 1 - off] for off in range(NA_KH)], axis=1)
    heads = rpb.shape[0]
    bias = bias.reshape(heads, NA_KH, NA_KH, GRID_W, NA_COL_BLOCKS, NA_KW)
    bias = bias.transpose(0, 1, 3, 4, 2, 5)
    return bias.reshape(heads, NA_KH, GRID_W, NA_KH * GRID_W)


def _na_kernel(q_ref, k_ref, v_ref, bias_ref, o_ref, *, rows):
    lane = lax.broadcasted_iota(jnp.int32, (GRID_W, LANES), 1)
    first = lane < NA_HEAD_DIM
    win = NA_KH * NA_KW
    groups = GRID_W // NA_QGROUP

    def window(r):
        start = jnp.clip(r - NA_KH // 2, 0, rows - NA_KH)
        return pl.ds(pl.multiple_of(start * NA_KW, NA_KW), win), r - start

    def tile(ref, keys, half):
        return jnp.concatenate([ref[0, 0, 2 * half, keys, :], ref[0, 0, 2 * half + 1, keys, :]], axis=0)

    def half_rows(x, half):
        lo, hi = NA_HALF_ROWS[half]
        return jnp.concatenate([x[lo:hi], x[GRID_W + lo:GRID_W + hi]], axis=0)

    def scores(r):
        keys, _ = window(r)
        q = q_ref[0, 0, pl.ds(pl.multiple_of(r * GRID_W, GRID_W), GRID_W), :] * (NA_HEAD_DIM ** -0.5)
        zero = jnp.zeros_like(q)
        qs = jnp.concatenate([jnp.where(first, q, zero), jnp.where(first, zero, q)], axis=0)
        return [_dot_nt(half_rows(qs, half), tile(k_ref, keys, half)) for half in range(2)]

    def attend(r, s):
        keys, off = window(r)
        p_rows = [[], []]
        inv = []
        for head in range(2):
            for g in range(groups):
                rows8 = slice(g * NA_QGROUP, (g + 1) * NA_QGROUP)
                pieces = {}
                for cb in NA_NEED[g]:
                    half = cb // 2
                    lo, hi = NA_HALF_ROWS[half]
                    at = head * (hi - lo) + g * NA_QGROUP - lo
                    pieces[cb] = (s[half][at:at + NA_QGROUP, (cb % 2) * LANES:(cb % 2 + 1) * LANES]
                                  + bias_ref[head, off, rows8, cb * LANES:(cb + 1) * LANES])
                m = functools.reduce(jnp.maximum, pieces.values())
                m = jnp.max(m, axis=-1, keepdims=True)
                pieces = {cb: jnp.exp(x - m) for cb, x in pieces.items()}
                total = functools.reduce(jnp.add, pieces.values())
                inv.append(1.0 / jnp.sum(total, axis=-1, keepdims=True))
                for half in range(2):
                    lo, hi = NA_HALF_ROWS[half]
                    if lo <= g * NA_QGROUP < hi:
                        zero = jnp.zeros((NA_QGROUP, LANES), F32)
                        p_rows[half].append(jnp.concatenate(
                            [pieces.get(2 * half, zero), pieces.get(2 * half + 1, zero)], axis=1))
        outs = []
        for half in range(2):
            p = jnp.concatenate(p_rows[half], axis=0).astype(BF16)
            outs.append(_dot(p, tile(v_ref, keys, half)))
        heads_o = []
        for head in range(2):
            rows_o = []
            for g in range(groups):
                acc = None
                for half in range(2):
                    lo, hi = NA_HALF_ROWS[half]
                    if lo <= g * NA_QGROUP < hi:
                        at = head * (hi - lo) + g * NA_QGROUP - lo
                        part = outs[half][at:at + NA_QGROUP]
                        acc = part if acc is None else acc + part
                rows_o.append(acc * inv[head * groups + g])
            heads_o.append(jnp.concatenate(rows_o, axis=0))
        o_ref[0, 0, pl.ds(pl.multiple_of(r * GRID_W, GRID_W), GRID_W), :] = jnp.where(
            first, heads_o[0], heads_o[1]).astype(BF16)

    def body(i, carry):
        rs = [i * NA_ROWS_PER_STEP + u for u in range(NA_ROWS_PER_STEP)]
        ss = [scores(r) for r in rs]
        for r, s in zip(rs, ss):
            attend(r, s)
        return carry

    lax.fori_loop(0, rows // NA_ROWS_PER_STEP, body, 0)


def _na_call(na_q, na_kv, bias, *, batch, seq):
    rows = seq // GRID_W
    assert rows >= NA_KH and seq % GRID_W == 0 and rows % NA_ROWS_PER_STEP == 0
    pairs = NA_WIDTH // LANES
    q_spec = pl.BlockSpec((1, 1, seq, LANES), lambda b, p: (b, p, 0, 0))
    kv_spec = lambda part: pl.BlockSpec((1, 1, NA_COL_BLOCKS, seq // NA_COL_BLOCKS, LANES),
                                        lambda b, p: (b, part * pairs + p, 0, 0, 0))
    return pl.pallas_call(
        functools.partial(_na_kernel, rows=rows),
        grid=(batch, pairs),
        in_specs=[q_spec, kv_spec(0), kv_spec(1),
                  pl.BlockSpec((2, NA_KH, GRID_W, NA_KH * GRID_W), lambda b, p: (p, 0, 0, 0))],
        out_specs=q_spec,
        out_shape=jax.ShapeDtypeStruct((batch, pairs, seq, LANES), BF16),
        compiler_params=pltpu.CompilerParams(dimension_semantics=("parallel", "arbitrary"),
                                             vmem_limit_bytes=VMEM_LIMIT),
        name="na",
    )(na_q, na_kv, na_kv, bias)


def _gla_constants(reverse):
    c = GLA_CHUNK
    t = np.arange(c)
    flip = (lambda a: a[::-1, ::-1]) if reverse else (lambda a: a)
    sums = [flip((t[None, :] <= t[:, None]).astype(np.float32))]
    fine = [np.eye(c, dtype=np.float32)]
    coarse = []
    for lvl in range(GLA_LEVELS):
        m = 1 << lvl
        base = (t // (2 * m)) * (2 * m)
        upper = (t % (2 * m)) >= m
        s = t[None, :]
        mask = flip((upper[:, None] & (~upper)[None, :] & (base[:, None] == base[None, :])).astype(np.float32))
        if lvl < GLA_FINE_LEVELS:
            q_rows = upper[:, None] & (s >= (base + m)[:, None]) & (s <= t[:, None])
            k_rows = (~upper)[:, None] & (s > t[:, None]) & (s <= (base + m - 1)[:, None])
            sums.append(flip((q_rows | k_rows).astype(np.float32)))
            fine.append(mask)
        else:
            coarse.append(mask[mask.any(axis=1)])
    two_heads = lambda mk: np.concatenate([mk, mk], axis=0)
    return (jnp.asarray(np.concatenate(sums, axis=0), BF16),
            jnp.asarray(np.stack([two_heads(mk) for mk in fine]), F32),
            jnp.asarray(np.stack([two_heads(mk) for mk in coarse]), F32))


def _stack_heads(x):
    first = lax.broadcasted_iota(jnp.int32, x.shape, 1) < GLA_DK
    zero = jnp.zeros_like(x)
    return jnp.concatenate([jnp.where(first, x, zero), jnp.where(first, zero, x)], axis=0)


class _GlaDirection:
    def __init__(self, q_ref, k_ref, v_ref, la_ref, sums_ref, fine_ref, coarse_ref, state_ref, o_ref, reverse,
                 chunk):
        self.refs = (fine_ref, coarse_ref, state_ref, o_ref)
        self.reverse = reverse
        c, w = GLA_CHUNK, GLA_KEY_WIDTH
        self.rows = rows = slice(chunk * c, (chunk + 1) * c)
        la = la_ref[0, rows, :]
        la_hi = la.astype(BF16)
        la_lo = (la - la_hi.astype(F32)).astype(BF16)
        sums = _dot(sums_ref[...], jnp.concatenate([la_hi, la_lo], axis=1))
        sums = sums[:, :w] + sums[:, w:]
        cum = sums[:c]
        last_row = 0 if reverse else c - 1
        cum_last = cum[last_row:last_row + 1]
        q = q_ref[0, rows, :].astype(F32) * (GLA_DK ** -0.5)
        k = k_ref[0, rows, :].astype(F32)
        self.v = v_ref[0, rows, :]
        self.q_lvl = [q.astype(BF16)]
        self.k_lvl = [k.astype(BF16)]
        for lvl in range(GLA_FINE_LEVELS):
            e = jnp.exp(sums[(lvl + 1) * c:(lvl + 2) * c])
            self.q_lvl.append((q * e).astype(BF16))
            self.k_lvl.append((k * e).astype(BF16))
        for lvl in range(GLA_FINE_LEVELS, GLA_LEVELS):
            m = 1 << lvl
            split = lambda x: x.reshape(c // (2 * m), 2 * m, w)
            cum3, q3, k3 = split(cum), split(q), split(k)
            lo, hi = slice(0, m), slice(m, 2 * m)
            qh, kh = (lo, hi) if reverse else (hi, lo)
            ref = cum3[:, m:m + 1] if reverse else cum3[:, m - 1:m]
            self.q_lvl.append((q3[:, qh] * jnp.exp(cum3[:, qh] - ref)).reshape(c // 2, w).astype(BF16))
            k_scaled = k3[:, kh] * jnp.exp(ref - cum3[:, kh])
            k_parts = [k3[:, lo], k_scaled] if reverse else [k_scaled, k3[:, hi]]
            self.k_lvl.append(jnp.concatenate(k_parts, axis=1).reshape(c, w).astype(BF16))
        self.q_dec = (q * jnp.exp(cum)).astype(BF16)
        self.k_dec = (k * jnp.exp(cum_last - cum)).astype(BF16)
        self.decay = jnp.exp(cum_last)

    def scores(self, pair):
        fine_ref, coarse_ref, _, _ = self.refs
        c = GLA_CHUNK
        ks = slice(pair * LANES, (pair + 1) * LANES)
        level = lambda lvl: _dot_nt(_stack_heads(self.q_lvl[lvl][:, ks]), self.k_lvl[lvl][:, ks])
        scores = fine_ref[0] * level(0)
        for lvl in range(GLA_FINE_LEVELS):
            scores += fine_ref[lvl + 1] * level(lvl + 1)
        for lvl in range(GLA_FINE_LEVELS, GLA_LEVELS):
            m = 1 << lvl
            x = (coarse_ref[lvl - GLA_FINE_LEVELS] * level(lvl + 1)).reshape(2 * c // (2 * m), m, c)
            zero = jnp.zeros_like(x)
            scores += jnp.concatenate([x, zero] if self.reverse else [zero, x], axis=1).reshape(2 * c, c)
        return scores.astype(BF16)

    def update_state(self, pair):
        _, _, state_ref, _ = self.refs
        ks = slice(pair * LANES, (pair + 1) * LANES)
        vs = slice(pair * 2 * GLA_DV, (pair + 1) * 2 * GLA_DV)
        state = state_ref[pair]
        v_t = self.v[:, vs].T
        upd = _dot(v_t, self.k_dec[:, ks])
        row = lax.broadcasted_iota(jnp.int32, upd.shape, 0)
        col = lax.broadcasted_iota(jnp.int32, upd.shape, 1)
        own = (row < GLA_DV) == (col < GLA_DK)
        state_ref[pair] = state * self.decay[:, ks] + jnp.where(own, upd, 0.0)
        return jnp.concatenate([v_t, state.astype(BF16)], axis=1)

    def output(self, pair, scores, rhs):
        _, _, _, o_ref = self.refs
        c = GLA_CHUNK
        ks = slice(pair * LANES, (pair + 1) * LANES)
        lhs = jnp.concatenate([scores, _stack_heads(self.q_dec[:, ks])], axis=1)
        o = _dot_nt(lhs, rhs)
        o_ref[0, self.rows, pair * 2 * GLA_DV:pair * 2 * GLA_DV + GLA_DV] = o[:c, :GLA_DV].astype(o_ref.dtype)
        o_ref[0, self.rows, pair * 2 * GLA_DV + GLA_DV:(pair + 1) * 2 * GLA_DV] = o[c:, GLA_DV:].astype(o_ref.dtype)


def _gla_kernel(qf_ref, kf_ref, vf_ref, laf_ref, qb_ref, kb_ref, vb_ref, lab_ref,
                sums_f_ref, fine_f_ref, coarse_f_ref, sums_b_ref, fine_b_ref, coarse_b_ref,
                of_ref, ob_ref, state_f_ref, state_b_ref):
    @pl.when(pl.program_id(1) == 0)
    def _():
        state_f_ref[...] = jnp.zeros_like(state_f_ref)
        state_b_ref[...] = jnp.zeros_like(state_b_ref)

    pairs = range(GLA_HEADS // 2)
    groups = []
    for j in range(GLA_CHUNKS_PER_STEP):
        groups.append((_GlaDirection(qf_ref, kf_ref, vf_ref, laf_ref, sums_f_ref, fine_f_ref, coarse_f_ref,
                                     state_f_ref, of_ref, False, j),
                       _GlaDirection(qb_ref, kb_ref, vb_ref, lab_ref, sums_b_ref, fine_b_ref, coarse_b_ref,
                                     state_b_ref, ob_ref, True, GLA_CHUNKS_PER_STEP - 1 - j)))
    scores = [[[d.scores(pair) for pair in pairs] for d in group] for group in groups]
    for group, group_scores in zip(groups, scores):
        rhs = [[d.update_state(pair) for pair in pairs] for d in group]
        for d, d_scores, d_rhs in zip(group, group_scores, rhs):
            for pair in pairs:
                d.output(pair, d_scores[pair], d_rhs[pair])


def _gla_call(q, k, v, la_f, la_b, *, batch, seq):
    c = GLA_CHUNK * GLA_CHUNKS_PER_STEP
    assert seq % c == 0
    n = seq // c
    fwd = lambda b, i: i
    bwd = lambda b, i: n - 1 - i
    consts = _gla_constants(False) + _gla_constants(True)
    spec = lambda chunk, w: pl.BlockSpec((1, c, w), lambda b, i: (b, chunk(b, i), 0))
    specs = lambda chunk: [spec(chunk, GLA_KEY_WIDTH)] * 2 + [spec(chunk, GLA_VAL_WIDTH), spec(chunk, GLA_KEY_WIDTH)]
    state = pltpu.VMEM((GLA_HEADS // 2, 2 * GLA_DV, LANES), F32)
    b3 = lambda a: a.reshape(batch, seq, a.shape[-1])
    q, k, v, la_f, la_b = b3(q), b3(k), b3(v), b3(la_f), b3(la_b)
    return pl.pallas_call(
        _gla_kernel,
        grid=(batch, n),
        in_specs=specs(fwd) + specs(bwd) + [_const_spec(a.shape) for a in consts],
        out_specs=[spec(fwd, GLA_VAL_WIDTH), spec(bwd, GLA_VAL_WIDTH)],
        out_shape=[jax.ShapeDtypeStruct((batch, seq, GLA_VAL_WIDTH), BF16)] * 2,
        scratch_shapes=[state, state],
        compiler_params=pltpu.CompilerParams(dimension_semantics=("parallel", "arbitrary"),
                                             vmem_limit_bytes=VMEM_LIMIT),
        name="gla",
    )(q, k, v, la_f, q, k, v, la_b, *consts)


def _tail_kernel(x_ref, nao_ref, of_ref, ob_ref, g_ref, gate_ref, gng_ref, wna_ref, wgla_ref, wout_ref,
                 gm_ref, wup_ref, wdown_ref, gf_ref, y_ref):
    na_o = jnp.concatenate([nao_ref[0, p] for p in range(NA_WIDTH // LANES)], axis=1)
    na_out = _dot(na_o, wna_ref[...])
    o = of_ref[...].astype(F32) + ob_ref[...].astype(F32)
    heads = []
    for h in range(GLA_HEADS):
        heads.append(_rms(o[:, h * GLA_DV:(h + 1) * GLA_DV]) * gng_ref[...])
    g = g_ref[...].astype(F32)
    gla_o = jnp.concatenate(heads, axis=1) * (g * _sigmoid(g))
    gla_out = _dot(gla_o.astype(BF16), wgla_ref[...])
    gate = gate_ref[...].astype(F32)
    merged = gate[:, :D_MODEL] * na_out + gate[:, D_MODEL:] * gla_out
    h = x_ref[...] + _dot(merged.astype(BF16), wout_ref[...])
    u = (_rms(h) * gm_ref[...]).astype(BF16)
    width = D_FF // MLP_SLICES
    for j in range(MLP_SLICES):
        cols = slice(j * width, (j + 1) * width)
        hdn = jnp.square(jnp.maximum(_dot(u, wup_ref[:, cols]), 0.0)).astype(BF16)
        h = h + _dot(hdn, wdown_ref[cols, :])
    y_ref[...] = _rms(h) * gf_ref[...]


def _tail_call(x2, na_o, o_f, o_b, gla_g, gate, gng, wna, wgla, wout, gm, wup, wdown, gf, *, tm):
    n = x2.shape[0]
    tiles = na_o.shape[2] // tm
    row = lambda w: pl.BlockSpec((tm, w), lambda i: (i, 0))
    nao_spec = pl.BlockSpec((1, NA_WIDTH // LANES, tm, LANES), lambda i: (i // tiles, 0, i % tiles, 0))
    consts = (gng, wna, wgla, wout, gm, wup, wdown, gf)
    return pl.pallas_call(
        _tail_kernel,
        grid=(n // tm,),
        in_specs=[row(D_MODEL), nao_spec, row(GLA_VAL_WIDTH), row(GLA_VAL_WIDTH),
                  row(GLA_VAL_WIDTH), row(2 * D_MODEL)] + [_const_spec(c.shape) for c in consts],
        out_specs=row(D_MODEL),
        out_shape=jax.ShapeDtypeStruct((n, D_MODEL), F32),
        compiler_params=pltpu.CompilerParams(dimension_semantics=("parallel",), vmem_limit_bytes=VMEM_LIMIT),
        name="tail",
    )(x2, na_o, o_f, o_b, gla_g, gate, *consts)


def _prepare(norm_mix_g, w_in, b_in, na_rpb, gk_fwd_w, gk_fwd_b, gk_bwd_w, gk_bwd_b, gla_norm_g,
             w_br_na, w_br_gla, w_out, norm_mlp_g, w_up, w_down, norm_final_g):
    edges = np.concatenate([[0], np.cumsum(IN_WIDTHS)])
    wcols = lambda a, b: w_in[:, edges[a]:edges[b]]
    bcols = lambda a, b: b_in[None, edges[a]:edges[b]]
    r = GLA_GATE_RANK
    pad = LR_PAD - 2 * r
    wlr = jnp.pad(wcols(7, 9), ((0, 0), (0, pad)))
    blr = jnp.pad(bcols(7, 9), ((0, 0), (0, pad)))
    gkw = jnp.zeros((LR_PAD, 2 * GLA_KEY_WIDTH), F32)
    gkw = gkw.at[:r, :GLA_KEY_WIDTH].set(gk_fwd_w).at[r:2 * r, GLA_KEY_WIDTH:].set(gk_bwd_w)
    gkb = jnp.concatenate([gk_fwd_b, gk_bwd_b])[None]
    return dict(
        proj=(norm_mix_g[None], wcols(0, 1).astype(BF16), bcols(0, 1), wcols(1, 3).astype(BF16), bcols(1, 3),
              wcols(3, 7).astype(BF16), bcols(3, 7),
              wcols(9, 11).astype(BF16), bcols(9, 11), wlr.astype(BF16), blr, gkw.astype(BF16), gkb),
        na_bias=_na_bias_table(na_rpb),
        tail=(gla_norm_g[None], w_br_na.astype(BF16), w_br_gla.astype(BF16), w_out.astype(BF16),
              norm_mlp_g[None], w_up.astype(BF16), w_down.astype(BF16), norm_final_g[None]),
    )


def _trunk(x, prep, *, tm):
    batch, seq, _ = x.shape
    x2 = x.reshape(batch * seq, D_MODEL)
    na_q, na_kv, gla_q, gla_k, gla_v, gla_g, gate, la_f, la_b = _proj_call(
        x2, *prep["proj"], tm=tm, batch=batch, seq=seq)
    na_o = _na_call(na_q, na_kv, prep["na_bias"], batch=batch, seq=seq)
    o_f, o_b = _gla_call(gla_q, gla_k, gla_v, la_f, la_b, batch=batch, seq=seq)
    y = _tail_call(x2, na_o, o_f.reshape(batch * seq, GLA_VAL_WIDTH), o_b.reshape(batch * seq, GLA_VAL_WIDTH),
                   gla_g, gate, *prep["tail"], tm=tm)
    return y.reshape(batch, seq, D_MODEL)


def kernel(x_prompt, x_sample, norm_mix_g, w_in, b_in, na_rpb, gk_fwd_w, gk_fwd_b, gk_bwd_w, gk_bwd_b, gla_norm_g, w_br_na, w_br_gla, w_out, norm_mlp_g, w_up, w_down, norm_final_g):
    assert norm_mix_g.shape[0] == 1, "single-layer trunk"
    prep = _prepare(norm_mix_g[0], w_in[0], b_in[0], na_rpb[0], gk_fwd_w[0], gk_fwd_b[0], gk_bwd_w[0],
                    gk_bwd_b[0], gla_norm_g[0], w_br_na[0], w_br_gla[0], w_out[0], norm_mlp_g[0], w_up[0],
                    w_down[0], norm_final_g)
    return _trunk(x_prompt, prep, tm=512), _trunk(x_sample, prep, tm=512)
```

```python
import functools

import numpy as np
import jax
import jax.numpy as jnp
from jax import lax
from jax.experimental import pallas as pl
from jax.experimental.pallas import tpu as pltpu

D_MODEL = 1024
GRID_W = 64
NA_HEADS = 8
NA_HEAD_DIM = 64
NA_WIDTH = NA_HEADS * NA_HEAD_DIM
NA_KH = 8
NA_KW = 16
GLA_HEADS = 4
GLA_KEY_WIDTH = 256
GLA_VAL_WIDTH = 512
GLA_DK = 64
GLA_DV = 128
GLA_GATE_RANK = 16
GLA_GATE_NORMALIZER = 16.0
D_FF = 4 * D_MODEL
RMS_EPS = 1e-6
IN_WIDTHS = (NA_WIDTH, NA_WIDTH, NA_WIDTH, GLA_KEY_WIDTH, GLA_KEY_WIDTH, GLA_VAL_WIDTH, GLA_VAL_WIDTH,
             GLA_GATE_RANK, GLA_GATE_RANK, D_MODEL, D_MODEL)

LANES = 128
BF16_ROWS = 16
GLA_CHUNK = 128
GLA_LEVELS = 7
GLA_CHUNKS_PER_STEP = 4
GLA_FINE_LEVELS = 3
LR_PAD = LANES
NA_COL_BLOCKS = GRID_W // NA_KW
NA_QGROUP = 8
NA_ROWS_PER_STEP = 16
MLP_SLICES = 4
VMEM_LIMIT = 56 * 1024 * 1024

BF16 = jnp.bfloat16
F32 = jnp.float32


def _dot(a, b):
    return jnp.dot(a, b, preferred_element_type=F32)


def _dot_nt(a, b):
    return lax.dot_general(a, b, (((1,), (1,)), ((), ())), preferred_element_type=F32)


def _sigmoid(z):
    return 1.0 / (1.0 + jnp.exp(-z))


def _rms(x):
    return x * lax.rsqrt(jnp.mean(x * x, axis=-1, keepdims=True) + RMS_EPS)


def _proj_kernel(x_ref, g_ref, wq_ref, bq_ref, wkv_ref, bkv_ref, wgla_ref, bgla_ref, wgate_ref, bgate_ref,
                 wlr_ref, blr_ref, gkw_ref, gkb_ref,
                 q_ref, kv_ref, gq_ref, gk_ref, gv_ref, gg_ref, gate_ref, laf_ref, lab_ref):
    xb = (_rms(x_ref[...]) * g_ref[...]).astype(BF16)
    q = (_dot(xb, wq_ref[...]) + bq_ref[...]).astype(BF16)
    for p in range(NA_WIDTH // LANES):
        q_ref[0, p] = q[:, p * LANES:(p + 1) * LANES]
    kv = (_dot(xb, wkv_ref[...]) + bkv_ref[...]).astype(BF16)
    kv = kv.reshape(kv.shape[0] // GRID_W, NA_COL_BLOCKS, NA_KW, kv.shape[1])
    for cb in range(NA_COL_BLOCKS):
        block = kv[:, cb].reshape(-1, kv.shape[3])
        for p in range(2 * NA_WIDTH // LANES):
            kv_ref[0, p, cb] = block[:, p * LANES:(p + 1) * LANES]
    gla = (_dot(xb, wgla_ref[...]) + bgla_ref[...]).astype(BF16)
    w = GLA_KEY_WIDTH
    gq_ref[...] = gla[:, :w]
    gk_ref[...] = gla[:, w:2 * w]
    gv_ref[...] = gla[:, 2 * w:2 * w + GLA_VAL_WIDTH]
    gg_ref[...] = gla[:, 2 * w + GLA_VAL_WIDTH:]
    gate_ref[...] = _sigmoid(_dot(xb, wgate_ref[...]) + bgate_ref[...]).astype(BF16)
    lr = (_dot(xb, wlr_ref[...]) + blr_ref[...]).astype(BF16)
    z = _dot(lr, gkw_ref[...]) + gkb_ref[...]
    log_a = (jnp.minimum(z, 0.0) - jnp.log1p(jnp.exp(-jnp.abs(z)))) * (1.0 / GLA_GATE_NORMALIZER)
    laf_ref[...] = log_a[:, :w]
    lab_ref[...] = log_a[:, w:]


def _const_spec(shape):
    nd = len(shape)
    return pl.BlockSpec(shape, lambda *_: (0,) * nd, pipeline_mode=pl.Buffered(1))


def _proj_call(x2, g, wq, bq, wkv, bkv, wgla, bgla, wgate, bgate, wlr, blr, gkw, gkb, *, tm, batch, seq):
    n = x2.shape[0]
    assert seq % tm == 0 and tm % GRID_W == 0
    tiles = seq // tm
    pairs = NA_WIDTH // LANES
    row = lambda w: pl.BlockSpec((tm, w), lambda i: (i, 0))
    consts = (g, wq, bq, wkv, bkv, wgla, bgla, wgate, bgate, wlr, blr, gkw, gkb)
    q_spec = pl.BlockSpec((1, pairs, tm, LANES), lambda i: (i // tiles, 0, i % tiles, 0))
    kv_spec = pl.BlockSpec((1, 2 * pairs, NA_COL_BLOCKS, tm // NA_COL_BLOCKS, LANES),
                           lambda i: (i // tiles, 0, 0, i % tiles, 0))
    widths = (GLA_KEY_WIDTH, GLA_KEY_WIDTH, GLA_VAL_WIDTH, GLA_VAL_WIDTH, 2 * D_MODEL)
    return pl.pallas_call(
        _proj_kernel,
        grid=(n // tm,),
        in_specs=[row(D_MODEL)] + [_const_spec(c.shape) for c in consts],
        out_specs=[q_spec, kv_spec] + [row(w) for w in widths] + [row(GLA_KEY_WIDTH)] * 2,
        out_shape=[jax.ShapeDtypeStruct((batch, pairs, seq, LANES), BF16),
                   jax.ShapeDtypeStruct((batch, 2 * pairs, NA_COL_BLOCKS, seq // NA_COL_BLOCKS, LANES), BF16)]
        + [jax.ShapeDtypeStruct((n, w), BF16) for w in widths]
        + [jax.ShapeDtypeStruct((n, GLA_KEY_WIDTH), F32)] * 2,
        compiler_params=pltpu.CompilerParams(dimension_semantics=("parallel",), vmem_limit_bytes=VMEM_LIMIT),
        name="proj",
    )(x2, *consts)


def _na_needed_blocks():
    need = []
    for g in range(GRID_W // NA_QGROUP):
        c = np.arange(g * NA_QGROUP, (g + 1) * NA_QGROUP)
        start = np.clip(c - NA_KW // 2, 0, GRID_W - NA_KW)
        need.append(sorted(set((start // NA_KW).tolist()) | set(((start + NA_KW - 1) // NA_KW).tolist())))
    return need


NA_NEED = _na_needed_blocks()
NA_HALF_ROWS = []
for _half in range(2):
    _groups = [g for g, blocks in enumerate(NA_NEED) if any(b // 2 == _half for b in blocks)]
    _lo = (min(_groups) * NA_QGROUP) // BF16_ROWS * BF16_ROWS
    _hi = -(-((max(_groups) + 1) * NA_QGROUP) // BF16_ROWS) * BF16_ROWS
    NA_HALF_ROWS.append((_lo, _hi))


def _na_bias_table(rpb):
    c = np.arange(GRID_W)
    col_start = np.clip(c - NA_KW // 2, 0, GRID_W - NA_KW)
    col_mask = (c[None, :] >= col_start[:, None]) & (c[None, :] < col_start[:, None] + NA_KW)
    dc_idx = np.clip(c[None, :] - c[:, None], -(NA_KW - 1), NA_KW - 1) + (NA_KW - 1)
    onehot = (dc_idx[None] == np.arange(2 * NA_KW - 1)[:, None, None]) & col_mask[None]
    cols = jnp.einsum('hrd,dqk->hrqk', rpb.astype(F32), jnp.asarray(onehot, F32),
                      precision=lax.Precision.HIGHEST)
    cols = jnp.where(col_mask[None, None], cols, -jnp.inf)
    bias = jnp.stack([cols[:, NA_KH - 1 - off:2 * NA_KH - 1 - off] for off in range(NA_KH)], axis=1)
    heads = rpb.shape[0]
    bias = bias.reshape(heads, NA_KH, NA_KH, GRID_W, NA_COL_BLOCKS, NA_KW)
    bias = bias.transpose(0, 1, 3, 4, 2, 5)
    return bias.reshape(heads, NA_KH, GRID_W, NA_KH * GRID_W)


def _na_kernel(q_ref, k_ref, v_ref, bias_ref, o_ref, *, rows):
    lane = lax.broadcasted_iota(jnp.int32, (GRID_W, LANES), 1)
    first = lane < NA_HEAD_DIM
    win = NA_KH * NA_KW
    groups = GRID_W // NA_QGROUP

    def window(r):
        start = jnp.clip(r - NA_KH // 2, 0, rows - NA_KH)
        return pl.ds(pl.multiple_of(start * NA_KW, NA_KW), win), r - start

    def tile(ref, keys, half):
        return jnp.concatenate([ref[0, 0, 2 * half, keys, :], ref[0, 0, 2 * half + 1, keys, :]], axis=0)

    def half_rows(x, half):
        lo, hi = NA_HALF_ROWS[half]
        return jnp.concatenate([x[lo:hi], x[GRID_W + lo:GRID_W + hi]], axis=0)

    def scores(r):
        keys, _ = window(r)
        q = q_ref[0, 0, pl.ds(pl.multiple_of(r * GRID_W, GRID_W), GRID_W), :] * (NA_HEAD_DIM ** -0.5)
        zero = jnp.zeros_like(q)
        qs = jnp.concatenate([jnp.where(first, q, zero), jnp.where(first, zero, q)], axis=0)
        return [_dot_nt(half_rows(qs, half), tile(k_ref, keys, half)) for half in range(2)]

    def attend(r, s):
        keys, off = window(r)
        p_rows = [[], []]
        inv = []
        for head in range(2):
            for g in range(groups):
                rows8 = slice(g * NA_QGROUP, (g + 1) * NA_QGROUP)
                pieces = {}
                for cb in NA_NEED[g]:
                    half = cb // 2
                    lo, hi = NA_HALF_ROWS[half]
                    at = head * (hi - lo) + g * NA_QGROUP - lo
                    pieces[cb] = (s[half][at:at + NA_QGROUP, (cb % 2) * LANES:(cb % 2 + 1) * LANES]
                                  + bias_ref[head, off, rows8, cb * LANES:(cb + 1) * LANES])
                m = functools.reduce(jnp.maximum, pieces.values())
                m = jnp.max(m, axis=-1, keepdims=True)
                pieces = {cb: jnp.exp(x - m) for cb, x in pieces.items()}
                total = functools.reduce(jnp.add, pieces.values())
                inv.append(1.0 / jnp.sum(total, axis=-1, keepdims=True))
                for half in range(2):
                    lo, hi = NA_HALF_ROWS[half]
                    if lo <= g * NA_QGROUP < hi:
                        zero = jnp.zeros((NA_QGROUP, LANES), F32)
                        p_rows[half].append(jnp.concatenate(
                            [pieces.get(2 * half, zero), pieces.get(2 * half + 1, zero)], axis=1))
        outs = []
        for half in range(2):
            p = jnp.concatenate(p_rows[half], axis=0).astype(BF16)
            outs.append(_dot(p, tile(v_ref, keys, half)))
        heads_o = []
        for head in range(2):
            rows_o = []
            for g in range(groups):
                acc = None
                for half in range(2):
                    lo, hi = NA_HALF_ROWS[half]
                    if lo <= g * NA_QGROUP < hi:
                        at = head * (hi - lo) + g * NA_QGROUP - lo
                        part = outs[half][at:at + NA_QGROUP]
                        acc = part if acc is None else acc + part
                rows_o.append(acc * inv[head * groups + g])
            heads_o.append(jnp.concatenate(rows_o, axis=0))
        o_ref[0, 0, pl.ds(pl.multiple_of(r * GRID_W, GRID_W), GRID_W), :] = jnp.where(
            first, heads_o[0], heads_o[1]).astype(BF16)

    def body(i, carry):
        rs = [i * NA_ROWS_PER_STEP + u for u in range(NA_ROWS_PER_STEP)]
        ss = [scores(r) for r in rs]
        for r, s in zip(rs, ss):
            attend(r, s)
        return carry

    lax.fori_loop(0, rows // NA_ROWS_PER_STEP, body, 0)


def _na_call(na_q, na_kv, bias, *, batch, seq):
    rows = seq // GRID_W
    assert rows >= NA_KH and seq % GRID_W == 0 and rows % NA_ROWS_PER_STEP == 0
    pairs = NA_WIDTH // LANES
    q_spec = pl.BlockSpec((1, 1, seq, LANES), lambda b, p: (b, p, 0, 0))
    kv_spec = lambda part: pl.BlockSpec((1, 1, NA_COL_BLOCKS, seq // NA_COL_BLOCKS, LANES),
                                        lambda b, p: (b, part * pairs + p, 0, 0, 0))
    return pl.pallas_call(
        functools.partial(_na_kernel, rows=rows),
        grid=(batch, pairs),
        in_specs=[q_spec, kv_spec(0), kv_spec(1),
                  pl.BlockSpec((2, NA_KH, GRID_W, NA_KH * GRID_W), lambda b, p: (p, 0, 0, 0))],
        out_specs=q_spec,
        out_shape=jax.ShapeDtypeStruct((batch, pairs, seq, LANES), BF16),
        compiler_params=pltpu.CompilerParams(dimension_semantics=("parallel", "arbitrary"),
                                             vmem_limit_bytes=VMEM_LIMIT),
        name="na",
    )(na_q, na_kv, na_kv, bias)


def _gla_constants(reverse):
    c = GLA_CHUNK
    t = np.arange(c)
    flip = (lambda a: a[::-1, ::-1]) if reverse else (lambda a: a)
    sums = [flip((t[None, :] <= t[:, None]).astype(np.float32))]
    fine = [np.eye(c, dtype=np.float32)]
    coarse = []
    for lvl in range(GLA_LEVELS):
        m = 1 << lvl
        base = (t // (2 * m)) * (2 * m)
        upper = (t % (2 * m)) >= m
        s = t[None, :]
        mask = flip((upper[:, None] & (~upper)[None, :] & (base[:, None] == base[None, :])).astype(np.float32))
        if lvl < GLA_FINE_LEVELS:
            q_rows = upper[:, None] & (s >= (base + m)[:, None]) & (s <= t[:, None])
            k_rows = (~upper)[:, None] & (s > t[:, None]) & (s <= (base + m - 1)[:, None])
            sums.append(flip((q_rows | k_rows).astype(np.float32)))
            fine.append(mask)
        else:
            coarse.append(mask[mask.any(axis=1)])
    two_heads = lambda mk: np.concatenate([mk, mk], axis=0)
    return (jnp.asarray(np.concatenate(sums, axis=0), BF16),
            jnp.asarray(np.stack([two_heads(mk) for mk in fine]), F32),
            jnp.asarray(np.stack([two_heads(mk) for mk in coarse]), F32))


def _stack_heads(x):
    first = lax.broadcasted_iota(jnp.int32, x.shape, 1) < GLA_DK
    zero = jnp.zeros_like(x)
    return jnp.concatenate([jnp.where(first, x, zero), jnp.where(first, zero, x)], axis=0)


class _GlaUnit:
    def __init__(self, q_ref, k_ref, v_ref, la_ref, sums_ref, fine_ref, coarse_ref, state_ref, o_ref, reverse,
                 chunk, pair):
        self.refs = (fine_ref, coarse_ref, state_ref, o_ref)
        self.reverse, self.pair = reverse, pair
        c, w = GLA_CHUNK, LANES
        self.rows = rows = slice(chunk * c, (chunk + 1) * c)
        ks = slice(pair * LANES, (pair + 1) * LANES)
        la = la_ref[0, rows, ks]
        la_hi = la.astype(BF16)
        la_lo = (la - la_hi.astype(F32)).astype(BF16)
        sums = _dot(sums_ref[...], jnp.concatenate([la_hi, la_lo], axis=1))
        sums = sums[:, :w] + sums[:, w:]
        cum = sums[:c]
        last_row = 0 if reverse else c - 1
        cum_last = cum[last_row:last_row + 1]
        q = q_ref[0, rows, ks].astype(F32) * (GLA_DK ** -0.5)
        k = k_ref[0, rows, ks].astype(F32)
        self.v = v_ref[0, rows, pair * 2 * GLA_DV:(pair + 1) * 2 * GLA_DV]
        self.q_lvl = [q.astype(BF16)]
        self.k_lvl = [k.astype(BF16)]
        for lvl in range(GLA_FINE_LEVELS):
            e = jnp.exp(sums[(lvl + 1) * c:(lvl + 2) * c])
            self.q_lvl.append((q * e).astype(BF16))
            self.k_lvl.append((k * e).astype(BF16))
        for lvl in range(GLA_FINE_LEVELS, GLA_LEVELS):
            m = 1 << lvl
            split = lambda x: x.reshape(c // (2 * m), 2 * m, w)
            cum3, q3, k3 = split(cum), split(q), split(k)
            lo, hi = slice(0, m), slice(m, 2 * m)
            qh, kh = (lo, hi) if reverse else (hi, lo)
            ref = cum3[:, m:m + 1] if reverse else cum3[:, m - 1:m]
            self.q_lvl.append((q3[:, qh] * jnp.exp(cum3[:, qh] - ref)).reshape(c // 2, w).astype(BF16))
            k_scaled = k3[:, kh] * jnp.exp(ref - cum3[:, kh])
            k_parts = [k3[:, lo], k_scaled] if reverse else [k_scaled, k3[:, hi]]
            self.k_lvl.append(jnp.concatenate(k_parts, axis=1).reshape(c, w).astype(BF16))
        self.q_dec = (q * jnp.exp(cum)).astype(BF16)
        self.k_dec = (k * jnp.exp(cum_last - cum)).astype(BF16)
        self.decay = jnp.exp(cum_last)

    def scores(self):
        fine_ref, coarse_ref, _, _ = self.refs
        c = GLA_CHUNK
        level = lambda lvl: _dot_nt(_stack_heads(self.q_lvl[lvl]), self.k_lvl[lvl])
        scores = fine_ref[0] * level(0)
        for lvl in range(GLA_FINE_LEVELS):
            scores += fine_ref[lvl + 1] * level(lvl + 1)
        for lvl in range(GLA_FINE_LEVELS, GLA_LEVELS):
            m = 1 << lvl
            x = (coarse_ref[lvl - GLA_FINE_LEVELS] * level(lvl + 1)).reshape(2 * c // (2 * m), m, c)
            zero = jnp.zeros_like(x)
            scores += jnp.concatenate([x, zero] if self.reverse else [zero, x], axis=1).reshape(2 * c, c)
        self.s = scores.astype(BF16)

    def output(self):
        _, _, state_ref, o_ref = self.refs
        c, pair = GLA_CHUNK, self.pair
        state = state_ref[pair]
        v_t = self.v.T
        lhs = jnp.concatenate([self.s, _stack_heads(self.q_dec)], axis=1)
        rhs = jnp.concatenate([v_t, state.astype(BF16)], axis=1)
        o = _dot_nt(lhs, rhs)
        o_ref[0, self.rows, pair * 2 * GLA_DV:pair * 2 * GLA_DV + GLA_DV] = o[:c, :GLA_DV].astype(o_ref.dtype)
        o_ref[0, self.rows, pair * 2 * GLA_DV + GLA_DV:(pair + 1) * 2 * GLA_DV] = o[c:, GLA_DV:].astype(o_ref.dtype)
        upd = _dot(v_t, self.k_dec)
        row = lax.broadcasted_iota(jnp.int32, upd.shape, 0)
        col = lax.broadcasted_iota(jnp.int32, upd.shape, 1)
        own = (row < GLA_DV) == (col < GLA_DK)
        state_ref[pair] = state * self.decay + jnp.where(own, upd, 0.0)


def _gla_kernel(qf_ref, kf_ref, vf_ref, laf_ref, qb_ref, kb_ref, vb_ref, lab_ref,
                sums_f_ref, fine_f_ref, coarse_f_ref, sums_b_ref, fine_b_ref, coarse_b_ref,
                of_ref, ob_ref, state_f_ref, state_b_ref):
    @pl.when(pl.program_id(1) == 0)
    def _():
        state_f_ref[...] = jnp.zeros_like(state_f_ref)
        state_b_ref[...] = jnp.zeros_like(state_b_ref)

    fwd = (qf_ref, kf_ref, vf_ref, laf_ref, sums_f_ref, fine_f_ref, coarse_f_ref, state_f_ref, of_ref, False)
    bwd = (qb_ref, kb_ref, vb_ref, lab_ref, sums_b_ref, fine_b_ref, coarse_b_ref, state_b_ref, ob_ref, True)
    order = []
    for j in range(GLA_CHUNKS_PER_STEP):
        for pair in range(GLA_HEADS // 2):
            order.append(fwd + (j, pair))
            order.append(bwd + (GLA_CHUNKS_PER_STEP - 1 - j, pair))
    units = [_GlaUnit(*order[0])]
    for i in range(len(order)):
        if i + 1 < len(order):
            units.append(_GlaUnit(*order[i + 1]))
        units[i].scores()
        if i > 0:
            units[i - 1].output()
    units[-1].output()


def _gla_call(q, k, v, la_f, la_b, *, batch, seq):
    c = GLA_CHUNK * GLA_CHUNKS_PER_STEP
    assert seq % c == 0
    n = seq // c
    fwd = lambda b, i: i
    bwd = lambda b, i: n - 1 - i
    consts = _gla_constants(False) + _gla_constants(True)
    spec = lambda chunk, w: pl.BlockSpec((1, c, w), lambda b, i: (b, chunk(b, i), 0))
    specs = lambda chunk: [spec(chunk, GLA_KEY_WIDTH)] * 2 + [spec(chunk, GLA_VAL_WIDTH), spec(chunk, GLA_KEY_WIDTH)]
    state = pltpu.VMEM((GLA_HEADS // 2, 2 * GLA_DV, LANES), F32)
    b3 = lambda a: a.reshape(batch, seq, a.shape[-1])
    q, k, v, la_f, la_b = b3(q), b3(k), b3(v), b3(la_f), b3(la_b)
    return pl.pallas_call(
        _gla_kernel,
        grid=(batch, n),
        in_specs=specs(fwd) + specs(bwd) + [_const_spec(a.shape) for a in consts],
        out_specs=[spec(fwd, GLA_VAL_WIDTH), spec(bwd, GLA_VAL_WIDTH)],
        out_shape=[jax.ShapeDtypeStruct((batch, seq, GLA_VAL_WIDTH), BF16)] * 2,
        scratch_shapes=[state, state],
        compiler_params=pltpu.CompilerParams(dimension_semantics=("parallel", "arbitrary"),
                                             vmem_limit_bytes=VMEM_LIMIT),
        name="gla",
    )(q, k, v, la_f, q, k, v, la_b, *consts)


def _tail_kernel(x_ref, nao_ref, of_ref, ob_ref, g_ref, gate_ref, gng_ref, wna_ref, wgla_ref, wout_ref,
                 gm_ref, wup_ref, wdown_ref, gf_ref, y_ref):
    na_o = jnp.concatenate([nao_ref[0, p] for p in range(NA_WIDTH // LANES)], axis=1)
    na_out = _dot(na_o, wna_ref[...])
    o = of_ref[...].astype(F32) + ob_ref[...].astype(F32)
    heads = []
    for h in range(GLA_HEADS):
        heads.append(_rms(o[:, h * GLA_DV:(h + 1) * GLA_DV]) * gng_ref[...])
    g = g_ref[...].astype(F32)
    gla_o = jnp.concatenate(heads, axis=1) * (g * _sigmoid(g))
    gla_out = _dot(gla_o.astype(BF16), wgla_ref[...])
    gate = gate_ref[...].astype(F32)
    merged = gate[:, :D_MODEL] * na_out + gate[:, D_MODEL:] * gla_out
    h = x_ref[...] + _dot(merged.astype(BF16), wout_ref[...])
    u = (_rms(h) * gm_ref[...]).astype(BF16)
    width = D_FF // MLP_SLICES
    for j in range(MLP_SLICES):
        cols = slice(j * width, (j + 1) * width)
        hdn = jnp.square(jnp.maximum(_dot(u, wup_ref[:, cols]), 0.0)).astype(BF16)
        h = h + _dot(hdn, wdown_ref[cols, :])
    y_ref[...] = _rms(h) * gf_ref[...]


def _tail_call(x2, na_o, o_f, o_b, gla_g, gate, gng, wna, wgla, wout, gm, wup, wdown, gf, *, tm):
    n = x2.shape[0]
    tiles = na_o.shape[2] // tm
    row = lambda w: pl.BlockSpec((tm, w), lambda i: (i, 0))
    nao_spec = pl.BlockSpec((1, NA_WIDTH // LANES, tm, LANES), lambda i: (i // tiles, 0, i % tiles, 0))
    consts = (gng, wna, wgla, wout, gm, wup, wdown, gf)
    return pl.pallas_call(
        _tail_kernel,
        grid=(n // tm,),
        in_specs=[row(D_MODEL), nao_spec, row(GLA_VAL_WIDTH), row(GLA_VAL_WIDTH),
                  row(GLA_VAL_WIDTH), row(2 * D_MODEL)] + [_const_spec(c.shape) for c in consts],
        out_specs=row(D_MODEL),
        out_shape=jax.ShapeDtypeStruct((n, D_MODEL), F32),
        compiler_params=pltpu.CompilerParams(dimension_semantics=("parallel",), vmem_limit_bytes=VMEM_LIMIT),
        name="tail",
    )(x2, na_o, o_f, o_b, gla_g, gate, *consts)


def _prepare(norm_mix_g, w_in, b_in, na_rpb, gk_fwd_w, gk_fwd_b, gk_bwd_w, gk_bwd_b, gla_norm_g,
             w_br_na, w_br_gla, w_out, norm_mlp_g, w_up, w_down, norm_final_g):
    edges = np.concatenate([[0], np.cumsum(IN_WIDTHS)])
    wcols = lambda a, b: w_in[:, edges[a]:edges[b]]
    bcols = lambda a, b: b_in[None, edges[a]:edges[b]]
    r = GLA_GATE_RANK
    pad = LR_PAD - 2 * r
    wlr = jnp.pad(wcols(7, 9), ((0, 0), (0, pad)))
    blr = jnp.pad(bcols(7, 9), ((0, 0), (0, pad)))
    gkw = jnp.zeros((LR_PAD, 2 * GLA_KEY_WIDTH), F32)
    gkw = gkw.at[:r, :GLA_KEY_WIDTH].set(gk_fwd_w).at[r:2 * r, GLA_KEY_WIDTH:].set(gk_bwd_w)
    gkb = jnp.concatenate([gk_fwd_b, gk_bwd_b])[None]
    return dict(
        proj=(norm_mix_g[None], wcols(0, 1).astype(BF16), bcols(0, 1), wcols(1, 3).astype(BF16), bcols(1, 3),
              wcols(3, 7).astype(BF16), bcols(3, 7),
              wcols(9, 11).astype(BF16), bcols(9, 11), wlr.astype(BF16), blr, gkw.astype(BF16), gkb),
        na_bias=_na_bias_table(na_rpb),
        tail=(gla_norm_g[None], w_br_na.astype(BF16), w_br_gla.astype(BF16), w_out.astype(BF16),
              norm_mlp_g[None], w_up.astype(BF16), w_down.astype(BF16), norm_final_g[None]),
    )


def _trunk(x, prep, *, tm):
    batch, seq, _ = x.shape
    x2 = x.reshape(batch * seq, D_MODEL)
    na_q, na_kv, gla_q, gla_k, gla_v, gla_g, gate, la_f, la_b = _proj_call(
        x2, *prep["proj"], tm=tm, batch=batch, seq=seq)
    na_o = _na_call(na_q, na_kv, prep["na_bias"], batch=batch, seq=seq)
    o_f, o_b = _gla_call(gla_q, gla_k, gla_v, la_f, la_b, batch=batch, seq=seq)
    y = _tail_call(x2, na_o, o_f.reshape(batch * seq, GLA_VAL_WIDTH), o_b.reshape(batch * seq, GLA_VAL_WIDTH),
                   gla_g, gate, *prep["tail"], tm=tm)
    return y.reshape(batch, seq, D_MODEL)


def kernel(x_prompt, x_sample, norm_mix_g, w_in, b_in, na_rpb, gk_fwd_w, gk_fwd_b, gk_bwd_w, gk_bwd_b, gla_norm_g, w_br_na, w_br_gla, w_out, norm_mlp_g, w_up, w_down, norm_final_g):
    assert norm_mix_g.shape[0] == 1, "single-layer trunk"
    prep = _prepare(norm_mix_g[0], w_in[0], b_in[0], na_rpb[0], gk_fwd_w[0], gk_fwd_b[0], gk_bwd_w[0],
                    gk_bwd_b[0], gla_norm_g[0], w_br_na[0], w_br_gla[0], w_out[0], norm_mlp_g[0], w_up[0],
                    w_down[0], norm_final_g)
    return _trunk(x_prompt, prep, tm=512), _trunk(x_sample, prep, tm=512)
```

```python
import functools

import numpy as np
import jax
import jax.numpy as jnp
from jax import lax
from jax.experimental import pallas as pl
from jax.experimental.pallas import tpu as pltpu

D_MODEL = 1024
GRID_W = 64
NA_HEADS = 8
NA_HEAD_DIM = 64
NA_WIDTH = NA_HEADS * NA_HEAD_DIM
NA_KH = 8
NA_KW = 16
GLA_HEADS = 4
GLA_KEY_WIDTH = 256
GLA_VAL_WIDTH = 512
GLA_DK = 64
GLA_DV = 128
GLA_GATE_RANK = 16
GLA_GATE_NORMALIZER = 16.0
D_FF = 4 * D_MODEL
RMS_EPS = 1e-6
IN_WIDTHS = (NA_WIDTH, NA_WIDTH, NA_WIDTH, GLA_KEY_WIDTH, GLA_KEY_WIDTH, GLA_VAL_WIDTH, GLA_VAL_WIDTH,
             GLA_GATE_RANK, GLA_GATE_RANK, D_MODEL, D_MODEL)

LANES = 128
BF16_ROWS = 16
GLA_CHUNK = 128
GLA_LEVELS = 7
GLA_CHUNKS_PER_STEP = 4
GLA_FINE_LEVELS = 3
LR_PAD = LANES
NA_COL_BLOCKS = GRID_W // NA_KW
NA_QGROUP = 8
NA_ROWS_PER_STEP = 16
NA_LOOKAHEAD = 6
MLP_SLICES = 4
VMEM_LIMIT = 56 * 1024 * 1024

BF16 = jnp.bfloat16
F32 = jnp.float32


def _dot(a, b):
    return jnp.dot(a, b, preferred_element_type=F32)


def _dot_nt(a, b):
    return lax.dot_general(a, b, (((1,), (1,)), ((), ())), preferred_element_type=F32)


def _sigmoid(z):
    return 1.0 / (1.0 + jnp.exp(-z))


def _rms(x):
    return x * lax.rsqrt(jnp.mean(x * x, axis=-1, keepdims=True) + RMS_EPS)


def _proj_kernel(x_ref, g_ref, wq_ref, bq_ref, wkv_ref, bkv_ref, wgla_ref, bgla_ref, wgate_ref, bgate_ref,
                 wlr_ref, blr_ref, gkw_ref, gkb_ref,
                 q_ref, kv_ref, gq_ref, gk_ref, gv_ref, gg_ref, gate_ref, laf_ref, lab_ref):
    xb = (_rms(x_ref[...]) * g_ref[...]).astype(BF16)
    q = (_dot(xb, wq_ref[...]) + bq_ref[...]).astype(BF16)
    for p in range(NA_WIDTH // LANES):
        q_ref[0, p] = q[:, p * LANES:(p + 1) * LANES]
    kv = (_dot(xb, wkv_ref[...]) + bkv_ref[...]).astype(BF16)
    kv = kv.reshape(kv.shape[0] // GRID_W, NA_COL_BLOCKS, NA_KW, kv.shape[1])
    for cb in range(NA_COL_BLOCKS):
        block = kv[:, cb].reshape(-1, kv.shape[3])
        for p in range(2 * NA_WIDTH // LANES):
            kv_ref[0, p, cb] = block[:, p * LANES:(p + 1) * LANES]
    gla = (_dot(xb, wgla_ref[...]) + bgla_ref[...]).astype(BF16)
    w = GLA_KEY_WIDTH
    gq_ref[...] = gla[:, :w]
    gk_ref[...] = gla[:, w:2 * w]
    gv_ref[...] = gla[:, 2 * w:2 * w + GLA_VAL_WIDTH]
    gg_ref[...] = gla[:, 2 * w + GLA_VAL_WIDTH:]
    gate_ref[...] = _sigmoid(_dot(xb, wgate_ref[...]) + bgate_ref[...]).astype(BF16)
    lr = (_dot(xb, wlr_ref[...]) + blr_ref[...]).astype(BF16)
    z = _dot(lr, gkw_ref[...]) + gkb_ref[...]
    log_a = (jnp.minimum(z, 0.0) - jnp.log1p(jnp.exp(-jnp.abs(z)))) * (1.0 / GLA_GATE_NORMALIZER)
    laf_ref[...] = log_a[:, :w]
    lab_ref[...] = log_a[:, w:]


def _const_spec(shape):
    nd = len(shape)
    return pl.BlockSpec(shape, lambda *_: (0,) * nd, pipeline_mode=pl.Buffered(1))


def _proj_call(x2, g, wq, bq, wkv, bkv, wgla, bgla, wgate, bgate, wlr, blr, gkw, gkb, *, tm, batch, seq):
    n = x2.shape[0]
    assert seq % tm == 0 and tm % GRID_W == 0
    tiles = seq // tm
    pairs = NA_WIDTH // LANES
    row = lambda w: pl.BlockSpec((tm, w), lambda i: (i, 0))
    consts = (g, wq, bq, wkv, bkv, wgla, bgla, wgate, bgate, wlr, blr, gkw, gkb)
    q_spec = pl.BlockSpec((1, pairs, tm, LANES), lambda i: (i // tiles, 0, i % tiles, 0))
    kv_spec = pl.BlockSpec((1, 2 * pairs, NA_COL_BLOCKS, tm // NA_COL_BLOCKS, LANES),
                           lambda i: (i // tiles, 0, 0, i % tiles, 0))
    widths = (GLA_KEY_WIDTH, GLA_KEY_WIDTH, GLA_VAL_WIDTH, GLA_VAL_WIDTH, 2 * D_MODEL)
    return pl.pallas_call(
        _proj_kernel,
        grid=(n // tm,),
        in_specs=[row(D_MODEL)] + [_const_spec(c.shape) for c in consts],
        out_specs=[q_spec, kv_spec] + [row(w) for w in widths] + [row(GLA_KEY_WIDTH)] * 2,
        out_shape=[jax.ShapeDtypeStruct((batch, pairs, seq, LANES), BF16),
                   jax.ShapeDtypeStruct((batch, 2 * pairs, NA_COL_BLOCKS, seq // NA_COL_BLOCKS, LANES), BF16)]
        + [jax.ShapeDtypeStruct((n, w), BF16) for w in widths]
        + [jax.ShapeDtypeStruct((n, GLA_KEY_WIDTH), F32)] * 2,
        compiler_params=pltpu.CompilerParams(dimension_semantics=("parallel",), vmem_limit_bytes=VMEM_LIMIT),
        name="proj",
    )(x2, *consts)


def _na_needed_blocks():
    need = []
    for g in range(GRID_W // NA_QGROUP):
        c = np.arange(g * NA_QGROUP, (g + 1) * NA_QGROUP)
        start = np.clip(c - NA_KW // 2, 0, GRID_W - NA_KW)
        need.append(sorted(set((start // NA_KW).tolist()) | set(((start + NA_KW - 1) // NA_KW).tolist())))
    return need


NA_NEED = _na_needed_blocks()
NA_HALF_ROWS = []
for _half in range(2):
    _groups = [g for g, blocks in enumerate(NA_NEED) if any(b // 2 == _half for b in blocks)]
    _lo = (min(_groups) * NA_QGROUP) // BF16_ROWS * BF16_ROWS
    _hi = -(-((max(_groups) + 1) * NA_QGROUP) // BF16_ROWS) * BF16_ROWS
    NA_HALF_ROWS.append((_lo, _hi))


def _na_bias_table(rpb):
    c = np.arange(GRID_W)
    col_start = np.clip(c - NA_KW // 2, 0, GRID_W - NA_KW)
    col_mask = (c[None, :] >= col_start[:, None]) & (c[None, :] < col_start[:, None] + NA_KW)
    dc_idx = np.clip(c[None, :] - c[:, None], -(NA_KW - 1), NA_KW - 1) + (NA_KW - 1)
    onehot = (dc_idx[None] == np.arange(2 * NA_KW - 1)[:, None, None]) & col_mask[None]
    cols = jnp.einsum('hrd,dqk->hrqk', rpb.astype(F32), jnp.asarray(onehot, F32),
                      precision=lax.Precision.HIGHEST)
    cols = jnp.where(col_mask[None, None], cols, -jnp.inf)
    bias = jnp.stack([cols[:, NA_KH - 1 - off:2 * NA_KH - 1 - off] for off in range(NA_KH)], axis=1)
    heads = rpb.shape[0]
    bias = bias.reshape(heads, NA_KH, NA_KH, GRID_W, NA_COL_BLOCKS, NA_KW)
    bias = bias.transpose(0, 1, 3, 4, 2, 5)
    return bias.reshape(heads, NA_KH, GRID_W, NA_KH * GRID_W)


def _na_kernel(q_ref, k_ref, v_ref, bias_ref, o_ref, *, rows):
    lane = lax.broadcasted_iota(jnp.int32, (GRID_W, LANES), 1)
    first = lane < NA_HEAD_DIM
    win = NA_KH * NA_KW
    groups = GRID_W // NA_QGROUP

    def window(r):
        start = jnp.clip(r - NA_KH // 2, 0, rows - NA_KH)
        return pl.ds(pl.multiple_of(start * NA_KW, NA_KW), win), r - start

    def tile(ref, keys, half):
        return jnp.concatenate([ref[0, 0, 2 * half, keys, :], ref[0, 0, 2 * half + 1, keys, :]], axis=0)

    def half_rows(x, half):
        lo, hi = NA_HALF_ROWS[half]
        return jnp.concatenate([x[lo:hi], x[GRID_W + lo:GRID_W + hi]], axis=0)

    def scores(r):
        keys, _ = window(r)
        q = q_ref[0, 0, pl.ds(pl.multiple_of(r * GRID_W, GRID_W), GRID_W), :] * (NA_HEAD_DIM ** -0.5)
        zero = jnp.zeros_like(q)
        qs = jnp.concatenate([jnp.where(first, q, zero), jnp.where(first, zero, q)], axis=0)
        return [_dot_nt(half_rows(qs, half), tile(k_ref, keys, half)) for half in range(2)]

    def attend(r, s):
        keys, off = window(r)
        p_rows = [[], []]
        inv = []
        for head in range(2):
            for g in range(groups):
                rows8 = slice(g * NA_QGROUP, (g + 1) * NA_QGROUP)
                pieces = {}
                for cb in NA_NEED[g]:
                    half = cb // 2
                    lo, hi = NA_HALF_ROWS[half]
                    at = head * (hi - lo) + g * NA_QGROUP - lo
                    pieces[cb] = (s[half][at:at + NA_QGROUP, (cb % 2) * LANES:(cb % 2 + 1) * LANES]
                                  + bias_ref[head, off, rows8, cb * LANES:(cb + 1) * LANES])
                m = functools.reduce(jnp.maximum, pieces.values())
                m = jnp.max(m, axis=-1, keepdims=True)
                pieces = {cb: jnp.exp(x - m) for cb, x in pieces.items()}
                total = functools.reduce(jnp.add, pieces.values())
                inv.append(1.0 / jnp.sum(total, axis=-1, keepdims=True))
                for half in range(2):
                    lo, hi = NA_HALF_ROWS[half]
                    if lo <= g * NA_QGROUP < hi:
                        zero = jnp.zeros((NA_QGROUP, LANES), F32)
                        p_rows[half].append(jnp.concatenate(
                            [pieces.get(2 * half, zero), pieces.get(2 * half + 1, zero)], axis=1))
        outs = []
        for half in range(2):
            p = jnp.concatenate(p_rows[half], axis=0).astype(BF16)
            outs.append(_dot(p, tile(v_ref, keys, half)))
        heads_o = []
        for head in range(2):
            rows_o = []
            for g in range(groups):
                acc = None
                for half in range(2):
                    lo, hi = NA_HALF_ROWS[half]
                    if lo <= g * NA_QGROUP < hi:
                        at = head * (hi - lo) + g * NA_QGROUP - lo
                        part = outs[half][at:at + NA_QGROUP]
                        acc = part if acc is None else acc + part
                rows_o.append(acc * inv[head * groups + g])
            heads_o.append(jnp.concatenate(rows_o, axis=0))
        o_ref[0, 0, pl.ds(pl.multiple_of(r * GRID_W, GRID_W), GRID_W), :] = jnp.where(
            first, heads_o[0], heads_o[1]).astype(BF16)

    def body(i, carry):
        rs = [i * NA_ROWS_PER_STEP + u for u in range(NA_ROWS_PER_STEP)]
        ss = [scores(r) for r in rs[:NA_LOOKAHEAD]]
        for u, r in enumerate(rs):
            if u + NA_LOOKAHEAD < len(rs):
                ss.append(scores(rs[u + NA_LOOKAHEAD]))
            attend(r, ss[u])
        return carry

    lax.fori_loop(0, rows // NA_ROWS_PER_STEP, body, 0)


def _na_call(na_q, na_kv, bias, *, batch, seq):
    rows = seq // GRID_W
    assert rows >= NA_KH and seq % GRID_W == 0 and rows % NA_ROWS_PER_STEP == 0
    pairs = NA_WIDTH // LANES
    q_spec = pl.BlockSpec((1, 1, seq, LANES), lambda b, p: (b, p, 0, 0))
    kv_spec = lambda part: pl.BlockSpec((1, 1, NA_COL_BLOCKS, seq // NA_COL_BLOCKS, LANES),
                                        lambda b, p: (b, part * pairs + p, 0, 0, 0))
    return pl.pallas_call(
        functools.partial(_na_kernel, rows=rows),
        grid=(batch, pairs),
        in_specs=[q_spec, kv_spec(0), kv_spec(1),
                  pl.BlockSpec((2, NA_KH, GRID_W, NA_KH * GRID_W), lambda b, p: (p, 0, 0, 0))],
        out_specs=q_spec,
        out_shape=jax.ShapeDtypeStruct((batch, pairs, seq, LANES), BF16),
        compiler_params=pltpu.CompilerParams(dimension_semantics=("parallel", "arbitrary"),
                                             vmem_limit_bytes=VMEM_LIMIT),
        name="na",
    )(na_q, na_kv, na_kv, bias)


def _gla_constants(reverse):
    c = GLA_CHUNK
    t = np.arange(c)
    flip = (lambda a: a[::-1, ::-1]) if reverse else (lambda a: a)
    sums = [flip((t[None, :] <= t[:, None]).astype(np.float32))]
    fine = [np.eye(c, dtype=np.float32)]
    coarse = []
    for lvl in range(GLA_LEVELS):
        m = 1 << lvl
        base = (t // (2 * m)) * (2 * m)
        upper = (t % (2 * m)) >= m
        s = t[None, :]
        mask = flip((upper[:, None] & (~upper)[None, :] & (base[:, None] == base[None, :])).astype(np.float32))
        if lvl < GLA_FINE_LEVELS:
            q_rows = upper[:, None] & (s >= (base + m)[:, None]) & (s <= t[:, None])
            k_rows = (~upper)[:, None] & (s > t[:, None]) & (s <= (base + m - 1)[:, None])
            sums.append(flip((q_rows | k_rows).astype(np.float32)))
            fine.append(mask)
        else:
            coarse.append(mask[mask.any(axis=1)])
    two_heads = lambda mk: np.concatenate([mk, mk], axis=0)
    return (jnp.asarray(np.concatenate(sums, axis=0), BF16),
            jnp.asarray(np.stack([two_heads(mk) for mk in fine]), F32),
            jnp.asarray(np.stack([two_heads(mk) for mk in coarse]), F32))


def _stack_heads(x):
    first = lax.broadcasted_iota(jnp.int32, x.shape, 1) < GLA_DK
    zero = jnp.zeros_like(x)
    return jnp.concatenate([jnp.where(first, x, zero), jnp.where(first, zero, x)], axis=0)


class _GlaUnit:
    def __init__(self, q_ref, k_ref, v_ref, la_ref, sums_ref, fine_ref, coarse_ref, state_ref, o_ref, reverse,
                 chunk, pair):
        self.refs = (fine_ref, coarse_ref, state_ref, o_ref)
        self.reverse, self.pair = reverse, pair
        c, w = GLA_CHUNK, LANES
        self.rows = rows = slice(chunk * c, (chunk + 1) * c)
        ks = slice(pair * LANES, (pair + 1) * LANES)
        la = la_ref[0, rows, ks]
        la_hi = la.astype(BF16)
        la_lo = (la - la_hi.astype(F32)).astype(BF16)
        sums = _dot(sums_ref[...], jnp.concatenate([la_hi, la_lo], axis=1))
        sums = sums[:, :w] + sums[:, w:]
        cum = sums[:c]
        last_row = 0 if reverse else c - 1
        cum_last = cum[last_row:last_row + 1]
        q = q_ref[0, rows, ks].astype(F32) * (GLA_DK ** -0.5)
        k = k_ref[0, rows, ks].astype(F32)
        self.v = v_ref[0, rows, pair * 2 * GLA_DV:(pair + 1) * 2 * GLA_DV]
        self.q_lvl = [q.astype(BF16)]
        self.k_lvl = [k.astype(BF16)]
        for lvl in range(GLA_FINE_LEVELS):
            e = jnp.exp(sums[(lvl + 1) * c:(lvl + 2) * c])
            self.q_lvl.append((q * e).astype(BF16))
            self.k_lvl.append((k * e).astype(BF16))
        for lvl in range(GLA_FINE_LEVELS, GLA_LEVELS):
            m = 1 << lvl
            split = lambda x: x.reshape(c // (2 * m), 2 * m, w)
            cum3, q3, k3 = split(cum), split(q), split(k)
            lo, hi = slice(0, m), slice(m, 2 * m)
            qh, kh = (lo, hi) if reverse else (hi, lo)
            ref = cum3[:, m:m + 1] if reverse else cum3[:, m - 1:m]
            self.q_lvl.append((q3[:, qh] * jnp.exp(cum3[:, qh] - ref)).reshape(c // 2, w).astype(BF16))
            k_scaled = k3[:, kh] * jnp.exp(ref - cum3[:, kh])
            k_parts = [k3[:, lo], k_scaled] if reverse else [k_scaled, k3[:, hi]]
            self.k_lvl.append(jnp.concatenate(k_parts, axis=1).reshape(c, w).astype(BF16))
        self.q_dec = (q * jnp.exp(cum)).astype(BF16)
        self.k_dec = (k * jnp.exp(cum_last - cum)).astype(BF16)
        self.decay = jnp.exp(cum_last)

    def scores(self):
        fine_ref, coarse_ref, _, _ = self.refs
        c = GLA_CHUNK
        level = lambda lvl: _dot_nt(_stack_heads(self.q_lvl[lvl]), self.k_lvl[lvl])
        scores = fine_ref[0] * level(0)
        for lvl in range(GLA_FINE_LEVELS):
            scores += fine_ref[lvl + 1] * level(lvl + 1)
        for lvl in range(GLA_FINE_LEVELS, GLA_LEVELS):
            m = 1 << lvl
            x = (coarse_ref[lvl - GLA_FINE_LEVELS] * level(lvl + 1)).reshape(2 * c // (2 * m), m, c)
            zero = jnp.zeros_like(x)
            scores += jnp.concatenate([x, zero] if self.reverse else [zero, x], axis=1).reshape(2 * c, c)
        self.s = scores.astype(BF16)

    def output(self):
        _, _, state_ref, o_ref = self.refs
        c, pair = GLA_CHUNK, self.pair
        state = state_ref[pair]
        v_t = self.v.T
        lhs = jnp.concatenate([self.s, _stack_heads(self.q_dec)], axis=1)
        rhs = jnp.concatenate([v_t, state.astype(BF16)], axis=1)
        o = _dot_nt(lhs, rhs)
        o_ref[0, self.rows, pair * 2 * GLA_DV:pair * 2 * GLA_DV + GLA_DV] = o[:c, :GLA_DV].astype(o_ref.dtype)
        o_ref[0, self.rows, pair * 2 * GLA_DV + GLA_DV:(pair + 1) * 2 * GLA_DV] = o[c:, GLA_DV:].astype(o_ref.dtype)
        upd = _dot(v_t, self.k_dec)
        row = lax.broadcasted_iota(jnp.int32, upd.shape, 0)
        col = lax.broadcasted_iota(jnp.int32, upd.shape, 1)
        own = (row < GLA_DV) == (col < GLA_DK)
        state_ref[pair] = state * self.decay + jnp.where(own, upd, 0.0)


def _gla_kernel(qf_ref, kf_ref, vf_ref, laf_ref, qb_ref, kb_ref, vb_ref, lab_ref,
                sums_f_ref, fine_f_ref, coarse_f_ref, sums_b_ref, fine_b_ref, coarse_b_ref,
                of_ref, ob_ref, state_f_ref, state_b_ref):
    @pl.when(pl.program_id(1) == 0)
    def _():
        state_f_ref[...] = jnp.zeros_like(state_f_ref)
        state_b_ref[...] = jnp.zeros_like(state_b_ref)

    fwd = (qf_ref, kf_ref, vf_ref, laf_ref, sums_f_ref, fine_f_ref, coarse_f_ref, state_f_ref, of_ref, False)
    bwd = (qb_ref, kb_ref, vb_ref, lab_ref, sums_b_ref, fine_b_ref, coarse_b_ref, state_b_ref, ob_ref, True)
    order = []
    for j in range(GLA_CHUNKS_PER_STEP):
        for pair in range(GLA_HEADS // 2):
            order.append(fwd + (j, pair))
            order.append(bwd + (GLA_CHUNKS_PER_STEP - 1 - j, pair))
    units = [_GlaUnit(*order[0])]
    for i in range(len(order)):
        if i + 1 < len(order):
            units.append(_GlaUnit(*order[i + 1]))
        units[i].scores()
        if i > 0:
            units[i - 1].output()
    units[-1].output()


def _gla_call(q, k, v, la_f, la_b, *, batch, seq):
    c = GLA_CHUNK * GLA_CHUNKS_PER_STEP
    assert seq % c == 0
    n = seq // c
    fwd = lambda b, i: i
    bwd = lambda b, i: n - 1 - i
    consts = _gla_constants(False) + _gla_constants(True)
    spec = lambda chunk, w: pl.BlockSpec((1, c, w), lambda b, i: (b, chunk(b, i), 0))
    specs = lambda chunk: [spec(chunk, GLA_KEY_WIDTH)] * 2 + [spec(chunk, GLA_VAL_WIDTH), spec(chunk, GLA_KEY_WIDTH)]
    state = pltpu.VMEM((GLA_HEADS // 2, 2 * GLA_DV, LANES), F32)
    b3 = lambda a: a.reshape(batch, seq, a.shape[-1])
    q, k, v, la_f, la_b = b3(q), b3(k), b3(v), b3(la_f), b3(la_b)
    return pl.pallas_call(
        _gla_kernel,
        grid=(batch, n),
        in_specs=specs(fwd) + specs(bwd) + [_const_spec(a.shape) for a in consts],
        out_specs=[spec(fwd, GLA_VAL_WIDTH), spec(bwd, GLA_VAL_WIDTH)],
        out_shape=[jax.ShapeDtypeStruct((batch, seq, GLA_VAL_WIDTH), BF16)] * 2,
        scratch_shapes=[state, state],
        compiler_params=pltpu.CompilerParams(dimension_semantics=("parallel", "arbitrary"),
                                             vmem_limit_bytes=VMEM_LIMIT),
        name="gla",
    )(q, k, v, la_f, q, k, v, la_b, *consts)


def _tail_kernel(x_ref, nao_ref, of_ref, ob_ref, g_ref, gate_ref, gng_ref, wna_ref, wgla_ref, wout_ref,
                 gm_ref, wup_ref, wdown_ref, gf_ref, y_ref):
    na_o = jnp.concatenate([nao_ref[0, p] for p in range(NA_WIDTH // LANES)], axis=1)
    na_out = _dot(na_o, wna_ref[...])
    o = of_ref[...].astype(F32) + ob_ref[...].astype(F32)
    heads = []
    for h in range(GLA_HEADS):
        heads.append(_rms(o[:, h * GLA_DV:(h + 1) * GLA_DV]) * gng_ref[...])
    g = g_ref[...].astype(F32)
    gla_o = jnp.concatenate(heads, axis=1) * (g * _sigmoid(g))
    gla_out = _dot(gla_o.astype(BF16), wgla_ref[...])
    gate = gate_ref[...].astype(F32)
    merged = gate[:, :D_MODEL] * na_out + gate[:, D_MODEL:] * gla_out
    h = x_ref[...] + _dot(merged.astype(BF16), wout_ref[...])
    u = (_rms(h) * gm_ref[...]).astype(BF16)
    width = D_FF // MLP_SLICES
    for j in range(MLP_SLICES):
        cols = slice(j * width, (j + 1) * width)
        hdn = jnp.square(jnp.maximum(_dot(u, wup_ref[:, cols]), 0.0)).astype(BF16)
        h = h + _dot(hdn, wdown_ref[cols, :])
    y_ref[...] = _rms(h) * gf_ref[...]


def _tail_call(x2, na_o, o_f, o_b, gla_g, gate, gng, wna, wgla, wout, gm, wup, wdown, gf, *, tm):
    n = x2.shape[0]
    tiles = na_o.shape[2] // tm
    row = lambda w: pl.BlockSpec((tm, w), lambda i: (i, 0))
    nao_spec = pl.BlockSpec((1, NA_WIDTH // LANES, tm, LANES), lambda i: (i // tiles, 0, i % tiles, 0))
    consts = (gng, wna, wgla, wout, gm, wup, wdown, gf)
    return pl.pallas_call(
        _tail_kernel,
        grid=(n // tm,),
        in_specs=[row(D_MODEL), nao_spec, row(GLA_VAL_WIDTH), row(GLA_VAL_WIDTH),
                  row(GLA_VAL_WIDTH), row(2 * D_MODEL)] + [_const_spec(c.shape) for c in consts],
        out_specs=row(D_MODEL),
        out_shape=jax.ShapeDtypeStruct((n, D_MODEL), F32),
        compiler_params=pltpu.CompilerParams(dimension_semantics=("parallel",), vmem_limit_bytes=VMEM_LIMIT),
        name="tail",
    )(x2, na_o, o_f, o_b, gla_g, gate, *consts)


def _prepare(norm_mix_g, w_in, b_in, na_rpb, gk_fwd_w, gk_fwd_b, gk_bwd_w, gk_bwd_b, gla_norm_g,
             w_br_na, w_br_gla, w_out, norm_mlp_g, w_up, w_down, norm_final_g):
    edges = np.concatenate([[0], np.cumsum(IN_WIDTHS)])
    wcols = lambda a, b: w_in[:, edges[a]:edges[b]]
    bcols = lambda a, b: b_in[None, edges[a]:edges[b]]
    r = GLA_GATE_RANK
    pad = LR_PAD - 2 * r
    wlr = jnp.pad(wcols(7, 9), ((0, 0), (0, pad)))
    blr = jnp.pad(bcols(7, 9), ((0, 0), (0, pad)))
    gkw = jnp.zeros((LR_PAD, 2 * GLA_KEY_WIDTH), F32)
    gkw = gkw.at[:r, :GLA_KEY_WIDTH].set(gk_fwd_w).at[r:2 * r, GLA_KEY_WIDTH:].set(gk_bwd_w)
    gkb = jnp.concatenate([gk_fwd_b, gk_bwd_b])[None]
    return dict(
        proj=(norm_mix_g[None], wcols(0, 1).astype(BF16), bcols(0, 1), wcols(1, 3).astype(BF16), bcols(1, 3),
              wcols(3, 7).astype(BF16), bcols(3, 7),
              wcols(9, 11).astype(BF16), bcols(9, 11), wlr.astype(BF16), blr, gkw.astype(BF16), gkb),
        na_bias=_na_bias_table(na_rpb),
        tail=(gla_norm_g[None], w_br_na.astype(BF16), w_br_gla.astype(BF16), w_out.astype(BF16),
              norm_mlp_g[None], w_up.astype(BF16), w_down.astype(BF16), norm_final_g[None]),
    )


def _trunk(x, prep, *, tm):
    batch, seq, _ = x.shape
    x2 = x.reshape(batch * seq, D_MODEL)
    na_q, na_kv, gla_q, gla_k, gla_v, gla_g, gate, la_f, la_b = _proj_call(
        x2, *prep["proj"], tm=tm, batch=batch, seq=seq)
    na_o = _na_call(na_q, na_kv, prep["na_bias"], batch=batch, seq=seq)
    o_f, o_b = _gla_call(gla_q, gla_k, gla_v, la_f, la_b, batch=batch, seq=seq)
    y = _tail_call(x2, na_o, o_f.reshape(batch * seq, GLA_VAL_WIDTH), o_b.reshape(batch * seq, GLA_VAL_WIDTH),
                   gla_g, gate, *prep["tail"], tm=tm)
    return y.reshape(batch, seq, D_MODEL)


def kernel(x_prompt, x_sample, norm_mix_g, w_in, b_in, na_rpb, gk_fwd_w, gk_fwd_b, gk_bwd_w, gk_bwd_b, gla_norm_g, w_br_na, w_br_gla, w_out, norm_mlp_g, w_up, w_down, norm_final_g):
    assert norm_mix_g.shape[0] == 1, "single-layer trunk"
    prep = _prepare(norm_mix_g[0], w_in[0], b_in[0], na_rpb[0], gk_fwd_w[0], gk_fwd_b[0], gk_bwd_w[0],
                    gk_bwd_b[0], gla_norm_g[0], w_br_na[0], w_br_gla[0], w_out[0], norm_mlp_g[0], w_up[0],
                    w_down[0], norm_final_g)
    return _trunk(x_prompt, prep, tm=512), _trunk(x_sample, prep, tm=512)
```

```python
import functools

import numpy as np
import jax
import jax.numpy as jnp
from jax import lax
from jax.experimental import pallas as pl
from jax.experimental.pallas import tpu as pltpu

D_MODEL = 1024
GRID_W = 64
NA_HEADS = 8
NA_HEAD_DIM = 64
NA_WIDTH = NA_HEADS * NA_HEAD_DIM
NA_KH = 8
NA_KW = 16
GLA_HEADS = 4
GLA_KEY_WIDTH = 256
GLA_VAL_WIDTH = 512
GLA_DK = 64
GLA_DV = 128
GLA_GATE_RANK = 16
GLA_GATE_NORMALIZER = 16.0
D_FF = 4 * D_MODEL
RMS_EPS = 1e-6
IN_WIDTHS = (NA_WIDTH, NA_WIDTH, NA_WIDTH, GLA_KEY_WIDTH, GLA_KEY_WIDTH, GLA_VAL_WIDTH, GLA_VAL_WIDTH,
             GLA_GATE_RANK, GLA_GATE_RANK, D_MODEL, D_MODEL)

LANES = 128
BF16_ROWS = 16
GLA_CHUNK = 128
GLA_LEVELS = 7
GLA_CHUNKS_PER_STEP = 4
GLA_FINE_LEVELS = 3
LR_PAD = LANES
NA_COL_BLOCKS = GRID_W // NA_KW
NA_QGROUP = 8
NA_ROWS_PER_STEP = 16
NA_LOOKAHEAD = 6
MLP_SLICES = 4
VMEM_LIMIT = 56 * 1024 * 1024

BF16 = jnp.bfloat16
F32 = jnp.float32


def _dot(a, b):
    return jnp.dot(a, b, preferred_element_type=F32)


def _dot_nt(a, b):
    return lax.dot_general(a, b, (((1,), (1,)), ((), ())), preferred_element_type=F32)


def _sigmoid(z):
    return 1.0 / (1.0 + jnp.exp(-z))


def _rms(x):
    return x * lax.rsqrt(jnp.mean(x * x, axis=-1, keepdims=True) + RMS_EPS)


def _proj_kernel(x_ref, g_ref, wq_ref, bq_ref, wkv_ref, bkv_ref, wgla_ref, bgla_ref, wgate_ref, bgate_ref,
                 wlr_ref, blr_ref, gkw_ref, gkb_ref,
                 q_ref, kv_ref, gq_ref, gk_ref, gv_ref, gg_ref, gate_ref, laf_ref, lab_ref):
    xb = (_rms(x_ref[...]) * g_ref[...]).astype(BF16)
    q = (_dot(xb, wq_ref[...]) + bq_ref[...]).astype(BF16)
    for p in range(NA_WIDTH // LANES):
        q_ref[0, p] = q[:, p * LANES:(p + 1) * LANES]
    kv = (_dot(xb, wkv_ref[...]) + bkv_ref[...]).astype(BF16)
    kv = kv.reshape(kv.shape[0] // GRID_W, NA_COL_BLOCKS, NA_KW, kv.shape[1])
    for cb in range(NA_COL_BLOCKS):
        block = kv[:, cb].reshape(-1, kv.shape[3])
        for p in range(2 * NA_WIDTH // LANES):
            kv_ref[0, p, cb] = block[:, p * LANES:(p + 1) * LANES]
    gla = (_dot(xb, wgla_ref[...]) + bgla_ref[...]).astype(BF16)
    w = GLA_KEY_WIDTH
    gq_ref[...] = gla[:, :w]
    gk_ref[...] = gla[:, w:2 * w]
    gv_ref[...] = gla[:, 2 * w:2 * w + GLA_VAL_WIDTH]
    gg_ref[...] = gla[:, 2 * w + GLA_VAL_WIDTH:]
    gate_ref[...] = _sigmoid(_dot(xb, wgate_ref[...]) + bgate_ref[...]).astype(BF16)
    lr = (_dot(xb, wlr_ref[...]) + blr_ref[...]).astype(BF16)
    z = _dot(lr, gkw_ref[...]) + gkb_ref[...]
    log_a = (jnp.minimum(z, 0.0) - jnp.log1p(jnp.exp(-jnp.abs(z)))) * (1.0 / GLA_GATE_NORMALIZER)
    laf_ref[...] = log_a[:, :w]
    lab_ref[...] = log_a[:, w:]


def _const_spec(shape):
    nd = len(shape)
    return pl.BlockSpec(shape, lambda *_: (0,) * nd, pipeline_mode=pl.Buffered(1))


def _proj_call(x2, g, wq, bq, wkv, bkv, wgla, bgla, wgate, bgate, wlr, blr, gkw, gkb, *, tm, batch, seq):
    n = x2.shape[0]
    assert seq % tm == 0 and tm % GRID_W == 0
    tiles = seq // tm
    pairs = NA_WIDTH // LANES
    row = lambda w: pl.BlockSpec((tm, w), lambda i: (i, 0))
    consts = (g, wq, bq, wkv, bkv, wgla, bgla, wgate, bgate, wlr, blr, gkw, gkb)
    q_spec = pl.BlockSpec((1, pairs, tm, LANES), lambda i: (i // tiles, 0, i % tiles, 0))
    kv_spec = pl.BlockSpec((1, 2 * pairs, NA_COL_BLOCKS, tm // NA_COL_BLOCKS, LANES),
                           lambda i: (i // tiles, 0, 0, i % tiles, 0))
    widths = (GLA_KEY_WIDTH, GLA_KEY_WIDTH, GLA_VAL_WIDTH, GLA_VAL_WIDTH, 2 * D_MODEL)
    return pl.pallas_call(
        _proj_kernel,
        grid=(n // tm,),
        in_specs=[row(D_MODEL)] + [_const_spec(c.shape) for c in consts],
        out_specs=[q_spec, kv_spec] + [row(w) for w in widths] + [row(GLA_KEY_WIDTH)] * 2,
        out_shape=[jax.ShapeDtypeStruct((batch, pairs, seq, LANES), BF16),
                   jax.ShapeDtypeStruct((batch, 2 * pairs, NA_COL_BLOCKS, seq // NA_COL_BLOCKS, LANES), BF16)]
        + [jax.ShapeDtypeStruct((n, w), BF16) for w in widths]
        + [jax.ShapeDtypeStruct((n, GLA_KEY_WIDTH), F32)] * 2,
        compiler_params=pltpu.CompilerParams(dimension_semantics=("parallel",), vmem_limit_bytes=VMEM_LIMIT),
        name="proj",
    )(x2, *consts)


def _na_needed_blocks():
    need = []
    for g in range(GRID_W // NA_QGROUP):
        c = np.arange(g * NA_QGROUP, (g + 1) * NA_QGROUP)
        start = np.clip(c - NA_KW // 2, 0, GRID_W - NA_KW)
        need.append(sorted(set((start // NA_KW).tolist()) | set(((start + NA_KW - 1) // NA_KW).tolist())))
    return need


NA_NEED = _na_needed_blocks()
NA_HALF_ROWS = []
for _half in range(2):
    _groups = [g for g, blocks in enumerate(NA_NEED) if any(b // 2 == _half for b in blocks)]
    _lo = (min(_groups) * NA_QGROUP) // BF16_ROWS * BF16_ROWS
    _hi = -(-((max(_groups) + 1) * NA_QGROUP) // BF16_ROWS) * BF16_ROWS
    NA_HALF_ROWS.append((_lo, _hi))


def _na_bias_table(rpb):
    c = np.arange(GRID_W)
    col_start = np.clip(c - NA_KW // 2, 0, GRID_W - NA_KW)
    col_mask = (c[None, :] >= col_start[:, None]) & (c[None, :] < col_start[:, None] + NA_KW)
    dc_idx = np.clip(c[None, :] - c[:, None], -(NA_KW - 1), NA_KW - 1) + (NA_KW - 1)
    onehot = (dc_idx[None] == np.arange(2 * NA_KW - 1)[:, None, None]) & col_mask[None]
    cols = jnp.einsum('hrd,dqk->hrqk', rpb.astype(F32), jnp.asarray(onehot, F32),
                      precision=lax.Precision.HIGHEST)
    cols = jnp.where(col_mask[None, None], cols, -jnp.inf)
    bias = jnp.stack([cols[:, NA_KH - 1 - off:2 * NA_KH - 1 - off] for off in range(NA_KH)], axis=1)
    heads = rpb.shape[0]
    bias = bias.reshape(heads, NA_KH, NA_KH, GRID_W, NA_COL_BLOCKS, NA_KW)
    bias = bias.transpose(0, 1, 3, 4, 2, 5)
    return bias.reshape(heads, NA_KH, GRID_W, NA_KH * GRID_W)


def _na_kernel(q_ref, k_ref, v_ref, bias_ref, o_ref, *, rows):
    lane = lax.broadcasted_iota(jnp.int32, (GRID_W, LANES), 1)
    first = lane < NA_HEAD_DIM
    win = NA_KH * NA_KW
    groups = GRID_W // NA_QGROUP

    def window(r):
        start = jnp.clip(r - NA_KH // 2, 0, rows - NA_KH)
        return pl.ds(pl.multiple_of(start * NA_KW, NA_KW), win), r - start

    def tile(ref, keys, half):
        return jnp.concatenate([ref[0, 0, 2 * half, keys, :], ref[0, 0, 2 * half + 1, keys, :]], axis=0)

    def half_rows(x, half):
        lo, hi = NA_HALF_ROWS[half]
        return jnp.concatenate([x[lo:hi], x[GRID_W + lo:GRID_W + hi]], axis=0)

    def scores(r):
        keys, _ = window(r)
        q = q_ref[0, 0, pl.ds(pl.multiple_of(r * GRID_W, GRID_W), GRID_W), :] * (NA_HEAD_DIM ** -0.5)
        zero = jnp.zeros_like(q)
        qs = jnp.concatenate([jnp.where(first, q, zero), jnp.where(first, zero, q)], axis=0)
        return [_dot_nt(half_rows(qs, half), tile(k_ref, keys, half)) for half in range(2)]

    def attend(r, s):
        keys, off = window(r)
        p_rows = [[], []]
        inv = []
        for head in range(2):
            for g in range(groups):
                rows8 = slice(g * NA_QGROUP, (g + 1) * NA_QGROUP)
                pieces = {}
                for cb in NA_NEED[g]:
                    half = cb // 2
                    lo, hi = NA_HALF_ROWS[half]
                    at = head * (hi - lo) + g * NA_QGROUP - lo
                    pieces[cb] = (s[half][at:at + NA_QGROUP, (cb % 2) * LANES:(cb % 2 + 1) * LANES]
                                  + bias_ref[head, off, rows8, cb * LANES:(cb + 1) * LANES])
                m = functools.reduce(jnp.maximum, pieces.values())
                m = jnp.max(m, axis=-1, keepdims=True)
                pieces = {cb: jnp.exp(x - m) for cb, x in pieces.items()}
                total = functools.reduce(jnp.add, pieces.values())
                inv.append(1.0 / jnp.sum(total, axis=-1, keepdims=True))
                for half in range(2):
                    lo, hi = NA_HALF_ROWS[half]
                    if lo <= g * NA_QGROUP < hi:
                        zero = jnp.zeros((NA_QGROUP, LANES), F32)
                        p_rows[half].append(jnp.concatenate(
                            [pieces.get(2 * half, zero), pieces.get(2 * half + 1, zero)], axis=1))
        outs = []
        for half in range(2):
            p = jnp.concatenate(p_rows[half], axis=0).astype(BF16)
            outs.append(_dot(p, tile(v_ref, keys, half)))
        heads_o = []
        for head in range(2):
            rows_o = []
            for g in range(groups):
                acc = None
                for half in range(2):
                    lo, hi = NA_HALF_ROWS[half]
                    if lo <= g * NA_QGROUP < hi:
                        at = head * (hi - lo) + g * NA_QGROUP - lo
                        part = outs[half][at:at + NA_QGROUP]
                        acc = part if acc is None else acc + part
                rows_o.append(acc * inv[head * groups + g])
            heads_o.append(jnp.concatenate(rows_o, axis=0))
        o_ref[0, 0, pl.ds(pl.multiple_of(r * GRID_W, GRID_W), GRID_W), :] = jnp.where(
            first, heads_o[0], heads_o[1]).astype(BF16)

    def body(i, carry):
        rs = [i * NA_ROWS_PER_STEP + u for u in range(NA_ROWS_PER_STEP)]
        ss = [scores(r) for r in rs[:NA_LOOKAHEAD]]
        for u, r in enumerate(rs):
            if u + NA_LOOKAHEAD < len(rs):
                ss.append(scores(rs[u + NA_LOOKAHEAD]))
            attend(r, ss[u])
        return carry

    lax.fori_loop(0, rows // NA_ROWS_PER_STEP, body, 0)


def _na_call(na_q, na_kv, bias, *, batch, seq):
    rows = seq // GRID_W
    assert rows >= NA_KH and seq % GRID_W == 0 and rows % NA_ROWS_PER_STEP == 0
    pairs = NA_WIDTH // LANES
    q_spec = pl.BlockSpec((1, 1, seq, LANES), lambda p, b: (b, p, 0, 0))
    kv_spec = lambda part: pl.BlockSpec((1, 1, NA_COL_BLOCKS, seq // NA_COL_BLOCKS, LANES),
                                        lambda p, b: (b, part * pairs + p, 0, 0, 0))
    return pl.pallas_call(
        functools.partial(_na_kernel, rows=rows),
        grid=(pairs, batch),
        in_specs=[q_spec, kv_spec(0), kv_spec(1),
                  pl.BlockSpec((2, NA_KH, GRID_W, NA_KH * GRID_W), lambda p, b: (p, 0, 0, 0))],
        out_specs=q_spec,
        out_shape=jax.ShapeDtypeStruct((batch, pairs, seq, LANES), BF16),
        compiler_params=pltpu.CompilerParams(dimension_semantics=("parallel", "arbitrary"),
                                             vmem_limit_bytes=VMEM_LIMIT),
        name="na",
    )(na_q, na_kv, na_kv, bias)


def _gla_constants(reverse):
    c = GLA_CHUNK
    t = np.arange(c)
    flip = (lambda a: a[::-1, ::-1]) if reverse else (lambda a: a)
    sums = [flip((t[None, :] <= t[:, None]).astype(np.float32))]
    fine = [np.eye(c, dtype=np.float32)]
    coarse = []
    for lvl in range(GLA_LEVELS):
        m = 1 << lvl
        base = (t // (2 * m)) * (2 * m)
        upper = (t % (2 * m)) >= m
        s = t[None, :]
        mask = flip((upper[:, None] & (~upper)[None, :] & (base[:, None] == base[None, :])).astype(np.float32))
        if lvl < GLA_FINE_LEVELS:
            q_rows = upper[:, None] & (s >= (base + m)[:, None]) & (s <= t[:, None])
            k_rows = (~upper)[:, None] & (s > t[:, None]) & (s <= (base + m - 1)[:, None])
            sums.append(flip((q_rows | k_rows).astype(np.float32)))
            fine.append(mask)
        else:
            coarse.append(mask[mask.any(axis=1)])
    two_heads = lambda mk: np.concatenate([mk, mk], axis=0)
    return (jnp.asarray(np.concatenate(sums, axis=0), BF16),
            jnp.asarray(np.stack([two_heads(mk) for mk in fine]), F32),
            jnp.asarray(np.stack([two_heads(mk) for mk in coarse]), F32))


def _stack_heads(x):
    first = lax.broadcasted_iota(jnp.int32, x.shape, 1) < GLA_DK
    zero = jnp.zeros_like(x)
    return jnp.concatenate([jnp.where(first, x, zero), jnp.where(first, zero, x)], axis=0)


class _GlaUnit:
    def __init__(self, q_ref, k_ref, v_ref, la_ref, sums_ref, fine_ref, coarse_ref, state_ref, o_ref, reverse,
                 chunk, pair):
        self.refs = (fine_ref, coarse_ref, state_ref, o_ref)
        self.reverse, self.pair = reverse, pair
        c, w = GLA_CHUNK, LANES
        self.rows = rows = slice(chunk * c, (chunk + 1) * c)
        ks = slice(pair * LANES, (pair + 1) * LANES)
        la = la_ref[0, rows, ks]
        la_hi = la.astype(BF16)
        la_lo = (la - la_hi.astype(F32)).astype(BF16)
        sums = _dot(sums_ref[...], jnp.concatenate([la_hi, la_lo], axis=1))
        sums = sums[:, :w] + sums[:, w:]
        cum = sums[:c]
        last_row = 0 if reverse else c - 1
        cum_last = cum[last_row:last_row + 1]
        q = q_ref[0, rows, ks].astype(F32) * (GLA_DK ** -0.5)
        k = k_ref[0, rows, ks].astype(F32)
        self.v = v_ref[0, rows, pair * 2 * GLA_DV:(pair + 1) * 2 * GLA_DV]
        self.q_lvl = [q.astype(BF16)]
        self.k_lvl = [k.astype(BF16)]
        for lvl in range(GLA_FINE_LEVELS):
            e = jnp.exp(sums[(lvl + 1) * c:(lvl + 2) * c])
            self.q_lvl.append((q * e).astype(BF16))
            self.k_lvl.append((k * e).astype(BF16))
        for lvl in range(GLA_FINE_LEVELS, GLA_LEVELS):
            m = 1 << lvl
            split = lambda x: x.reshape(c // (2 * m), 2 * m, w)
            cum3, q3, k3 = split(cum), split(q), split(k)
            lo, hi = slice(0, m), slice(m, 2 * m)
            qh, kh = (lo, hi) if reverse else (hi, lo)
            ref = cum3[:, m:m + 1] if reverse else cum3[:, m - 1:m]
            self.q_lvl.append((q3[:, qh] * jnp.exp(cum3[:, qh] - ref)).reshape(c // 2, w).astype(BF16))
            k_scaled = k3[:, kh] * jnp.exp(ref - cum3[:, kh])
            k_parts = [k3[:, lo], k_scaled] if reverse else [k_scaled, k3[:, hi]]
            self.k_lvl.append(jnp.concatenate(k_parts, axis=1).reshape(c, w).astype(BF16))
        self.q_dec = (q * jnp.exp(cum)).astype(BF16)
        self.k_dec = (k * jnp.exp(cum_last - cum)).astype(BF16)
        self.decay = jnp.exp(cum_last)

    def scores(self):
        fine_ref, coarse_ref, _, _ = self.refs
        c = GLA_CHUNK
        level = lambda lvl: _dot_nt(_stack_heads(self.q_lvl[lvl]), self.k_lvl[lvl])
        scores = fine_ref[0] * level(0)
        for lvl in range(GLA_FINE_LEVELS):
            scores += fine_ref[lvl + 1] * level(lvl + 1)
        for lvl in range(GLA_FINE_LEVELS, GLA_LEVELS):
            m = 1 << lvl
            x = (coarse_ref[lvl - GLA_FINE_LEVELS] * level(lvl + 1)).reshape(2 * c // (2 * m), m, c)
            zero = jnp.zeros_like(x)
            scores += jnp.concatenate([x, zero] if self.reverse else [zero, x], axis=1).reshape(2 * c, c)
        self.s = scores.astype(BF16)

    def output(self):
        _, _, state_ref, o_ref = self.refs
        c, pair = GLA_CHUNK, self.pair
        state = state_ref[pair]
        v_t = self.v.T
        lhs = jnp.concatenate([self.s, _stack_heads(self.q_dec)], axis=1)
        rhs = jnp.concatenate([v_t, state.astype(BF16)], axis=1)
        o = _dot_nt(lhs, rhs)
        o_ref[0, self.rows, pair * 2 * GLA_DV:pair * 2 * GLA_DV + GLA_DV] = o[:c, :GLA_DV].astype(o_ref.dtype)
        o_ref[0, self.rows, pair * 2 * GLA_DV + GLA_DV:(pair + 1) * 2 * GLA_DV] = o[c:, GLA_DV:].astype(o_ref.dtype)
        upd = _dot(v_t, self.k_dec)
        row = lax.broadcasted_iota(jnp.int32, upd.shape, 0)
        col = lax.broadcasted_iota(jnp.int32, upd.shape, 1)
        own = (row < GLA_DV) == (col < GLA_DK)
        state_ref[pair] = state * self.decay + jnp.where(own, upd, 0.0)


def _gla_kernel(qf_ref, kf_ref, vf_ref, laf_ref, qb_ref, kb_ref, vb_ref, lab_ref,
                sums_f_ref, fine_f_ref, coarse_f_ref, sums_b_ref, fine_b_ref, coarse_b_ref,
                of_ref, ob_ref, state_f_ref, state_b_ref):
    @pl.when(pl.program_id(1) == 0)
    def _():
        state_f_ref[...] = jnp.zeros_like(state_f_ref)
        state_b_ref[...] = jnp.zeros_like(state_b_ref)

    fwd = (qf_ref, kf_ref, vf_ref, laf_ref, sums_f_ref, fine_f_ref, coarse_f_ref, state_f_ref, of_ref, False)
    bwd = (qb_ref, kb_ref, vb_ref, lab_ref, sums_b_ref, fine_b_ref, coarse_b_ref, state_b_ref, ob_ref, True)
    order = []
    for j in range(GLA_CHUNKS_PER_STEP):
        for pair in range(GLA_HEADS // 2):
            order.append(fwd + (j, pair))
            order.append(bwd + (GLA_CHUNKS_PER_STEP - 1 - j, pair))
    units = [_GlaUnit(*order[0])]
    for i in range(len(order)):
        if i + 1 < len(order):
            units.append(_GlaUnit(*order[i + 1]))
        units[i].scores()
        if i > 0:
            units[i - 1].output()
    units[-1].output()


def _gla_call(q, k, v, la_f, la_b, *, batch, seq):
    c = GLA_CHUNK * GLA_CHUNKS_PER_STEP
    assert seq % c == 0
    n = seq // c
    fwd = lambda b, i: i
    bwd = lambda b, i: n - 1 - i
    consts = _gla_constants(False) + _gla_constants(True)
    spec = lambda chunk, w: pl.BlockSpec((1, c, w), lambda b, i: (b, chunk(b, i), 0))
    specs = lambda chunk: [spec(chunk, GLA_KEY_WIDTH)] * 2 + [spec(chunk, GLA_VAL_WIDTH), spec(chunk, GLA_KEY_WIDTH)]
    state = pltpu.VMEM((GLA_HEADS // 2, 2 * GLA_DV, LANES), F32)
    b3 = lambda a: a.reshape(batch, seq, a.shape[-1])
    q, k, v, la_f, la_b = b3(q), b3(k), b3(v), b3(la_f), b3(la_b)
    return pl.pallas_call(
        _gla_kernel,
        grid=(batch, n),
        in_specs=specs(fwd) + specs(bwd) + [_const_spec(a.shape) for a in consts],
        out_specs=[spec(fwd, GLA_VAL_WIDTH), spec(bwd, GLA_VAL_WIDTH)],
        out_shape=[jax.ShapeDtypeStruct((batch, seq, GLA_VAL_WIDTH), BF16)] * 2,
        scratch_shapes=[state, state],
        compiler_params=pltpu.CompilerParams(dimension_semantics=("parallel", "arbitrary"),
                                             vmem_limit_bytes=VMEM_LIMIT),
        name="gla",
    )(q, k, v, la_f, q, k, v, la_b, *consts)


def _tail_kernel(x_ref, nao_ref, of_ref, ob_ref, g_ref, gate_ref, gng_ref, wna_ref, wgla_ref, wout_ref,
                 gm_ref, wup_ref, wdown_ref, gf_ref, y_ref):
    na_o = jnp.concatenate([nao_ref[0, p] for p in range(NA_WIDTH // LANES)], axis=1)
    na_out = _dot(na_o, wna_ref[...])
    o = of_ref[...].astype(F32) + ob_ref[...].astype(F32)
    heads = []
    for h in range(GLA_HEADS):
        heads.append(_rms(o[:, h * GLA_DV:(h + 1) * GLA_DV]) * gng_ref[...])
    g = g_ref[...].astype(F32)
    gla_o = jnp.concatenate(heads, axis=1) * (g * _sigmoid(g))
    gla_out = _dot(gla_o.astype(BF16), wgla_ref[...])
    gate = gate_ref[...].astype(F32)
    merged = gate[:, :D_MODEL] * na_out + gate[:, D_MODEL:] * gla_out
    h = x_ref[...] + _dot(merged.astype(BF16), wout_ref[...])
    u = (_rms(h) * gm_ref[...]).astype(BF16)
    width = D_FF // MLP_SLICES
    for j in range(MLP_SLICES):
        cols = slice(j * width, (j + 1) * width)
        hdn = jnp.square(jnp.maximum(_dot(u, wup_ref[:, cols]), 0.0)).astype(BF16)
        h = h + _dot(hdn, wdown_ref[cols, :])
    y_ref[...] = _rms(h) * gf_ref[...]


def _tail_call(x2, na_o, o_f, o_b, gla_g, gate, gng, wna, wgla, wout, gm, wup, wdown, gf, *, tm):
    n = x2.shape[0]
    tiles = na_o.shape[2] // tm
    row = lambda w: pl.BlockSpec((tm, w), lambda i: (i, 0))
    nao_spec = pl.BlockSpec((1, NA_WIDTH // LANES, tm, LANES), lambda i: (i // tiles, 0, i % tiles, 0))
    consts = (gng, wna, wgla, wout, gm, wup, wdown, gf)
    return pl.pallas_call(
        _tail_kernel,
        grid=(n // tm,),
        in_specs=[row(D_MODEL), nao_spec, row(GLA_VAL_WIDTH), row(GLA_VAL_WIDTH),
                  row(GLA_VAL_WIDTH), row(2 * D_MODEL)] + [_const_spec(c.shape) for c in consts],
        out_specs=row(D_MODEL),
        out_shape=jax.ShapeDtypeStruct((n, D_MODEL), F32),
        compiler_params=pltpu.CompilerParams(dimension_semantics=("parallel",), vmem_limit_bytes=VMEM_LIMIT),
        name="tail",
    )(x2, na_o, o_f, o_b, gla_g, gate, *consts)


def _prepare(norm_mix_g, w_in, b_in, na_rpb, gk_fwd_w, gk_fwd_b, gk_bwd_w, gk_bwd_b, gla_norm_g,
             w_br_na, w_br_gla, w_out, norm_mlp_g, w_up, w_down, norm_final_g):
    edges = np.concatenate([[0], np.cumsum(IN_WIDTHS)])
    wcols = lambda a, b: w_in[:, edges[a]:edges[b]]
    bcols = lambda a, b: b_in[None, edges[a]:edges[b]]
    r = GLA_GATE_RANK
    pad = LR_PAD - 2 * r
    wlr = jnp.pad(wcols(7, 9), ((0, 0), (0, pad)))
    blr = jnp.pad(bcols(7, 9), ((0, 0), (0, pad)))
    gkw = jnp.zeros((LR_PAD, 2 * GLA_KEY_WIDTH), F32)
    gkw = gkw.at[:r, :GLA_KEY_WIDTH].set(gk_fwd_w).at[r:2 * r, GLA_KEY_WIDTH:].set(gk_bwd_w)
    gkb = jnp.concatenate([gk_fwd_b, gk_bwd_b])[None]
    return dict(
        proj=(norm_mix_g[None], wcols(0, 1).astype(BF16), bcols(0, 1), wcols(1, 3).astype(BF16), bcols(1, 3),
              wcols(3, 7).astype(BF16), bcols(3, 7),
              wcols(9, 11).astype(BF16), bcols(9, 11), wlr.astype(BF16), blr, gkw.astype(BF16), gkb),
        na_bias=_na_bias_table(na_rpb),
        tail=(gla_norm_g[None], w_br_na.astype(BF16), w_br_gla.astype(BF16), w_out.astype(BF16),
              norm_mlp_g[None], w_up.astype(BF16), w_down.astype(BF16), norm_final_g[None]),
    )


def _trunk(x, prep, *, tm):
    batch, seq, _ = x.shape
    x2 = x.reshape(batch * seq, D_MODEL)
    na_q, na_kv, gla_q, gla_k, gla_v, gla_g, gate, la_f, la_b = _proj_call(
        x2, *prep["proj"], tm=tm, batch=batch, seq=seq)
    na_o = _na_call(na_q, na_kv, prep["na_bias"], batch=batch, seq=seq)
    o_f, o_b = _gla_call(gla_q, gla_k, gla_v, la_f, la_b, batch=batch, seq=seq)
    y = _tail_call(x2, na_o, o_f.reshape(batch * seq, GLA_VAL_WIDTH), o_b.reshape(batch * seq, GLA_VAL_WIDTH),
                   gla_g, gate, *prep["tail"], tm=tm)
    return y.reshape(batch, seq, D_MODEL)


def kernel(x_prompt, x_sample, norm_mix_g, w_in, b_in, na_rpb, gk_fwd_w, gk_fwd_b, gk_bwd_w, gk_bwd_b, gla_norm_g, w_br_na, w_br_gla, w_out, norm_mlp_g, w_up, w_down, norm_final_g):
    assert norm_mix_g.shape[0] == 1, "single-layer trunk"
    prep = _prepare(norm_mix_g[0], w_in[0], b_in[0], na_rpb[0], gk_fwd_w[0], gk_fwd_b[0], gk_bwd_w[0],
                    gk_bwd_b[0], gla_norm_g[0], w_br_na[0], w_br_gla[0], w_out[0], norm_mlp_g[0], w_up[0],
                    w_down[0], norm_final_g)
    return _trunk(x_prompt, prep, tm=512), _trunk(x_sample, prep, tm=512)
```

```python
import functools

import numpy as np
import jax
import jax.numpy as jnp
from jax import lax
from jax.experimental import pallas as pl
from jax.experimental.pallas import tpu as pltpu

D_MODEL = 1024
GRID_W = 64
NA_HEADS = 8
NA_HEAD_DIM = 64
NA_WIDTH = NA_HEADS * NA_HEAD_DIM
NA_KH = 8
NA_KW = 16
GLA_HEADS = 4
GLA_KEY_WIDTH = 256
GLA_VAL_WIDTH = 512
GLA_DK = 64
GLA_DV = 128
GLA_GATE_RANK = 16
GLA_GATE_NORMALIZER = 16.0
D_FF = 4 * D_MODEL
RMS_EPS = 1e-6
IN_WIDTHS = (NA_WIDTH, NA_WIDTH, NA_WIDTH, GLA_KEY_WIDTH, GLA_KEY_WIDTH, GLA_VAL_WIDTH, GLA_VAL_WIDTH,
             GLA_GATE_RANK, GLA_GATE_RANK, D_MODEL, D_MODEL)

LANES = 128
BF16_ROWS = 16
GLA_CHUNK = 128
GLA_LEVELS = 7
GLA_CHUNKS_PER_STEP = 8
GLA_FINE_LEVELS = 3
LR_PAD = LANES
NA_COL_BLOCKS = GRID_W // NA_KW
NA_QGROUP = 8
NA_ROWS_PER_STEP = 16
NA_LOOKAHEAD = 6
MLP_SLICES = 4
VMEM_LIMIT = 56 * 1024 * 1024

BF16 = jnp.bfloat16
F32 = jnp.float32


def _dot(a, b):
    return jnp.dot(a, b, preferred_element_type=F32)


def _dot_nt(a, b):
    return lax.dot_general(a, b, (((1,), (1,)), ((), ())), preferred_element_type=F32)


def _sigmoid(z):
    return 0.5 * jnp.tanh(0.5 * z) + 0.5


def _rms(x):
    return x * lax.rsqrt(jnp.mean(x * x, axis=-1, keepdims=True) + RMS_EPS)


def _proj_kernel(x_ref, g_ref, wq_ref, bq_ref, wkv_ref, bkv_ref, wgla_ref, bgla_ref, wgate_ref, bgate_ref,
                 wlr_ref, blr_ref, gkw_ref, gkb_ref,
                 q_ref, kv_ref, gq_ref, gk_ref, gv_ref, gg_ref, gate_ref, laf_ref, lab_ref):
    xb = (_rms(x_ref[...]) * g_ref[...]).astype(BF16)
    q = (_dot(xb, wq_ref[...]) + bq_ref[...]).astype(BF16)
    for p in range(NA_WIDTH // LANES):
        q_ref[0, p] = q[:, p * LANES:(p + 1) * LANES]
    kv = (_dot(xb, wkv_ref[...]) + bkv_ref[...]).astype(BF16)
    kv = kv.reshape(kv.shape[0] // GRID_W, NA_COL_BLOCKS, NA_KW, kv.shape[1])
    for cb in range(NA_COL_BLOCKS):
        block = kv[:, cb].reshape(-1, kv.shape[3])
        for p in range(2 * NA_WIDTH // LANES):
            kv_ref[0, p, cb] = block[:, p * LANES:(p + 1) * LANES]
    gla = (_dot(xb, wgla_ref[...]) + bgla_ref[...]).astype(BF16)
    w = GLA_KEY_WIDTH
    gq_ref[...] = gla[:, :w]
    gk_ref[...] = gla[:, w:2 * w]
    gv_ref[...] = gla[:, 2 * w:2 * w + GLA_VAL_WIDTH]
    gg_ref[...] = gla[:, 2 * w + GLA_VAL_WIDTH:]
    gate_ref[...] = _sigmoid(_dot(xb, wgate_ref[...]) + bgate_ref[...]).astype(BF16)
    lr = (_dot(xb, wlr_ref[...]) + blr_ref[...]).astype(BF16)
    z = _dot(lr, gkw_ref[...]) + gkb_ref[...]
    log_a = (jnp.minimum(z, 0.0) - jnp.log(1.0 + jnp.exp(-jnp.abs(z)))) * (1.0 / GLA_GATE_NORMALIZER)
    laf_ref[...] = log_a[:, :w]
    lab_ref[...] = log_a[:, w:]


def _const_spec(shape):
    nd = len(shape)
    return pl.BlockSpec(shape, lambda *_: (0,) * nd, pipeline_mode=pl.Buffered(1))


def _proj_call(x2, g, wq, bq, wkv, bkv, wgla, bgla, wgate, bgate, wlr, blr, gkw, gkb, *, tm, batch, seq):
    n = x2.shape[0]
    assert seq % tm == 0 and tm % GRID_W == 0
    tiles = seq // tm
    pairs = NA_WIDTH // LANES
    row = lambda w: pl.BlockSpec((tm, w), lambda i: (i, 0))
    consts = (g, wq, bq, wkv, bkv, wgla, bgla, wgate, bgate, wlr, blr, gkw, gkb)
    q_spec = pl.BlockSpec((1, pairs, tm, LANES), lambda i: (i // tiles, 0, i % tiles, 0))
    kv_spec = pl.BlockSpec((1, 2 * pairs, NA_COL_BLOCKS, tm // NA_COL_BLOCKS, LANES),
                           lambda i: (i // tiles, 0, 0, i % tiles, 0))
    widths = (GLA_KEY_WIDTH, GLA_KEY_WIDTH, GLA_VAL_WIDTH, GLA_VAL_WIDTH, 2 * D_MODEL)
    return pl.pallas_call(
        _proj_kernel,
        grid=(n // tm,),
        in_specs=[row(D_MODEL)] + [_const_spec(c.shape) for c in consts],
        out_specs=[q_spec, kv_spec] + [row(w) for w in widths] + [row(GLA_KEY_WIDTH)] * 2,
        out_shape=[jax.ShapeDtypeStruct((batch, pairs, seq, LANES), BF16),
                   jax.ShapeDtypeStruct((batch, 2 * pairs, NA_COL_BLOCKS, seq // NA_COL_BLOCKS, LANES), BF16)]
        + [jax.ShapeDtypeStruct((n, w), BF16) for w in widths]
        + [jax.ShapeDtypeStruct((n, GLA_KEY_WIDTH), F32)] * 2,
        compiler_params=pltpu.CompilerParams(dimension_semantics=("parallel",), vmem_limit_bytes=VMEM_LIMIT),
        name="proj",
    )(x2, *consts)


def _na_needed_blocks():
    need = []
    for g in range(GRID_W // NA_QGROUP):
        c = np.arange(g * NA_QGROUP, (g + 1) * NA_QGROUP)
        start = np.clip(c - NA_KW // 2, 0, GRID_W - NA_KW)
        need.append(sorted(set((start // NA_KW).tolist()) | set(((start + NA_KW - 1) // NA_KW).tolist())))
    return need


NA_NEED = _na_needed_blocks()
NA_HALF_ROWS = []
for _half in range(2):
    _groups = [g for g, blocks in enumerate(NA_NEED) if any(b // 2 == _half for b in blocks)]
    _lo = (min(_groups) * NA_QGROUP) // BF16_ROWS * BF16_ROWS
    _hi = -(-((max(_groups) + 1) * NA_QGROUP) // BF16_ROWS) * BF16_ROWS
    NA_HALF_ROWS.append((_lo, _hi))


def _na_bias_table(rpb):
    c = np.arange(GRID_W)
    col_start = np.clip(c - NA_KW // 2, 0, GRID_W - NA_KW)
    col_mask = (c[None, :] >= col_start[:, None]) & (c[None, :] < col_start[:, None] + NA_KW)
    dc_idx = np.clip(c[None, :] - c[:, None], -(NA_KW - 1), NA_KW - 1) + (NA_KW - 1)
    onehot = (dc_idx[None] == np.arange(2 * NA_KW - 1)[:, None, None]) & col_mask[None]
    cols = jnp.einsum('hrd,dqk->hrqk', rpb.astype(F32), jnp.asarray(onehot, F32),
                      precision=lax.Precision.HIGHEST)
    cols = jnp.where(col_mask[None, None], cols, -jnp.inf)
    bias = jnp.stack([cols[:, NA_KH - 1 - off:2 * NA_KH - 1 - off] for off in range(NA_KH)], axis=1)
    heads = rpb.shape[0]
    bias = bias.reshape(heads, NA_KH, NA_KH, GRID_W, NA_COL_BLOCKS, NA_KW)
    bias = bias.transpose(0, 1, 3, 4, 2, 5)
    return bias.reshape(heads, NA_KH, GRID_W, NA_KH * GRID_W)


def _na_kernel(q_ref, k_ref, v_ref, bias_ref, o_ref, *, rows):
    lane = lax.broadcasted_iota(jnp.int32, (GRID_W, LANES), 1)
    first = lane < NA_HEAD_DIM
    win = NA_KH * NA_KW
    groups = GRID_W // NA_QGROUP

    def window(r):
        start = jnp.clip(r - NA_KH // 2, 0, rows - NA_KH)
        return pl.ds(pl.multiple_of(start * NA_KW, NA_KW), win), r - start

    def tile(ref, keys, half):
        return jnp.concatenate([ref[0, 0, 2 * half, keys, :], ref[0, 0, 2 * half + 1, keys, :]], axis=0)

    def half_rows(x, half):
        lo, hi = NA_HALF_ROWS[half]
        return jnp.concatenate([x[lo:hi], x[GRID_W + lo:GRID_W + hi]], axis=0)

    def scores(r):
        keys, _ = window(r)
        q = q_ref[0, 0, pl.ds(pl.multiple_of(r * GRID_W, GRID_W), GRID_W), :] * (NA_HEAD_DIM ** -0.5)
        zero = jnp.zeros_like(q)
        qs = jnp.concatenate([jnp.where(first, q, zero), jnp.where(first, zero, q)], axis=0)
        return [_dot_nt(half_rows(qs, half), tile(k_ref, keys, half)) for half in range(2)]

    def attend(r, s):
        keys, off = window(r)
        p_rows = [[], []]
        inv = []
        for head in range(2):
            for g in range(groups):
                rows8 = slice(g * NA_QGROUP, (g + 1) * NA_QGROUP)
                pieces = {}
                for cb in NA_NEED[g]:
                    half = cb // 2
                    lo, hi = NA_HALF_ROWS[half]
                    at = head * (hi - lo) + g * NA_QGROUP - lo
                    pieces[cb] = (s[half][at:at + NA_QGROUP, (cb % 2) * LANES:(cb % 2 + 1) * LANES]
                                  + bias_ref[head, off, rows8, cb * LANES:(cb + 1) * LANES])
                m = functools.reduce(jnp.maximum, pieces.values())
                m = jnp.max(m, axis=-1, keepdims=True)
                pieces = {cb: jnp.exp(x - m) for cb, x in pieces.items()}
                total = functools.reduce(jnp.add, pieces.values())
                inv.append(1.0 / jnp.sum(total, axis=-1, keepdims=True))
                for half in range(2):
                    lo, hi = NA_HALF_ROWS[half]
                    if lo <= g * NA_QGROUP < hi:
                        zero = jnp.zeros((NA_QGROUP, LANES), F32)
                        p_rows[half].append(jnp.concatenate(
                            [pieces.get(2 * half, zero), pieces.get(2 * half + 1, zero)], axis=1))
        outs = []
        for half in range(2):
            p = jnp.concatenate(p_rows[half], axis=0).astype(BF16)
            outs.append(_dot(p, tile(v_ref, keys, half)))
        heads_o = []
        for head in range(2):
            rows_o = []
            for g in range(groups):
                acc = None
                for half in range(2):
                    lo, hi = NA_HALF_ROWS[half]
                    if lo <= g * NA_QGROUP < hi:
                        at = head * (hi - lo) + g * NA_QGROUP - lo
                        part = outs[half][at:at + NA_QGROUP]
                        acc = part if acc is None else acc + part
                rows_o.append(acc * inv[head * groups + g])
            heads_o.append(jnp.concatenate(rows_o, axis=0))
        o_ref[0, 0, pl.ds(pl.multiple_of(r * GRID_W, GRID_W), GRID_W), :] = jnp.where(
            first, heads_o[0], heads_o[1]).astype(BF16)

    def body(i, carry):
        rs = [i * NA_ROWS_PER_STEP + u for u in range(NA_ROWS_PER_STEP)]
        ss = [scores(r) for r in rs[:NA_LOOKAHEAD]]
        for u, r in enumerate(rs):
            if u + NA_LOOKAHEAD < len(rs):
                ss.append(scores(rs[u + NA_LOOKAHEAD]))
            attend(r, ss[u])
        return carry

    lax.fori_loop(0, rows // NA_ROWS_PER_STEP, body, 0)


def _na_call(na_q, na_kv, bias, *, batch, seq):
    rows = seq // GRID_W
    assert rows >= NA_KH and seq % GRID_W == 0 and rows % NA_ROWS_PER_STEP == 0
    pairs = NA_WIDTH // LANES
    q_spec = pl.BlockSpec((1, 1, seq, LANES), lambda p, b: (b, p, 0, 0))
    kv_spec = lambda part: pl.BlockSpec((1, 1, NA_COL_BLOCKS, seq // NA_COL_BLOCKS, LANES),
                                        lambda p, b: (b, part * pairs + p, 0, 0, 0))
    return pl.pallas_call(
        functools.partial(_na_kernel, rows=rows),
        grid=(pairs, batch),
        in_specs=[q_spec, kv_spec(0), kv_spec(1),
                  pl.BlockSpec((2, NA_KH, GRID_W, NA_KH * GRID_W), lambda p, b: (p, 0, 0, 0))],
        out_specs=q_spec,
        out_shape=jax.ShapeDtypeStruct((batch, pairs, seq, LANES), BF16),
        compiler_params=pltpu.CompilerParams(dimension_semantics=("parallel", "arbitrary"),
                                             vmem_limit_bytes=VMEM_LIMIT),
        name="na",
    )(na_q, na_kv, na_kv, bias)


def _gla_constants(reverse):
    c = GLA_CHUNK
    t = np.arange(c)
    flip = (lambda a: a[::-1, ::-1]) if reverse else (lambda a: a)
    sums = [flip((t[None, :] <= t[:, None]).astype(np.float32))]
    fine = [np.eye(c, dtype=np.float32)]
    coarse = []
    for lvl in range(GLA_LEVELS):
        m = 1 << lvl
        base = (t // (2 * m)) * (2 * m)
        upper = (t % (2 * m)) >= m
        s = t[None, :]
        mask = flip((upper[:, None] & (~upper)[None, :] & (base[:, None] == base[None, :])).astype(np.float32))
        if lvl < GLA_FINE_LEVELS:
            q_rows = upper[:, None] & (s >= (base + m)[:, None]) & (s <= t[:, None])
            k_rows = (~upper)[:, None] & (s > t[:, None]) & (s <= (base + m - 1)[:, None])
            sums.append(flip((q_rows | k_rows).astype(np.float32)))
            fine.append(mask)
        else:
            coarse.append(mask[mask.any(axis=1)])
    two_heads = lambda mk: np.concatenate([mk, mk], axis=0)
    return (jnp.asarray(np.concatenate(sums, axis=0), BF16),
            jnp.asarray(np.stack([two_heads(mk) for mk in fine]), F32),
            jnp.asarray(np.stack([two_heads(mk) for mk in coarse]), F32))


def _stack_heads(x):
    first = lax.broadcasted_iota(jnp.int32, x.shape, 1) < GLA_DK
    zero = jnp.zeros_like(x)
    return jnp.concatenate([jnp.where(first, x, zero), jnp.where(first, zero, x)], axis=0)


class _GlaUnit:
    def __init__(self, q_ref, k_ref, v_ref, la_ref, sums_ref, fine_ref, coarse_ref, state_ref, o_ref, reverse,
                 chunk, pair):
        self.refs = (fine_ref, coarse_ref, state_ref, o_ref)
        self.reverse, self.pair = reverse, pair
        c, w = GLA_CHUNK, LANES
        self.rows = rows = slice(chunk * c, (chunk + 1) * c)
        ks = slice(pair * LANES, (pair + 1) * LANES)
        la = la_ref[0, rows, ks]
        la_hi = la.astype(BF16)
        la_lo = (la - la_hi.astype(F32)).astype(BF16)
        sums = _dot(sums_ref[...], jnp.concatenate([la_hi, la_lo], axis=1))
        sums = sums[:, :w] + sums[:, w:]
        cum = sums[:c]
        last_row = 0 if reverse else c - 1
        cum_last = cum[last_row:last_row + 1]
        q = q_ref[0, rows, ks].astype(F32) * (GLA_DK ** -0.5)
        k = k_ref[0, rows, ks].astype(F32)
        self.v = v_ref[0, rows, pair * 2 * GLA_DV:(pair + 1) * 2 * GLA_DV]
        self.q_lvl = [q.astype(BF16)]
        self.k_lvl = [k.astype(BF16)]
        for lvl in range(GLA_FINE_LEVELS):
            e = jnp.exp(sums[(lvl + 1) * c:(lvl + 2) * c])
            self.q_lvl.append((q * e).astype(BF16))
            self.k_lvl.append((k * e).astype(BF16))
        for lvl in range(GLA_FINE_LEVELS, GLA_LEVELS):
            m = 1 << lvl
            split = lambda x: x.reshape(c // (2 * m), 2 * m, w)
            cum3, q3, k3 = split(cum), split(q), split(k)
            lo, hi = slice(0, m), slice(m, 2 * m)
            qh, kh = (lo, hi) if reverse else (hi, lo)
            ref = cum3[:, m:m + 1] if reverse else cum3[:, m - 1:m]
            self.q_lvl.append((q3[:, qh] * jnp.exp(cum3[:, qh] - ref)).reshape(c // 2, w).astype(BF16))
            k_scaled = k3[:, kh] * jnp.exp(ref - cum3[:, kh])
            k_parts = [k3[:, lo], k_scaled] if reverse else [k_scaled, k3[:, hi]]
            self.k_lvl.append(jnp.concatenate(k_parts, axis=1).reshape(c, w).astype(BF16))
        self.q_dec = (q * jnp.exp(cum)).astype(BF16)
        self.k_dec = (k * jnp.exp(cum_last - cum)).astype(BF16)
        self.decay = jnp.exp(cum_last)

    def scores(self):
        fine_ref, coarse_ref, _, _ = self.refs
        c = GLA_CHUNK
        level = lambda lvl: _dot_nt(_stack_heads(self.q_lvl[lvl]), self.k_lvl[lvl])
        scores = fine_ref[0] * level(0)
        for lvl in range(GLA_FINE_LEVELS):
            scores += fine_ref[lvl + 1] * level(lvl + 1)
        for lvl in range(GLA_FINE_LEVELS, GLA_LEVELS):
            m = 1 << lvl
            x = (coarse_ref[lvl - GLA_FINE_LEVELS] * level(lvl + 1)).reshape(2 * c // (2 * m), m, c)
            zero = jnp.zeros_like(x)
            scores += jnp.concatenate([x, zero] if self.reverse else [zero, x], axis=1).reshape(2 * c, c)
        self.s = scores.astype(BF16)

    def output(self):
        _, _, state_ref, o_ref = self.refs
        c, pair = GLA_CHUNK, self.pair
        state = state_ref[pair]
        v_t = self.v.T
        lhs = jnp.concatenate([self.s, _stack_heads(self.q_dec)], axis=1)
        rhs = jnp.concatenate([v_t, state.astype(BF16)], axis=1)
        o = _dot_nt(lhs, rhs)
        o_ref[0, self.rows, pair * 2 * GLA_DV:pair * 2 * GLA_DV + GLA_DV] = o[:c, :GLA_DV].astype(o_ref.dtype)
        o_ref[0, self.rows, pair * 2 * GLA_DV + GLA_DV:(pair + 1) * 2 * GLA_DV] = o[c:, GLA_DV:].astype(o_ref.dtype)
        upd = _dot(v_t, self.k_dec)
        row = lax.broadcasted_iota(jnp.int32, upd.shape, 0)
        col = lax.broadcasted_iota(jnp.int32, upd.shape, 1)
        own = (row < GLA_DV) == (col < GLA_DK)
        state_ref[pair] = state * self.decay + jnp.where(own, upd, 0.0)


def _gla_kernel(qf_ref, kf_ref, vf_ref, laf_ref, qb_ref, kb_ref, vb_ref, lab_ref,
                sums_f_ref, fine_f_ref, coarse_f_ref, sums_b_ref, fine_b_ref, coarse_b_ref,
                of_ref, ob_ref, state_f_ref, state_b_ref):
    @pl.when(pl.program_id(1) == 0)
    def _():
        state_f_ref[...] = jnp.zeros_like(state_f_ref)
        state_b_ref[...] = jnp.zeros_like(state_b_ref)

    fwd = (qf_ref, kf_ref, vf_ref, laf_ref, sums_f_ref, fine_f_ref, coarse_f_ref, state_f_ref, of_ref, False)
    bwd = (qb_ref, kb_ref, vb_ref, lab_ref, sums_b_ref, fine_b_ref, coarse_b_ref, state_b_ref, ob_ref, True)
    order = []
    for j in range(GLA_CHUNKS_PER_STEP):
        for pair in range(GLA_HEADS // 2):
            order.append(fwd + (j, pair))
            order.append(bwd + (GLA_CHUNKS_PER_STEP - 1 - j, pair))
    units = [_GlaUnit(*order[0])]
    for i in range(len(order)):
        if i + 1 < len(order):
            units.append(_GlaUnit(*order[i + 1]))
        units[i].scores()
        if i > 0:
            units[i - 1].output()
    units[-1].output()


def _gla_call(q, k, v, la_f, la_b, *, batch, seq):
    c = GLA_CHUNK * GLA_CHUNKS_PER_STEP
    assert seq % c == 0
    n = seq // c
    fwd = lambda b, i: i
    bwd = lambda b, i: n - 1 - i
    consts = _gla_constants(False) + _gla_constants(True)
    spec = lambda chunk, w: pl.BlockSpec((1, c, w), lambda b, i: (b, chunk(b, i), 0))
    specs = lambda chunk: [spec(chunk, GLA_KEY_WIDTH)] * 2 + [spec(chunk, GLA_VAL_WIDTH), spec(chunk, GLA_KEY_WIDTH)]
    state = pltpu.VMEM((GLA_HEADS // 2, 2 * GLA_DV, LANES), F32)
    b3 = lambda a: a.reshape(batch, seq, a.shape[-1])
    q, k, v, la_f, la_b = b3(q), b3(k), b3(v), b3(la_f), b3(la_b)
    return pl.pallas_call(
        _gla_kernel,
        grid=(batch, n),
        in_specs=specs(fwd) + specs(bwd) + [_const_spec(a.shape) for a in consts],
        out_specs=[spec(fwd, GLA_VAL_WIDTH), spec(bwd, GLA_VAL_WIDTH)],
        out_shape=[jax.ShapeDtypeStruct((batch, seq, GLA_VAL_WIDTH), BF16)] * 2,
        scratch_shapes=[state, state],
        compiler_params=pltpu.CompilerParams(dimension_semantics=("parallel", "arbitrary"),
                                             vmem_limit_bytes=VMEM_LIMIT),
        name="gla",
    )(q, k, v, la_f, q, k, v, la_b, *consts)


def _tail_kernel(x_ref, nao_ref, of_ref, ob_ref, g_ref, gate_ref, gng_ref, wna_ref, wgla_ref, wout_ref,
                 gm_ref, wup_ref, wdown_ref, gf_ref, y_ref):
    na_o = jnp.concatenate([nao_ref[0, p] for p in range(NA_WIDTH // LANES)], axis=1)
    na_out = _dot(na_o, wna_ref[...])
    o = of_ref[...].astype(F32) + ob_ref[...].astype(F32)
    heads = []
    for h in range(GLA_HEADS):
        heads.append(_rms(o[:, h * GLA_DV:(h + 1) * GLA_DV]) * gng_ref[...])
    g = g_ref[...].astype(F32)
    gla_o = jnp.concatenate(heads, axis=1) * (g * _sigmoid(g))
    gla_out = _dot(gla_o.astype(BF16), wgla_ref[...])
    gate = gate_ref[...].astype(F32)
    merged = gate[:, :D_MODEL] * na_out + gate[:, D_MODEL:] * gla_out
    h = x_ref[...] + _dot(merged.astype(BF16), wout_ref[...])
    u = (_rms(h) * gm_ref[...]).astype(BF16)
    width = D_FF // MLP_SLICES
    for j in range(MLP_SLICES):
        cols = slice(j * width, (j + 1) * width)
        hdn = jnp.square(jnp.maximum(_dot(u, wup_ref[:, cols]), 0.0)).astype(BF16)
        h = h + _dot(hdn, wdown_ref[cols, :])
    y_ref[...] = _rms(h) * gf_ref[...]


def _tail_call(x2, na_o, o_f, o_b, gla_g, gate, gng, wna, wgla, wout, gm, wup, wdown, gf, *, tm):
    n = x2.shape[0]
    tiles = na_o.shape[2] // tm
    row = lambda w: pl.BlockSpec((tm, w), lambda i: (i, 0))
    nao_spec = pl.BlockSpec((1, NA_WIDTH // LANES, tm, LANES), lambda i: (i // tiles, 0, i % tiles, 0))
    consts = (gng, wna, wgla, wout, gm, wup, wdown, gf)
    return pl.pallas_call(
        _tail_kernel,
        grid=(n // tm,),
        in_specs=[row(D_MODEL), nao_spec, row(GLA_VAL_WIDTH), row(GLA_VAL_WIDTH),
                  row(GLA_VAL_WIDTH), row(2 * D_MODEL)] + [_const_spec(c.shape) for c in consts],
        out_specs=row(D_MODEL),
        out_shape=jax.ShapeDtypeStruct((n, D_MODEL), F32),
        compiler_params=pltpu.CompilerParams(dimension_semantics=("parallel",), vmem_limit_bytes=VMEM_LIMIT),
        name="tail",
    )(x2, na_o, o_f, o_b, gla_g, gate, *consts)


def _prepare(norm_mix_g, w_in, b_in, na_rpb, gk_fwd_w, gk_fwd_b, gk_bwd_w, gk_bwd_b, gla_norm_g,
             w_br_na, w_br_gla, w_out, norm_mlp_g, w_up, w_down, norm_final_g):
    edges = np.concatenate([[0], np.cumsum(IN_WIDTHS)])
    wcols = lambda a, b: w_in[:, edges[a]:edges[b]]
    bcols = lambda a, b: b_in[None, edges[a]:edges[b]]
    r = GLA_GATE_RANK
    pad = LR_PAD - 2 * r
    wlr = jnp.pad(wcols(7, 9), ((0, 0), (0, pad)))
    blr = jnp.pad(bcols(7, 9), ((0, 0), (0, pad)))
    gkw = jnp.zeros((LR_PAD, 2 * GLA_KEY_WIDTH), F32)
    gkw = gkw.at[:r, :GLA_KEY_WIDTH].set(gk_fwd_w).at[r:2 * r, GLA_KEY_WIDTH:].set(gk_bwd_w)
    gkb = jnp.concatenate([gk_fwd_b, gk_bwd_b])[None]
    return dict(
        proj=(norm_mix_g[None], wcols(0, 1).astype(BF16), bcols(0, 1), wcols(1, 3).astype(BF16), bcols(1, 3),
              wcols(3, 7).astype(BF16), bcols(3, 7),
              wcols(9, 11).astype(BF16), bcols(9, 11), wlr.astype(BF16), blr, gkw.astype(BF16), gkb),
        na_bias=_na_bias_table(na_rpb),
        tail=(gla_norm_g[None], w_br_na.astype(BF16), w_br_gla.astype(BF16), w_out.astype(BF16),
              norm_mlp_g[None], w_up.astype(BF16), w_down.astype(BF16), norm_final_g[None]),
    )


def _trunk(x, prep, *, tm):
    batch, seq, _ = x.shape
    x2 = x.reshape(batch * seq, D_MODEL)
    na_q, na_kv, gla_q, gla_k, gla_v, gla_g, gate, la_f, la_b = _proj_call(
        x2, *prep["proj"], tm=tm, batch=batch, seq=seq)
    na_o = _na_call(na_q, na_kv, prep["na_bias"], batch=batch, seq=seq)
    o_f, o_b = _gla_call(gla_q, gla_k, gla_v, la_f, la_b, batch=batch, seq=seq)
    y = _tail_call(x2, na_o, o_f.reshape(batch * seq, GLA_VAL_WIDTH), o_b.reshape(batch * seq, GLA_VAL_WIDTH),
                   gla_g, gate, *prep["tail"], tm=tm)
    return y.reshape(batch, seq, D_MODEL)


def kernel(x_prompt, x_sample, norm_mix_g, w_in, b_in, na_rpb, gk_fwd_w, gk_fwd_b, gk_bwd_w, gk_bwd_b, gla_norm_g, w_br_na, w_br_gla, w_out, norm_mlp_g, w_up, w_down, norm_final_g):
    assert norm_mix_g.shape[0] == 1, "single-layer trunk"
    prep = _prepare(norm_mix_g[0], w_in[0], b_in[0], na_rpb[0], gk_fwd_w[0], gk_fwd_b[0], gk_bwd_w[0],
                    gk_bwd_b[0], gla_norm_g[0], w_br_na[0], w_br_gla[0], w_out[0], norm_mlp_g[0], w_up[0],
                    w_down[0], norm_final_g)
    return _trunk(x_prompt, prep, tm=512), _trunk(x_sample, prep, tm=512)
```

```python
import functools

import numpy as np
import jax
import jax.numpy as jnp
from jax import lax
from jax.experimental import pallas as pl
from jax.experimental.pallas import tpu as pltpu

D_MODEL = 1024
GRID_W = 64
NA_HEADS = 8
NA_HEAD_DIM = 64
NA_WIDTH = NA_HEADS * NA_HEAD_DIM
NA_KH = 8
NA_KW = 16
GLA_HEADS = 4
GLA_KEY_WIDTH = 256
GLA_VAL_WIDTH = 512
GLA_DK = 64
GLA_DV = 128
GLA_GATE_RANK = 16
GLA_GATE_NORMALIZER = 16.0
D_FF = 4 * D_MODEL
RMS_EPS = 1e-6
IN_WIDTHS = (NA_WIDTH, NA_WIDTH, NA_WIDTH, GLA_KEY_WIDTH, GLA_KEY_WIDTH, GLA_VAL_WIDTH, GLA_VAL_WIDTH,
             GLA_GATE_RANK, GLA_GATE_RANK, D_MODEL, D_MODEL)

LANES = 128
BF16_ROWS = 16
GLA_CHUNK = 128
GLA_LEVELS = 7
GLA_CHUNKS_PER_STEP = 8
GLA_FINE_LEVELS = 3
LR_PAD = LANES
NA_COL_BLOCKS = GRID_W // NA_KW
NA_QGROUP = 8
NA_ROWS_PER_STEP = 32
NA_LOOKAHEAD = 6
MLP_SLICES = 4
ROW_TILE = 512
VMEM_LIMIT = 56 * 1024 * 1024

BF16 = jnp.bfloat16
F32 = jnp.float32


def _dot(a, b):
    return jnp.dot(a, b, preferred_element_type=F32)


def _dot_nt(a, b):
    return lax.dot_general(a, b, (((1,), (1,)), ((), ())), preferred_element_type=F32)


def _sigmoid(z):
    return 0.5 * jnp.tanh(0.5 * z) + 0.5


def _rms(x):
    return x * lax.rsqrt(jnp.mean(x * x, axis=-1, keepdims=True) + RMS_EPS)


def _proj_kernel(x_ref, g_ref, wq_ref, bq_ref, wkv_ref, bkv_ref, wgla_ref, bgla_ref, wgate_ref, bgate_ref,
                 wlr_ref, blr_ref, gkw_ref, gkb_ref,
                 q_ref, kv_ref, gq_ref, gk_ref, gv_ref, gg_ref, gate_ref, laf_ref, lab_ref):
    xb = (_rms(x_ref[...]) * g_ref[...]).astype(BF16)
    q = (_dot(xb, wq_ref[...]) + bq_ref[...]).astype(BF16)
    for p in range(NA_WIDTH // LANES):
        q_ref[0, p] = q[:, p * LANES:(p + 1) * LANES]
    kv = (_dot(xb, wkv_ref[...]) + bkv_ref[...]).astype(BF16)
    kv = kv.reshape(kv.shape[0] // GRID_W, NA_COL_BLOCKS, NA_KW, kv.shape[1])
    for cb in range(NA_COL_BLOCKS):
        block = kv[:, cb].reshape(-1, kv.shape[3])
        for p in range(2 * NA_WIDTH // LANES):
            kv_ref[0, p, cb] = block[:, p * LANES:(p + 1) * LANES]
    gla = (_dot(xb, wgla_ref[...]) + bgla_ref[...]).astype(BF16)
    w = GLA_KEY_WIDTH
    gq_ref[...] = gla[:, :w]
    gk_ref[...] = gla[:, w:2 * w]
    gv_ref[...] = gla[:, 2 * w:2 * w + GLA_VAL_WIDTH]
    gg_ref[...] = gla[:, 2 * w + GLA_VAL_WIDTH:]
    gate_ref[...] = _sigmoid(_dot(xb, wgate_ref[...]) + bgate_ref[...]).astype(BF16)
    lr = (_dot(xb, wlr_ref[...]) + blr_ref[...]).astype(BF16)
    z = _dot(lr, gkw_ref[...]) + gkb_ref[...]
    log_a = (jnp.minimum(z, 0.0) - jnp.log(1.0 + jnp.exp(-jnp.abs(z)))) * (1.0 / GLA_GATE_NORMALIZER)
    laf_ref[...] = log_a[:, :w]
    lab_ref[...] = log_a[:, w:]


def _const_spec(shape):
    nd = len(shape)
    return pl.BlockSpec(shape, lambda *_: (0,) * nd, pipeline_mode=pl.Buffered(1))


def _proj_call(x2, g, wq, bq, wkv, bkv, wgla, bgla, wgate, bgate, wlr, blr, gkw, gkb, *, tm, batch, seq):
    n = x2.shape[0]
    assert seq % tm == 0 and tm % GRID_W == 0
    tiles = seq // tm
    pairs = NA_WIDTH // LANES
    row = lambda w: pl.BlockSpec((tm, w), lambda i: (i, 0))
    consts = (g, wq, bq, wkv, bkv, wgla, bgla, wgate, bgate, wlr, blr, gkw, gkb)
    q_spec = pl.BlockSpec((1, pairs, tm, LANES), lambda i: (i // tiles, 0, i % tiles, 0))
    kv_spec = pl.BlockSpec((1, 2 * pairs, NA_COL_BLOCKS, tm // NA_COL_BLOCKS, LANES),
                           lambda i: (i // tiles, 0, 0, i % tiles, 0))
    widths = (GLA_KEY_WIDTH, GLA_KEY_WIDTH, GLA_VAL_WIDTH, GLA_VAL_WIDTH, 2 * D_MODEL)
    return pl.pallas_call(
        _proj_kernel,
        grid=(n // tm,),
        in_specs=[row(D_MODEL)] + [_const_spec(c.shape) for c in consts],
        out_specs=[q_spec, kv_spec] + [row(w) for w in widths] + [row(GLA_KEY_WIDTH)] * 2,
        out_shape=[jax.ShapeDtypeStruct((batch, pairs, seq, LANES), BF16),
                   jax.ShapeDtypeStruct((batch, 2 * pairs, NA_COL_BLOCKS, seq // NA_COL_BLOCKS, LANES), BF16)]
        + [jax.ShapeDtypeStruct((n, w), BF16) for w in widths]
        + [jax.ShapeDtypeStruct((n, GLA_KEY_WIDTH), F32)] * 2,
        compiler_params=pltpu.CompilerParams(dimension_semantics=("parallel",), vmem_limit_bytes=VMEM_LIMIT),
        name="proj",
    )(x2, *consts)


def _na_needed_blocks():
    need = []
    for g in range(GRID_W // NA_QGROUP):
        c = np.arange(g * NA_QGROUP, (g + 1) * NA_QGROUP)
        start = np.clip(c - NA_KW // 2, 0, GRID_W - NA_KW)
        need.append(sorted(set((start // NA_KW).tolist()) | set(((start + NA_KW - 1) // NA_KW).tolist())))
    return need


NA_NEED = _na_needed_blocks()
NA_HALF_ROWS = []
for _half in range(2):
    _groups = [g for g, blocks in enumerate(NA_NEED) if any(b // 2 == _half for b in blocks)]
    _lo = (min(_groups) * NA_QGROUP) // BF16_ROWS * BF16_ROWS
    _hi = -(-((max(_groups) + 1) * NA_QGROUP) // BF16_ROWS) * BF16_ROWS
    NA_HALF_ROWS.append((_lo, _hi))


def _na_bias_table(rpb):
    c = np.arange(GRID_W)
    col_start = np.clip(c - NA_KW // 2, 0, GRID_W - NA_KW)
    col_mask = (c[None, :] >= col_start[:, None]) & (c[None, :] < col_start[:, None] + NA_KW)
    dc_idx = np.clip(c[None, :] - c[:, None], -(NA_KW - 1), NA_KW - 1) + (NA_KW - 1)
    onehot = (dc_idx[None] == np.arange(2 * NA_KW - 1)[:, None, None]) & col_mask[None]
    cols = jnp.einsum('hrd,dqk->hrqk', rpb.astype(F32), jnp.asarray(onehot, F32),
                      precision=lax.Precision.HIGHEST)
    cols = jnp.where(col_mask[None, None], cols, -jnp.inf)
    bias = jnp.stack([cols[:, NA_KH - 1 - off:2 * NA_KH - 1 - off] for off in range(NA_KH)], axis=1)
    heads = rpb.shape[0]
    bias = bias.reshape(heads, NA_KH, NA_KH, GRID_W, NA_COL_BLOCKS, NA_KW)
    bias = bias.transpose(0, 1, 3, 4, 2, 5)
    return bias.reshape(heads, NA_KH, GRID_W, NA_KH * GRID_W)


def _na_kernel(q_ref, k_ref, v_ref, bias_ref, o_ref, *, rows):
    lane = lax.broadcasted_iota(jnp.int32, (GRID_W, LANES), 1)
    first = lane < NA_HEAD_DIM
    win = NA_KH * NA_KW
    groups = GRID_W // NA_QGROUP

    def window(r):
        start = jnp.clip(r - NA_KH // 2, 0, rows - NA_KH)
        return pl.ds(pl.multiple_of(start * NA_KW, NA_KW), win), r - start

    def tile(ref, keys, half):
        return jnp.concatenate([ref[0, 0, 2 * half, keys, :], ref[0, 0, 2 * half + 1, keys, :]], axis=0)

    def half_rows(x, half):
        lo, hi = NA_HALF_ROWS[half]
        return jnp.concatenate([x[lo:hi], x[GRID_W + lo:GRID_W + hi]], axis=0)

    def scores(r):
        keys, _ = window(r)
        q = q_ref[0, 0, pl.ds(pl.multiple_of(r * GRID_W, GRID_W), GRID_W), :] * (NA_HEAD_DIM ** -0.5)
        zero = jnp.zeros_like(q)
        qs = jnp.concatenate([jnp.where(first, q, zero), jnp.where(first, zero, q)], axis=0)
        return [_dot_nt(half_rows(qs, half), tile(k_ref, keys, half)) for half in range(2)]

    def attend(r, s):
        keys, off = window(r)
        p_rows = [[], []]
        inv = []
        for head in range(2):
            for g in range(groups):
                rows8 = slice(g * NA_QGROUP, (g + 1) * NA_QGROUP)
                pieces = {}
                for cb in NA_NEED[g]:
                    half = cb // 2
                    lo, hi = NA_HALF_ROWS[half]
                    at = head * (hi - lo) + g * NA_QGROUP - lo
                    pieces[cb] = (s[half][at:at + NA_QGROUP, (cb % 2) * LANES:(cb % 2 + 1) * LANES]
                                  + bias_ref[head, off, rows8, cb * LANES:(cb + 1) * LANES])
                m = functools.reduce(jnp.maximum, pieces.values())
                m = jnp.max(m, axis=-1, keepdims=True)
                pieces = {cb: jnp.exp(x - m) for cb, x in pieces.items()}
                total = functools.reduce(jnp.add, pieces.values())
                inv.append(1.0 / jnp.sum(total, axis=-1, keepdims=True))
                for half in range(2):
                    lo, hi = NA_HALF_ROWS[half]
                    if lo <= g * NA_QGROUP < hi:
                        zero = jnp.zeros((NA_QGROUP, LANES), F32)
                        p_rows[half].append(jnp.concatenate(
                            [pieces.get(2 * half, zero), pieces.get(2 * half + 1, zero)], axis=1))
        outs = []
        for half in range(2):
            p = jnp.concatenate(p_rows[half], axis=0).astype(BF16)
            outs.append(_dot(p, tile(v_ref, keys, half)))
        heads_o = []
        for head in range(2):
            rows_o = []
            for g in range(groups):
                acc = None
                for half in range(2):
                    lo, hi = NA_HALF_ROWS[half]
                    if lo <= g * NA_QGROUP < hi:
                        at = head * (hi - lo) + g * NA_QGROUP - lo
                        part = outs[half][at:at + NA_QGROUP]
                        acc = part if acc is None else acc + part
                rows_o.append(acc * inv[head * groups + g])
            heads_o.append(jnp.concatenate(rows_o, axis=0))
        o_ref[0, 0, pl.ds(pl.multiple_of(r * GRID_W, GRID_W), GRID_W), :] = jnp.where(
            first, heads_o[0], heads_o[1]).astype(BF16)

    def body(i, carry):
        rs = [i * NA_ROWS_PER_STEP + u for u in range(NA_ROWS_PER_STEP)]
        ss = [scores(r) for r in rs[:NA_LOOKAHEAD]]
        for u, r in enumerate(rs):
            if u + NA_LOOKAHEAD < len(rs):
                ss.append(scores(rs[u + NA_LOOKAHEAD]))
            attend(r, ss[u])
        return carry

    lax.fori_loop(0, rows // NA_ROWS_PER_STEP, body, 0)


def _na_call(na_q, na_kv, bias, *, batch, seq):
    rows = seq // GRID_W
    assert rows >= NA_KH and seq % GRID_W == 0 and rows % NA_ROWS_PER_STEP == 0
    pairs = NA_WIDTH // LANES
    q_spec = pl.BlockSpec((1, 1, seq, LANES), lambda p, b: (b, p, 0, 0))
    kv_spec = lambda part: pl.BlockSpec((1, 1, NA_COL_BLOCKS, seq // NA_COL_BLOCKS, LANES),
                                        lambda p, b: (b, part * pairs + p, 0, 0, 0))
    return pl.pallas_call(
        functools.partial(_na_kernel, rows=rows),
        grid=(pairs, batch),
        in_specs=[q_spec, kv_spec(0), kv_spec(1),
                  pl.BlockSpec((2, NA_KH, GRID_W, NA_KH * GRID_W), lambda p, b: (p, 0, 0, 0))],
        out_specs=q_spec,
        out_shape=jax.ShapeDtypeStruct((batch, pairs, seq, LANES), BF16),
        compiler_params=pltpu.CompilerParams(dimension_semantics=("parallel", "arbitrary"),
                                             vmem_limit_bytes=VMEM_LIMIT),
        name="na",
    )(na_q, na_kv, na_kv, bias)


def _gla_constants(reverse):
    c = GLA_CHUNK
    t = np.arange(c)
    flip = (lambda a: a[::-1, ::-1]) if reverse else (lambda a: a)
    sums = [flip((t[None, :] <= t[:, None]).astype(np.float32))]
    fine = [np.eye(c, dtype=np.float32)]
    coarse = []
    for lvl in range(GLA_LEVELS):
        m = 1 << lvl
        base = (t // (2 * m)) * (2 * m)
        upper = (t % (2 * m)) >= m
        s = t[None, :]
        mask = flip((upper[:, None] & (~upper)[None, :] & (base[:, None] == base[None, :])).astype(np.float32))
        if lvl < GLA_FINE_LEVELS:
            q_rows = upper[:, None] & (s >= (base + m)[:, None]) & (s <= t[:, None])
            k_rows = (~upper)[:, None] & (s > t[:, None]) & (s <= (base + m - 1)[:, None])
            sums.append(flip((q_rows | k_rows).astype(np.float32)))
            fine.append(mask)
        else:
            coarse.append(mask[mask.any(axis=1)])
    two_heads = lambda mk: np.concatenate([mk, mk], axis=0)
    return (jnp.asarray(np.concatenate(sums, axis=0), BF16),
            jnp.asarray(np.stack([two_heads(mk) for mk in fine]), F32),
            jnp.asarray(np.stack([two_heads(mk) for mk in coarse]), F32))


def _stack_heads(x):
    first = lax.broadcasted_iota(jnp.int32, x.shape, 1) < GLA_DK
    zero = jnp.zeros_like(x)
    return jnp.concatenate([jnp.where(first, x, zero), jnp.where(first, zero, x)], axis=0)


class _GlaUnit:
    def __init__(self, q_ref, k_ref, v_ref, la_ref, sums_ref, fine_ref, coarse_ref, state_ref, o_ref, reverse,
                 chunk, pair):
        self.refs = (fine_ref, coarse_ref, state_ref, o_ref)
        self.reverse, self.pair = reverse, pair
        c, w = GLA_CHUNK, LANES
        self.rows = rows = slice(chunk * c, (chunk + 1) * c)
        ks = slice(pair * LANES, (pair + 1) * LANES)
        la = la_ref[0, rows, ks]
        la_hi = la.astype(BF16)
        la_lo = (la - la_hi.astype(F32)).astype(BF16)
        sums = _dot(sums_ref[...], jnp.concatenate([la_hi, la_lo], axis=1))
        sums = sums[:, :w] + sums[:, w:]
        cum = sums[:c]
        last_row = 0 if reverse else c - 1
        cum_last = cum[last_row:last_row + 1]
        q = q_ref[0, rows, ks].astype(F32) * (GLA_DK ** -0.5)
        k = k_ref[0, rows, ks].astype(F32)
        self.v = v_ref[0, rows, pair * 2 * GLA_DV:(pair + 1) * 2 * GLA_DV]
        self.q_lvl = [q.astype(BF16)]
        self.k_lvl = [k.astype(BF16)]
        for lvl in range(GLA_FINE_LEVELS):
            e = jnp.exp(sums[(lvl + 1) * c:(lvl + 2) * c])
            self.q_lvl.append((q * e).astype(BF16))
            self.k_lvl.append((k * e).astype(BF16))
        for lvl in range(GLA_FINE_LEVELS, GLA_LEVELS):
            m = 1 << lvl
            split = lambda x: x.reshape(c // (2 * m), 2 * m, w)
            cum3, q3, k3 = split(cum), split(q), split(k)
            lo, hi = slice(0, m), slice(m, 2 * m)
            qh, kh = (lo, hi) if reverse else (hi, lo)
            ref = cum3[:, m:m + 1] if reverse else cum3[:, m - 1:m]
            self.q_lvl.append((q3[:, qh] * jnp.exp(cum3[:, qh] - ref)).reshape(c // 2, w).astype(BF16))
            k_scaled = k3[:, kh] * jnp.exp(ref - cum3[:, kh])
            k_parts = [k3[:, lo], k_scaled] if reverse else [k_scaled, k3[:, hi]]
            self.k_lvl.append(jnp.concatenate(k_parts, axis=1).reshape(c, w).astype(BF16))
        self.q_dec = (q * jnp.exp(cum)).astype(BF16)
        self.k_dec = (k * jnp.exp(cum_last - cum)).astype(BF16)
        self.decay = jnp.exp(cum_last)

    def scores(self):
        fine_ref, coarse_ref, _, _ = self.refs
        c = GLA_CHUNK
        level = lambda lvl: _dot_nt(_stack_heads(self.q_lvl[lvl]), self.k_lvl[lvl])
        scores = fine_ref[0] * level(0)
        for lvl in range(GLA_FINE_LEVELS):
            scores += fine_ref[lvl + 1] * level(lvl + 1)
        for lvl in range(GLA_FINE_LEVELS, GLA_LEVELS):
            m = 1 << lvl
            x = (coarse_ref[lvl - GLA_FINE_LEVELS] * level(lvl + 1)).reshape(2 * c // (2 * m), m, c)
            zero = jnp.zeros_like(x)
            scores += jnp.concatenate([x, zero] if self.reverse else [zero, x], axis=1).reshape(2 * c, c)
        self.s = scores.astype(BF16)

    def output(self):
        _, _, state_ref, o_ref = self.refs
        c, pair = GLA_CHUNK, self.pair
        state = state_ref[pair]
        v_t = self.v.T
        lhs = jnp.concatenate([self.s, _stack_heads(self.q_dec)], axis=1)
        rhs = jnp.concatenate([v_t, state.astype(BF16)], axis=1)
        o = _dot_nt(lhs, rhs)
        o_ref[0, self.rows, pair * 2 * GLA_DV:pair * 2 * GLA_DV + GLA_DV] = o[:c, :GLA_DV].astype(o_ref.dtype)
        o_ref[0, self.rows, pair * 2 * GLA_DV + GLA_DV:(pair + 1) * 2 * GLA_DV] = o[c:, GLA_DV:].astype(o_ref.dtype)
        upd = _dot(v_t, self.k_dec)
        row = lax.broadcasted_iota(jnp.int32, upd.shape, 0)
        col = lax.broadcasted_iota(jnp.int32, upd.shape, 1)
        own = (row < GLA_DV) == (col < GLA_DK)
        state_ref[pair] = state * self.decay + jnp.where(own, upd, 0.0)


def _gla_kernel(qf_ref, kf_ref, vf_ref, laf_ref, qb_ref, kb_ref, vb_ref, lab_ref,
                sums_f_ref, fine_f_ref, coarse_f_ref, sums_b_ref, fine_b_ref, coarse_b_ref,
                of_ref, ob_ref, state_f_ref, state_b_ref):
    @pl.when(pl.program_id(1) == 0)
    def _():
        state_f_ref[...] = jnp.zeros_like(state_f_ref)
        state_b_ref[...] = jnp.zeros_like(state_b_ref)

    fwd = (qf_ref, kf_ref, vf_ref, laf_ref, sums_f_ref, fine_f_ref, coarse_f_ref, state_f_ref, of_ref, False)
    bwd = (qb_ref, kb_ref, vb_ref, lab_ref, sums_b_ref, fine_b_ref, coarse_b_ref, state_b_ref, ob_ref, True)
    order = []
    for j in range(GLA_CHUNKS_PER_STEP):
        for pair in range(GLA_HEADS // 2):
            order.append(fwd + (j, pair))
            order.append(bwd + (GLA_CHUNKS_PER_STEP - 1 - j, pair))
    units = [_GlaUnit(*order[0])]
    for i in range(len(order)):
        if i + 1 < len(order):
            units.append(_GlaUnit(*order[i + 1]))
        units[i].scores()
        if i > 0:
            units[i - 1].output()
    units[-1].output()


def _gla_call(q, k, v, la_f, la_b, *, batch, seq):
    c = GLA_CHUNK * GLA_CHUNKS_PER_STEP
    assert seq % c == 0
    n = seq // c
    fwd = lambda b, i: i
    bwd = lambda b, i: n - 1 - i
    consts = _gla_constants(False) + _gla_constants(True)
    spec = lambda chunk, w: pl.BlockSpec((1, c, w), lambda b, i: (b, chunk(b, i), 0))
    specs = lambda chunk: [spec(chunk, GLA_KEY_WIDTH)] * 2 + [spec(chunk, GLA_VAL_WIDTH), spec(chunk, GLA_KEY_WIDTH)]
    state = pltpu.VMEM((GLA_HEADS // 2, 2 * GLA_DV, LANES), F32)
    b3 = lambda a: a.reshape(batch, seq, a.shape[-1])
    q, k, v, la_f, la_b = b3(q), b3(k), b3(v), b3(la_f), b3(la_b)
    return pl.pallas_call(
        _gla_kernel,
        grid=(batch, n),
        in_specs=specs(fwd) + specs(bwd) + [_const_spec(a.shape) for a in consts],
        out_specs=[spec(fwd, GLA_VAL_WIDTH), spec(bwd, GLA_VAL_WIDTH)],
        out_shape=[jax.ShapeDtypeStruct((batch, seq, GLA_VAL_WIDTH), BF16)] * 2,
        scratch_shapes=[state, state],
        compiler_params=pltpu.CompilerParams(dimension_semantics=("parallel", "arbitrary"),
                                             vmem_limit_bytes=VMEM_LIMIT),
        name="gla",
    )(q, k, v, la_f, q, k, v, la_b, *consts)


def _tail_kernel(x_ref, nao_ref, of_ref, ob_ref, g_ref, gate_ref, gng_ref, wna_ref, wgla_ref, wout_ref,
                 gm_ref, wup_ref, wdown_ref, gf_ref, y_ref):
    na_o = jnp.concatenate([nao_ref[0, p] for p in range(NA_WIDTH // LANES)], axis=1)
    na_out = _dot(na_o, wna_ref[...])
    o = of_ref[...].astype(F32) + ob_ref[...].astype(F32)
    heads = []
    for h in range(GLA_HEADS):
        heads.append(_rms(o[:, h * GLA_DV:(h + 1) * GLA_DV]) * gng_ref[...])
    g = g_ref[...].astype(F32)
    gla_o = jnp.concatenate(heads, axis=1) * (g * _sigmoid(g))
    gla_out = _dot(gla_o.astype(BF16), wgla_ref[...])
    gate = gate_ref[...].astype(F32)
    merged = gate[:, :D_MODEL] * na_out + gate[:, D_MODEL:] * gla_out
    h = x_ref[...] + _dot(merged.astype(BF16), wout_ref[...])
    u = (_rms(h) * gm_ref[...]).astype(BF16)
    width = D_FF // MLP_SLICES
    for j in range(MLP_SLICES):
        cols = slice(j * width, (j + 1) * width)
        hdn = jnp.square(jnp.maximum(_dot(u, wup_ref[:, cols]), 0.0)).astype(BF16)
        h = h + _dot(hdn, wdown_ref[cols, :])
    y_ref[...] = _rms(h) * gf_ref[...]


def _tail_call(x2, na_o, o_f, o_b, gla_g, gate, gng, wna, wgla, wout, gm, wup, wdown, gf, *, tm):
    n = x2.shape[0]
    tiles = na_o.shape[2] // tm
    row = lambda w: pl.BlockSpec((tm, w), lambda i: (i, 0))
    nao_spec = pl.BlockSpec((1, NA_WIDTH // LANES, tm, LANES), lambda i: (i // tiles, 0, i % tiles, 0))
    consts = (gng, wna, wgla, wout, gm, wup, wdown, gf)
    return pl.pallas_call(
        _tail_kernel,
        grid=(n // tm,),
        in_specs=[row(D_MODEL), nao_spec, row(GLA_VAL_WIDTH), row(GLA_VAL_WIDTH),
                  row(GLA_VAL_WIDTH), row(2 * D_MODEL)] + [_const_spec(c.shape) for c in consts],
        out_specs=row(D_MODEL),
        out_shape=jax.ShapeDtypeStruct((n, D_MODEL), F32),
        compiler_params=pltpu.CompilerParams(dimension_semantics=("parallel",), vmem_limit_bytes=VMEM_LIMIT),
        name="tail",
    )(x2, na_o, o_f, o_b, gla_g, gate, *consts)


def _prepare(norm_mix_g, w_in, b_in, na_rpb, gk_fwd_w, gk_fwd_b, gk_bwd_w, gk_bwd_b, gla_norm_g,
             w_br_na, w_br_gla, w_out, norm_mlp_g, w_up, w_down, norm_final_g):
    edges = np.concatenate([[0], np.cumsum(IN_WIDTHS)])
    wcols = lambda a, b: w_in[:, edges[a]:edges[b]]
    bcols = lambda a, b: b_in[None, edges[a]:edges[b]]
    r = GLA_GATE_RANK
    pad = LR_PAD - 2 * r
    wlr = jnp.pad(wcols(7, 9), ((0, 0), (0, pad)))
    blr = jnp.pad(bcols(7, 9), ((0, 0), (0, pad)))
    gkw = jnp.zeros((LR_PAD, 2 * GLA_KEY_WIDTH), F32)
    gkw = gkw.at[:r, :GLA_KEY_WIDTH].set(gk_fwd_w).at[r:2 * r, GLA_KEY_WIDTH:].set(gk_bwd_w)
    gkb = jnp.concatenate([gk_fwd_b, gk_bwd_b])[None]
    return dict(
        proj=(norm_mix_g[None], wcols(0, 1).astype(BF16), bcols(0, 1), wcols(1, 3).astype(BF16), bcols(1, 3),
              wcols(3, 7).astype(BF16), bcols(3, 7),
              wcols(9, 11).astype(BF16), bcols(9, 11), wlr.astype(BF16), blr, gkw.astype(BF16), gkb),
        na_bias=_na_bias_table(na_rpb),
        tail=(gla_norm_g[None], w_br_na.astype(BF16), w_br_gla.astype(BF16), w_out.astype(BF16),
              norm_mlp_g[None], w_up.astype(BF16), w_down.astype(BF16), norm_final_g[None]),
    )


def _trunk(x, prep, *, tm):
    batch, seq, _ = x.shape
    x2 = x.reshape(batch * seq, D_MODEL)
    na_q, na_kv, gla_q, gla_k, gla_v, gla_g, gate, la_f, la_b = _proj_call(
        x2, *prep["proj"], tm=tm, batch=batch, seq=seq)
    na_o = _na_call(na_q, na_kv, prep["na_bias"], batch=batch, seq=seq)
    o_f, o_b = _gla_call(gla_q, gla_k, gla_v, la_f, la_b, batch=batch, seq=seq)
    y = _tail_call(x2, na_o, o_f.reshape(batch * seq, GLA_VAL_WIDTH), o_b.reshape(batch * seq, GLA_VAL_WIDTH),
                   gla_g, gate, *prep["tail"], tm=tm)
    return y.reshape(batch, seq, D_MODEL)


def kernel(x_prompt, x_sample, norm_mix_g, w_in, b_in, na_rpb, gk_fwd_w, gk_fwd_b, gk_bwd_w, gk_bwd_b, gla_norm_g, w_br_na, w_br_gla, w_out, norm_mlp_g, w_up, w_down, norm_final_g):
    assert norm_mix_g.shape[0] == 1, "single-layer trunk"
    prep = _prepare(norm_mix_g[0], w_in[0], b_in[0], na_rpb[0], gk_fwd_w[0], gk_fwd_b[0], gk_bwd_w[0],
                    gk_bwd_b[0], gla_norm_g[0], w_br_na[0], w_br_gla[0], w_out[0], norm_mlp_g[0], w_up[0],
                    w_down[0], norm_final_g)
    return _trunk(x_prompt, prep, tm=ROW_TILE), _trunk(x_sample, prep, tm=ROW_TILE)
```

```python
import functools

import numpy as np
import jax
import jax.numpy as jnp
from jax import lax
from jax.experimental import pallas as pl
from jax.experimental.pallas import tpu as pltpu

D_MODEL = 1024
GRID_W = 64
NA_HEADS = 8
NA_HEAD_DIM = 64
NA_WIDTH = NA_HEADS * NA_HEAD_DIM
NA_KH = 8
NA_KW = 16
GLA_HEADS = 4
GLA_KEY_WIDTH = 256
GLA_VAL_WIDTH = 512
GLA_DK = 64
GLA_DV = 128
GLA_GATE_RANK = 16
GLA_GATE_NORMALIZER = 16.0
D_FF = 4 * D_MODEL
RMS_EPS = 1e-6
LOG2_E = float(np.log2(np.e))
IN_WIDTHS = (NA_WIDTH, NA_WIDTH, NA_WIDTH, GLA_KEY_WIDTH, GLA_KEY_WIDTH, GLA_VAL_WIDTH, GLA_VAL_WIDTH,
             GLA_GATE_RANK, GLA_GATE_RANK, D_MODEL, D_MODEL)

LANES = 128
BF16_ROWS = 16
GLA_CHUNK = 128
GLA_LEVELS = 7
GLA_CHUNKS_PER_STEP = 8
GLA_FINE_LEVELS = 3
LR_PAD = LANES
NA_COL_BLOCKS = GRID_W // NA_KW
NA_QGROUP = 8
NA_ROWS_PER_STEP = 32
NA_LOOKAHEAD = 6
MLP_SLICES = 4
ROW_TILE = 512
VMEM_LIMIT = 56 * 1024 * 1024

BF16 = jnp.bfloat16
F32 = jnp.float32


def _dot(a, b):
    return jnp.dot(a, b, preferred_element_type=F32)


def _dot_nt(a, b):
    return lax.dot_general(a, b, (((1,), (1,)), ((), ())), preferred_element_type=F32)


def _sigmoid(z):
    return 0.5 * jnp.tanh(0.5 * z) + 0.5


def _rms(x):
    return x * lax.rsqrt(jnp.mean(x * x, axis=-1, keepdims=True) + RMS_EPS)


def _proj_kernel(x_ref, g_ref, wq_ref, bq_ref, wkv_ref, bkv_ref, wgla_ref, bgla_ref, wgate_ref, bgate_ref,
                 wlr_ref, blr_ref, gkw_ref, gkb_ref,
                 q_ref, kv_ref, gq_ref, gk_ref, gv_ref, gg_ref, gate_ref, laf_ref, lab_ref):
    xb = (_rms(x_ref[...]) * g_ref[...]).astype(BF16)
    q = (_dot(xb, wq_ref[...]) + bq_ref[...]).astype(BF16)
    for p in range(NA_WIDTH // LANES):
        q_ref[0, p] = q[:, p * LANES:(p + 1) * LANES]
    kv = (_dot(xb, wkv_ref[...]) + bkv_ref[...]).astype(BF16)
    kv = kv.reshape(kv.shape[0] // GRID_W, NA_COL_BLOCKS, NA_KW, kv.shape[1])
    for cb in range(NA_COL_BLOCKS):
        block = kv[:, cb].reshape(-1, kv.shape[3])
        for p in range(2 * NA_WIDTH // LANES):
            kv_ref[0, p, cb] = block[:, p * LANES:(p + 1) * LANES]
    gla = (_dot(xb, wgla_ref[...]) + bgla_ref[...]).astype(BF16)
    w = GLA_KEY_WIDTH
    gq_ref[...] = gla[:, :w]
    gk_ref[...] = gla[:, w:2 * w]
    gv_ref[...] = gla[:, 2 * w:2 * w + GLA_VAL_WIDTH]
    gg_ref[...] = gla[:, 2 * w + GLA_VAL_WIDTH:]
    gate_ref[...] = _sigmoid(_dot(xb, wgate_ref[...]) + bgate_ref[...]).astype(BF16)
    lr = (_dot(xb, wlr_ref[...]) + blr_ref[...]).astype(BF16)
    z = _dot(lr, gkw_ref[...]) + gkb_ref[...]
    log_a = (jnp.minimum(z, 0.0) - jnp.log(1.0 + jnp.exp(-jnp.abs(z)))) * (LOG2_E / GLA_GATE_NORMALIZER)
    laf_ref[...] = log_a[:, :w]
    lab_ref[...] = log_a[:, w:]


def _const_spec(shape):
    nd = len(shape)
    return pl.BlockSpec(shape, lambda *_: (0,) * nd, pipeline_mode=pl.Buffered(1))


def _proj_call(x2, g, wq, bq, wkv, bkv, wgla, bgla, wgate, bgate, wlr, blr, gkw, gkb, *, tm, batch, seq):
    n = x2.shape[0]
    assert seq % tm == 0 and tm % GRID_W == 0
    tiles = seq // tm
    pairs = NA_WIDTH // LANES
    row = lambda w: pl.BlockSpec((tm, w), lambda i: (i, 0))
    consts = (g, wq, bq, wkv, bkv, wgla, bgla, wgate, bgate, wlr, blr, gkw, gkb)
    q_spec = pl.BlockSpec((1, pairs, tm, LANES), lambda i: (i // tiles, 0, i % tiles, 0))
    kv_spec = pl.BlockSpec((1, 2 * pairs, NA_COL_BLOCKS, tm // NA_COL_BLOCKS, LANES),
                           lambda i: (i // tiles, 0, 0, i % tiles, 0))
    widths = (GLA_KEY_WIDTH, GLA_KEY_WIDTH, GLA_VAL_WIDTH, GLA_VAL_WIDTH, 2 * D_MODEL)
    return pl.pallas_call(
        _proj_kernel,
        grid=(n // tm,),
        in_specs=[row(D_MODEL)] + [_const_spec(c.shape) for c in consts],
        out_specs=[q_spec, kv_spec] + [row(w) for w in widths] + [row(GLA_KEY_WIDTH)] * 2,
        out_shape=[jax.ShapeDtypeStruct((batch, pairs, seq, LANES), BF16),
                   jax.ShapeDtypeStruct((batch, 2 * pairs, NA_COL_BLOCKS, seq // NA_COL_BLOCKS, LANES), BF16)]
        + [jax.ShapeDtypeStruct((n, w), BF16) for w in widths]
        + [jax.ShapeDtypeStruct((n, GLA_KEY_WIDTH), F32)] * 2,
        compiler_params=pltpu.CompilerParams(dimension_semantics=("parallel",), vmem_limit_bytes=VMEM_LIMIT),
        name="proj",
    )(x2, *consts)


def _na_needed_blocks():
    need = []
    for g in range(GRID_W // NA_QGROUP):
        c = np.arange(g * NA_QGROUP, (g + 1) * NA_QGROUP)
        start = np.clip(c - NA_KW // 2, 0, GRID_W - NA_KW)
        need.append(sorted(set((start // NA_KW).tolist()) | set(((start + NA_KW - 1) // NA_KW).tolist())))
    return need


NA_NEED = _na_needed_blocks()
NA_HALF_ROWS = []
for _half in range(2):
    _groups = [g for g, blocks in enumerate(NA_NEED) if any(b // 2 == _half for b in blocks)]
    _lo = (min(_groups) * NA_QGROUP) // BF16_ROWS * BF16_ROWS
    _hi = -(-((max(_groups) + 1) * NA_QGROUP) // BF16_ROWS) * BF16_ROWS
    NA_HALF_ROWS.append((_lo, _hi))


def _na_bias_table(rpb):
    c = np.arange(GRID_W)
    col_start = np.clip(c - NA_KW // 2, 0, GRID_W - NA_KW)
    col_mask = (c[None, :] >= col_start[:, None]) & (c[None, :] < col_start[:, None] + NA_KW)
    dc_idx = np.clip(c[None, :] - c[:, None], -(NA_KW - 1), NA_KW - 1) + (NA_KW - 1)
    onehot = (dc_idx[None] == np.arange(2 * NA_KW - 1)[:, None, None]) & col_mask[None]
    cols = jnp.einsum('hrd,dqk->hrqk', rpb.astype(F32), jnp.asarray(onehot, F32),
                      precision=lax.Precision.HIGHEST)
    cols = jnp.where(col_mask[None, None], cols, -jnp.inf)
    heads = rpb.shape[0]
    cols = cols.reshape(heads, 2 * NA_KH - 1, GRID_W, NA_COL_BLOCKS, NA_KW).transpose(0, 2, 3, 1, 4)
    bias = jnp.stack([cols[:, :, :, NA_KH - 1 - off:2 * NA_KH - 1 - off] for off in range(NA_KH)], axis=1)
    return bias.reshape(heads, NA_KH, GRID_W, NA_KH * GRID_W)


def _na_kernel(q_ref, k_ref, v_ref, bias_ref, o_ref, *, rows):
    lane = lax.broadcasted_iota(jnp.int32, (GRID_W, LANES), 1)
    first = lane < NA_HEAD_DIM
    win = NA_KH * NA_KW
    groups = GRID_W // NA_QGROUP

    def window(r):
        start = jnp.clip(r - NA_KH // 2, 0, rows - NA_KH)
        return pl.ds(pl.multiple_of(start * NA_KW, NA_KW), win), r - start

    def tile(ref, keys, half):
        return jnp.concatenate([ref[0, 0, 2 * half, keys, :], ref[0, 0, 2 * half + 1, keys, :]], axis=0)

    def half_rows(x, half):
        lo, hi = NA_HALF_ROWS[half]
        return jnp.concatenate([x[lo:hi], x[GRID_W + lo:GRID_W + hi]], axis=0)

    def scores(r):
        keys, _ = window(r)
        q = q_ref[0, 0, pl.ds(pl.multiple_of(r * GRID_W, GRID_W), GRID_W), :] * (NA_HEAD_DIM ** -0.5)
        zero = jnp.zeros_like(q)
        qs = jnp.concatenate([jnp.where(first, q, zero), jnp.where(first, zero, q)], axis=0)
        return [_dot_nt(half_rows(qs, half), tile(k_ref, keys, half)) for half in range(2)]

    def attend(r, s):
        keys, off = window(r)
        p_rows = [[], []]
        inv = []
        for head in range(2):
            for g in range(groups):
                rows8 = slice(g * NA_QGROUP, (g + 1) * NA_QGROUP)
                pieces = {}
                for cb in NA_NEED[g]:
                    half = cb // 2
                    lo, hi = NA_HALF_ROWS[half]
                    at = head * (hi - lo) + g * NA_QGROUP - lo
                    pieces[cb] = (s[half][at:at + NA_QGROUP, (cb % 2) * LANES:(cb % 2 + 1) * LANES]
                                  + bias_ref[head, off, rows8, cb * LANES:(cb + 1) * LANES])
                m = functools.reduce(jnp.maximum, pieces.values())
                m = jnp.max(m, axis=-1, keepdims=True)
                pieces = {cb: jnp.exp(x - m) for cb, x in pieces.items()}
                total = functools.reduce(jnp.add, pieces.values())
                inv.append(1.0 / jnp.sum(total, axis=-1, keepdims=True))
                for half in range(2):
                    lo, hi = NA_HALF_ROWS[half]
                    if lo <= g * NA_QGROUP < hi:
                        zero = jnp.zeros((NA_QGROUP, LANES), F32)
                        p_rows[half].append(jnp.concatenate(
                            [pieces.get(2 * half, zero), pieces.get(2 * half + 1, zero)], axis=1))
        outs = []
        for half in range(2):
            p = jnp.concatenate(p_rows[half], axis=0).astype(BF16)
            outs.append(_dot(p, tile(v_ref, keys, half)))
        heads_o = []
        for head in range(2):
            rows_o = []
            for g in range(groups):
                acc = None
                for half in range(2):
                    lo, hi = NA_HALF_ROWS[half]
                    if lo <= g * NA_QGROUP < hi:
                        at = head * (hi - lo) + g * NA_QGROUP - lo
                        part = outs[half][at:at + NA_QGROUP]
                        acc = part if acc is None else acc + part
                rows_o.append(acc * inv[head * groups + g])
            heads_o.append(jnp.concatenate(rows_o, axis=0))
        o_ref[0, 0, pl.ds(pl.multiple_of(r * GRID_W, GRID_W), GRID_W), :] = jnp.where(
            first, heads_o[0], heads_o[1]).astype(BF16)

    def body(i, carry):
        rs = [i * NA_ROWS_PER_STEP + u for u in range(NA_ROWS_PER_STEP)]
        ss = [scores(r) for r in rs[:NA_LOOKAHEAD]]
        for u, r in enumerate(rs):
            if u + NA_LOOKAHEAD < len(rs):
                ss.append(scores(rs[u + NA_LOOKAHEAD]))
            attend(r, ss[u])
        return carry

    lax.fori_loop(0, rows // NA_ROWS_PER_STEP, body, 0)


def _na_call(na_q, na_kv, bias, *, batch, seq):
    rows = seq // GRID_W
    assert rows >= NA_KH and seq % GRID_W == 0 and rows % NA_ROWS_PER_STEP == 0
    pairs = NA_WIDTH // LANES
    q_spec = pl.BlockSpec((1, 1, seq, LANES), lambda p, b: (b, p, 0, 0))
    kv_spec = lambda part: pl.BlockSpec((1, 1, NA_COL_BLOCKS, seq // NA_COL_BLOCKS, LANES),
                                        lambda p, b: (b, part * pairs + p, 0, 0, 0))
    return pl.pallas_call(
        functools.partial(_na_kernel, rows=rows),
        grid=(pairs, batch),
        in_specs=[q_spec, kv_spec(0), kv_spec(1),
                  pl.BlockSpec((2, NA_KH, GRID_W, NA_KH * GRID_W), lambda p, b: (p, 0, 0, 0))],
        out_specs=q_spec,
        out_shape=jax.ShapeDtypeStruct((batch, pairs, seq, LANES), BF16),
        compiler_params=pltpu.CompilerParams(dimension_semantics=("parallel", "arbitrary"),
                                             vmem_limit_bytes=VMEM_LIMIT),
        name="na",
    )(na_q, na_kv, na_kv, bias)


def _gla_constants(reverse):
    c = GLA_CHUNK
    t = np.arange(c)
    flip = (lambda a: a[::-1, ::-1]) if reverse else (lambda a: a)
    sums = [flip((t[None, :] <= t[:, None]).astype(np.float32))]
    fine = [np.eye(c, dtype=np.float32)]
    coarse = []
    for lvl in range(GLA_LEVELS):
        m = 1 << lvl
        base = (t // (2 * m)) * (2 * m)
        upper = (t % (2 * m)) >= m
        s = t[None, :]
        mask = flip((upper[:, None] & (~upper)[None, :] & (base[:, None] == base[None, :])).astype(np.float32))
        if lvl < GLA_FINE_LEVELS:
            q_rows = upper[:, None] & (s >= (base + m)[:, None]) & (s <= t[:, None])
            k_rows = (~upper)[:, None] & (s > t[:, None]) & (s <= (base + m - 1)[:, None])
            sums.append(flip((q_rows | k_rows).astype(np.float32)))
            fine.append(mask)
        else:
            coarse.append(mask[mask.any(axis=1)])
    two_heads = lambda mk: np.concatenate([mk, mk], axis=0)
    return (jnp.asarray(np.concatenate(sums, axis=0), BF16),
            jnp.asarray(np.stack([two_heads(mk) for mk in fine]), F32),
            jnp.asarray(np.stack([two_heads(mk) for mk in coarse]), F32))


def _stack_heads(x):
    first = lax.broadcasted_iota(jnp.int32, x.shape, 1) < GLA_DK
    zero = jnp.zeros_like(x)
    return jnp.concatenate([jnp.where(first, x, zero), jnp.where(first, zero, x)], axis=0)


class _GlaUnit:
    def __init__(self, q_ref, k_ref, v_ref, la_ref, sums_ref, fine_ref, coarse_ref, state_ref, o_ref, reverse,
                 chunk, pair):
        self.refs = (fine_ref, coarse_ref, state_ref, o_ref)
        self.reverse, self.pair = reverse, pair
        c, w = GLA_CHUNK, LANES
        self.rows = rows = slice(chunk * c, (chunk + 1) * c)
        ks = slice(pair * LANES, (pair + 1) * LANES)
        la = la_ref[0, rows, ks]
        la_hi = la.astype(BF16)
        la_lo = (la - la_hi.astype(F32)).astype(BF16)
        sums = _dot(sums_ref[...], jnp.concatenate([la_hi, la_lo], axis=1))
        sums = sums[:, :w] + sums[:, w:]
        cum = sums[:c]
        last_row = 0 if reverse else c - 1
        cum_last = cum[last_row:last_row + 1]
        q = q_ref[0, rows, ks].astype(F32) * (GLA_DK ** -0.5)
        k = k_ref[0, rows, ks].astype(F32)
        self.v = v_ref[0, rows, pair * 2 * GLA_DV:(pair + 1) * 2 * GLA_DV]
        self.q_lvl = [q.astype(BF16)]
        self.k_lvl = [k.astype(BF16)]
        for lvl in range(GLA_FINE_LEVELS):
            e = jnp.exp2(sums[(lvl + 1) * c:(lvl + 2) * c])
            self.q_lvl.append((q * e).astype(BF16))
            self.k_lvl.append((k * e).astype(BF16))
        for lvl in range(GLA_FINE_LEVELS, GLA_LEVELS):
            m = 1 << lvl
            split = lambda x: x.reshape(c // (2 * m), 2 * m, w)
            cum3, q3, k3 = split(cum), split(q), split(k)
            lo, hi = slice(0, m), slice(m, 2 * m)
            qh, kh = (lo, hi) if reverse else (hi, lo)
            ref = cum3[:, m:m + 1] if reverse else cum3[:, m - 1:m]
            self.q_lvl.append((q3[:, qh] * jnp.exp2(cum3[:, qh] - ref)).reshape(c // 2, w).astype(BF16))
            k_scaled = k3[:, kh] * jnp.exp2(ref - cum3[:, kh])
            k_parts = [k3[:, lo], k_scaled] if reverse else [k_scaled, k3[:, hi]]
            self.k_lvl.append(jnp.concatenate(k_parts, axis=1).reshape(c, w).astype(BF16))
        self.q_dec = (q * jnp.exp2(cum)).astype(BF16)
        self.k_dec = (k * jnp.exp2(cum_last - cum)).astype(BF16)
        self.decay = jnp.exp2(cum_last)

    def scores(self):
        fine_ref, coarse_ref, _, _ = self.refs
        c = GLA_CHUNK
        level = lambda lvl: _dot_nt(_stack_heads(self.q_lvl[lvl]), self.k_lvl[lvl])
        scores = fine_ref[0] * level(0)
        for lvl in range(GLA_FINE_LEVELS):
            scores += fine_ref[lvl + 1] * level(lvl + 1)
        for lvl in range(GLA_FINE_LEVELS, GLA_LEVELS):
            m = 1 << lvl
            x = (coarse_ref[lvl - GLA_FINE_LEVELS] * level(lvl + 1)).reshape(2 * c // (2 * m), m, c)
            zero = jnp.zeros_like(x)
            scores += jnp.concatenate([x, zero] if self.reverse else [zero, x], axis=1).reshape(2 * c, c)
        self.s = scores.astype(BF16)

    def output(self):
        _, _, state_ref, o_ref = self.refs
        c, pair = GLA_CHUNK, self.pair
        state = state_ref[pair]
        v_t = self.v.T
        lhs = jnp.concatenate([self.s, _stack_heads(self.q_dec)], axis=1)
        rhs = jnp.concatenate([v_t, state.astype(BF16)], axis=1)
        o = _dot_nt(lhs, rhs)
        o_ref[0, self.rows, pair * 2 * GLA_DV:pair * 2 * GLA_DV + GLA_DV] = o[:c, :GLA_DV].astype(o_ref.dtype)
        o_ref[0, self.rows, pair * 2 * GLA_DV + GLA_DV:(pair + 1) * 2 * GLA_DV] = o[c:, GLA_DV:].astype(o_ref.dtype)
        upd = _dot(v_t, self.k_dec)
        row = lax.broadcasted_iota(jnp.int32, upd.shape, 0)
        col = lax.broadcasted_iota(jnp.int32, upd.shape, 1)
        own = (row < GLA_DV) == (col < GLA_DK)
        state_ref[pair] = state * self.decay + jnp.where(own, upd, 0.0)


def _gla_kernel(qf_ref, kf_ref, vf_ref, laf_ref, qb_ref, kb_ref, vb_ref, lab_ref,
                sums_f_ref, fine_f_ref, coarse_f_ref, sums_b_ref, fine_b_ref, coarse_b_ref,
                of_ref, ob_ref, state_f_ref, state_b_ref):
    @pl.when(pl.program_id(1) == 0)
    def _():
        state_f_ref[...] = jnp.zeros_like(state_f_ref)
        state_b_ref[...] = jnp.zeros_like(state_b_ref)

    fwd = (qf_ref, kf_ref, vf_ref, laf_ref, sums_f_ref, fine_f_ref, coarse_f_ref, state_f_ref, of_ref, False)
    bwd = (qb_ref, kb_ref, vb_ref, lab_ref, sums_b_ref, fine_b_ref, coarse_b_ref, state_b_ref, ob_ref, True)
    order = []
    for j in range(GLA_CHUNKS_PER_STEP):
        for pair in range(GLA_HEADS // 2):
            order.append(fwd + (j, pair))
            order.append(bwd + (GLA_CHUNKS_PER_STEP - 1 - j, pair))
    units = [_GlaUnit(*order[0])]
    for i in range(len(order)):
        if i + 1 < len(order):
            units.append(_GlaUnit(*order[i + 1]))
        units[i].scores()
        if i > 0:
            units[i - 1].output()
    units[-1].output()


def _gla_call(q, k, v, la_f, la_b, *, batch, seq):
    c = GLA_CHUNK * GLA_CHUNKS_PER_STEP
    assert seq % c == 0
    n = seq // c
    fwd = lambda b, i: i
    bwd = lambda b, i: n - 1 - i
    consts = _gla_constants(False) + _gla_constants(True)
    spec = lambda chunk, w: pl.BlockSpec((1, c, w), lambda b, i: (b, chunk(b, i), 0))
    specs = lambda chunk: [spec(chunk, GLA_KEY_WIDTH)] * 2 + [spec(chunk, GLA_VAL_WIDTH), spec(chunk, GLA_KEY_WIDTH)]
    state = pltpu.VMEM((GLA_HEADS // 2, 2 * GLA_DV, LANES), F32)
    b3 = lambda a: a.reshape(batch, seq, a.shape[-1])
    q, k, v, la_f, la_b = b3(q), b3(k), b3(v), b3(la_f), b3(la_b)
    return pl.pallas_call(
        _gla_kernel,
        grid=(batch, n),
        in_specs=specs(fwd) + specs(bwd) + [_const_spec(a.shape) for a in consts],
        out_specs=[spec(fwd, GLA_VAL_WIDTH), spec(bwd, GLA_VAL_WIDTH)],
        out_shape=[jax.ShapeDtypeStruct((batch, seq, GLA_VAL_WIDTH), BF16)] * 2,
        scratch_shapes=[state, state],
        compiler_params=pltpu.CompilerParams(dimension_semantics=("parallel", "arbitrary"),
                                             vmem_limit_bytes=VMEM_LIMIT),
        name="gla",
    )(q, k, v, la_f, q, k, v, la_b, *consts)


def _tail_kernel(x_ref, nao_ref, of_ref, ob_ref, g_ref, gate_ref, gng_ref, wna_ref, wgla_ref, wout_ref,
                 gm_ref, wup_ref, wdown_ref, gf_ref, y_ref):
    na_o = jnp.concatenate([nao_ref[0, p] for p in range(NA_WIDTH // LANES)], axis=1)
    na_out = _dot(na_o, wna_ref[...])
    o = of_ref[...].astype(F32) + ob_ref[...].astype(F32)
    heads = []
    for h in range(GLA_HEADS):
        heads.append(_rms(o[:, h * GLA_DV:(h + 1) * GLA_DV]) * gng_ref[...])
    g = g_ref[...].astype(F32)
    gla_o = jnp.concatenate(heads, axis=1) * (g * _sigmoid(g))
    gla_out = _dot(gla_o.astype(BF16), wgla_ref[...])
    gate = gate_ref[...].astype(F32)
    merged = gate[:, :D_MODEL] * na_out + gate[:, D_MODEL:] * gla_out
    h = x_ref[...] + _dot(merged.astype(BF16), wout_ref[...])
    u = (_rms(h) * gm_ref[...]).astype(BF16)
    width = D_FF // MLP_SLICES
    for j in range(MLP_SLICES):
        cols = slice(j * width, (j + 1) * width)
        hdn = jnp.square(jnp.maximum(_dot(u, wup_ref[:, cols]), 0.0)).astype(BF16)
        h = h + _dot(hdn, wdown_ref[cols, :])
    y_ref[...] = _rms(h) * gf_ref[...]


def _tail_call(x2, na_o, o_f, o_b, gla_g, gate, gng, wna, wgla, wout, gm, wup, wdown, gf, *, tm):
    n = x2.shape[0]
    tiles = na_o.shape[2] // tm
    row = lambda w: pl.BlockSpec((tm, w), lambda i: (i, 0))
    nao_spec = pl.BlockSpec((1, NA_WIDTH // LANES, tm, LANES), lambda i: (i // tiles, 0, i % tiles, 0))
    consts = (gng, wna, wgla, wout, gm, wup, wdown, gf)
    return pl.pallas_call(
        _tail_kernel,
        grid=(n // tm,),
        in_specs=[row(D_MODEL), nao_spec, row(GLA_VAL_WIDTH), row(GLA_VAL_WIDTH),
                  row(GLA_VAL_WIDTH), row(2 * D_MODEL)] + [_const_spec(c.shape) for c in consts],
        out_specs=row(D_MODEL),
        out_shape=jax.ShapeDtypeStruct((n, D_MODEL), F32),
        compiler_params=pltpu.CompilerParams(dimension_semantics=("parallel",), vmem_limit_bytes=VMEM_LIMIT),
        name="tail",
    )(x2, na_o, o_f, o_b, gla_g, gate, *consts)


def _prepare(norm_mix_g, w_in, b_in, na_rpb, gk_fwd_w, gk_fwd_b, gk_bwd_w, gk_bwd_b, gla_norm_g,
             w_br_na, w_br_gla, w_out, norm_mlp_g, w_up, w_down, norm_final_g):
    edges = np.concatenate([[0], np.cumsum(IN_WIDTHS)])
    wcols = lambda a, b: w_in[:, edges[a]:edges[b]]
    bcols = lambda a, b: b_in[None, edges[a]:edges[b]]
    r = GLA_GATE_RANK
    pad = LR_PAD - 2 * r
    wlr = jnp.pad(wcols(7, 9), ((0, 0), (0, pad)))
    blr = jnp.pad(bcols(7, 9), ((0, 0), (0, pad)))
    gkw = jnp.zeros((LR_PAD, 2 * GLA_KEY_WIDTH), F32)
    gkw = gkw.at[:r, :GLA_KEY_WIDTH].set(gk_fwd_w).at[r:2 * r, GLA_KEY_WIDTH:].set(gk_bwd_w)
    gkb = jnp.concatenate([gk_fwd_b, gk_bwd_b])[None]
    return dict(
        proj=(norm_mix_g[None], wcols(0, 1).astype(BF16), bcols(0, 1), wcols(1, 3).astype(BF16), bcols(1, 3),
              wcols(3, 7).astype(BF16), bcols(3, 7),
              wcols(9, 11).astype(BF16), bcols(9, 11), wlr.astype(BF16), blr, gkw.astype(BF16), gkb),
        na_bias=_na_bias_table(na_rpb),
        tail=(gla_norm_g[None], w_br_na.astype(BF16), w_br_gla.astype(BF16), w_out.astype(BF16),
              norm_mlp_g[None], w_up.astype(BF16), w_down.astype(BF16), norm_final_g[None]),
    )


def _trunk(x, prep, *, tm):
    batch, seq, _ = x.shape
    x2 = x.reshape(batch * seq, D_MODEL)
    na_q, na_kv, gla_q, gla_k, gla_v, gla_g, gate, la_f, la_b = _proj_call(
        x2, *prep["proj"], tm=tm, batch=batch, seq=seq)
    na_o = _na_call(na_q, na_kv, prep["na_bias"], batch=batch, seq=seq)
    o_f, o_b = _gla_call(gla_q, gla_k, gla_v, la_f, la_b, batch=batch, seq=seq)
    y = _tail_call(x2, na_o, o_f.reshape(batch * seq, GLA_VAL_WIDTH), o_b.reshape(batch * seq, GLA_VAL_WIDTH),
                   gla_g, gate, *prep["tail"], tm=tm)
    return y.reshape(batch, seq, D_MODEL)


def kernel(x_prompt, x_sample, norm_mix_g, w_in, b_in, na_rpb, gk_fwd_w, gk_fwd_b, gk_bwd_w, gk_bwd_b, gla_norm_g, w_br_na, w_br_gla, w_out, norm_mlp_g, w_up, w_down, norm_final_g):
    assert norm_mix_g.shape[0] == 1, "single-layer trunk"
    prep = _prepare(norm_mix_g[0], w_in[0], b_in[0], na_rpb[0], gk_fwd_w[0], gk_fwd_b[0], gk_bwd_w[0],
                    gk_bwd_b[0], gla_norm_g[0], w_br_na[0], w_br_gla[0], w_out[0], norm_mlp_g[0], w_up[0],
                    w_down[0], norm_final_g)
    return _trunk(x_prompt, prep, tm=ROW_TILE), _trunk(x_sample, prep, tm=ROW_TILE)
```

```python
import functools

import numpy as np
import jax
import jax.numpy as jnp
from jax import lax
from jax.experimental import pallas as pl
from jax.experimental.pallas import tpu as pltpu

D_MODEL = 1024
GRID_W = 64
NA_HEADS = 8
NA_HEAD_DIM = 64
NA_WIDTH = NA_HEADS * NA_HEAD_DIM
NA_KH = 8
NA_KW = 16
GLA_HEADS = 4
GLA_KEY_WIDTH = 256
GLA_VAL_WIDTH = 512
GLA_DK = 64
GLA_DV = 128
GLA_GATE_RANK = 16
GLA_GATE_NORMALIZER = 16.0
D_FF = 4 * D_MODEL
RMS_EPS = 1e-6
LOG2_E = float(np.log2(np.e))
IN_WIDTHS = (NA_WIDTH, NA_WIDTH, NA_WIDTH, GLA_KEY_WIDTH, GLA_KEY_WIDTH, GLA_VAL_WIDTH, GLA_VAL_WIDTH,
             GLA_GATE_RANK, GLA_GATE_RANK, D_MODEL, D_MODEL)

LANES = 128
BF16_ROWS = 16
GLA_CHUNK = 128
GLA_LEVELS = 7
GLA_CHUNKS_PER_STEP = 8
GLA_FINE_LEVELS = 3
LR_PAD = LANES
NA_COL_BLOCKS = GRID_W // NA_KW
NA_QGROUP = 8
NA_ROWS_PER_STEP = 32
NA_LOOKAHEAD = 6
MLP_SLICES = 4
ROW_TILE = 512
VMEM_LIMIT = 56 * 1024 * 1024

BF16 = jnp.bfloat16
F32 = jnp.float32


def _dot(a, b):
    return jnp.dot(a, b, preferred_element_type=F32)


def _dot_nt(a, b):
    return lax.dot_general(a, b, (((1,), (1,)), ((), ())), preferred_element_type=F32)


def _sigmoid(z):
    return 0.5 * jnp.tanh(0.5 * z) + 0.5


def _rms(x):
    return x * lax.rsqrt(jnp.mean(x * x, axis=-1, keepdims=True) + RMS_EPS)


def _proj_kernel(x_ref, g_ref, wq_ref, bq_ref, wkv_ref, bkv_ref, wgla_ref, bgla_ref, wgate_ref, bgate_ref,
                 wlr_ref, blr_ref, gkw_ref, gkb_ref,
                 q_ref, kv_ref, gq_ref, gk_ref, gv_ref, gg_ref, gate_ref, laf_ref, lab_ref):
    xb = (_rms(x_ref[...]) * g_ref[...]).astype(BF16)
    w = GLA_KEY_WIDTH
    gate_ref[...] = _sigmoid(_dot(xb, wgate_ref[...]) + bgate_ref[...]).astype(BF16)
    lr = (_dot(xb, wlr_ref[...]) + blr_ref[...]).astype(BF16)
    kv = (_dot(xb, wkv_ref[...]) + bkv_ref[...]).astype(BF16)
    kv = kv.reshape(kv.shape[0] // GRID_W, NA_COL_BLOCKS, NA_KW, kv.shape[1])
    for cb in range(NA_COL_BLOCKS):
        block = kv[:, cb].reshape(-1, kv.shape[3])
        for p in range(2 * NA_WIDTH // LANES):
            kv_ref[0, p, cb] = block[:, p * LANES:(p + 1) * LANES]
    z = _dot(lr, gkw_ref[...]) + gkb_ref[...]
    log_a = (jnp.minimum(z, 0.0) - jnp.log(1.0 + jnp.exp(-jnp.abs(z)))) * (LOG2_E / GLA_GATE_NORMALIZER)
    laf_ref[...] = log_a[:, :w]
    lab_ref[...] = log_a[:, w:]
    q = (_dot(xb, wq_ref[...]) + bq_ref[...]).astype(BF16)
    for p in range(NA_WIDTH // LANES):
        q_ref[0, p] = q[:, p * LANES:(p + 1) * LANES]
    gla = (_dot(xb, wgla_ref[...]) + bgla_ref[...]).astype(BF16)
    gq_ref[...] = gla[:, :w]
    gk_ref[...] = gla[:, w:2 * w]
    gv_ref[...] = gla[:, 2 * w:2 * w + GLA_VAL_WIDTH]
    gg_ref[...] = gla[:, 2 * w + GLA_VAL_WIDTH:]


def _const_spec(shape):
    nd = len(shape)
    return pl.BlockSpec(shape, lambda *_: (0,) * nd, pipeline_mode=pl.Buffered(1))


def _proj_call(x2, g, wq, bq, wkv, bkv, wgla, bgla, wgate, bgate, wlr, blr, gkw, gkb, *, tm, batch, seq):
    n = x2.shape[0]
    assert seq % tm == 0 and tm % GRID_W == 0
    tiles = seq // tm
    pairs = NA_WIDTH // LANES
    row = lambda w: pl.BlockSpec((tm, w), lambda i: (i, 0))
    consts = (g, wq, bq, wkv, bkv, wgla, bgla, wgate, bgate, wlr, blr, gkw, gkb)
    q_spec = pl.BlockSpec((1, pairs, tm, LANES), lambda i: (i // tiles, 0, i % tiles, 0))
    kv_spec = pl.BlockSpec((1, 2 * pairs, NA_COL_BLOCKS, tm // NA_COL_BLOCKS, LANES),
                           lambda i: (i // tiles, 0, 0, i % tiles, 0))
    widths = (GLA_KEY_WIDTH, GLA_KEY_WIDTH, GLA_VAL_WIDTH, GLA_VAL_WIDTH, 2 * D_MODEL)
    return pl.pallas_call(
        _proj_kernel,
        grid=(n // tm,),
        in_specs=[row(D_MODEL)] + [_const_spec(c.shape) for c in consts],
        out_specs=[q_spec, kv_spec] + [row(w) for w in widths] + [row(GLA_KEY_WIDTH)] * 2,
        out_shape=[jax.ShapeDtypeStruct((batch, pairs, seq, LANES), BF16),
                   jax.ShapeDtypeStruct((batch, 2 * pairs, NA_COL_BLOCKS, seq // NA_COL_BLOCKS, LANES), BF16)]
        + [jax.ShapeDtypeStruct((n, w), BF16) for w in widths]
        + [jax.ShapeDtypeStruct((n, GLA_KEY_WIDTH), F32)] * 2,
        compiler_params=pltpu.CompilerParams(dimension_semantics=("parallel",), vmem_limit_bytes=VMEM_LIMIT),
        name="proj",
    )(x2, *consts)


def _na_needed_blocks():
    need = []
    for g in range(GRID_W // NA_QGROUP):
        c = np.arange(g * NA_QGROUP, (g + 1) * NA_QGROUP)
        start = np.clip(c - NA_KW // 2, 0, GRID_W - NA_KW)
        need.append(sorted(set((start // NA_KW).tolist()) | set(((start + NA_KW - 1) // NA_KW).tolist())))
    return need


NA_NEED = _na_needed_blocks()
NA_HALF_ROWS = []
for _half in range(2):
    _groups = [g for g, blocks in enumerate(NA_NEED) if any(b // 2 == _half for b in blocks)]
    _lo = (min(_groups) * NA_QGROUP) // BF16_ROWS * BF16_ROWS
    _hi = -(-((max(_groups) + 1) * NA_QGROUP) // BF16_ROWS) * BF16_ROWS
    NA_HALF_ROWS.append((_lo, _hi))


def _na_bias_table(rpb):
    c = np.arange(GRID_W)
    col_start = np.clip(c - NA_KW // 2, 0, GRID_W - NA_KW)
    col_mask = (c[None, :] >= col_start[:, None]) & (c[None, :] < col_start[:, None] + NA_KW)
    dc_idx = np.clip(c[None, :] - c[:, None], -(NA_KW - 1), NA_KW - 1) + (NA_KW - 1)
    onehot = (dc_idx[None] == np.arange(2 * NA_KW - 1)[:, None, None]) & col_mask[None]
    cols = jnp.einsum('hrd,dqk->hrqk', rpb.astype(F32), jnp.asarray(onehot, F32),
                      precision=lax.Precision.HIGHEST)
    cols = jnp.where(col_mask[None, None], cols, -jnp.inf)
    heads = rpb.shape[0]
    cols = cols.reshape(heads, 2 * NA_KH - 1, GRID_W, NA_COL_BLOCKS, NA_KW).transpose(0, 2, 3, 1, 4)
    bias = jnp.stack([cols[:, :, :, NA_KH - 1 - off:2 * NA_KH - 1 - off] for off in range(NA_KH)], axis=1)
    return bias.reshape(heads, NA_KH, GRID_W, NA_KH * GRID_W)


def _na_kernel(q_ref, k_ref, v_ref, bias_ref, o_ref, *, rows):
    lane = lax.broadcasted_iota(jnp.int32, (GRID_W, LANES), 1)
    first = lane < NA_HEAD_DIM
    win = NA_KH * NA_KW
    groups = GRID_W // NA_QGROUP

    def window(r):
        start = jnp.clip(r - NA_KH // 2, 0, rows - NA_KH)
        return pl.ds(pl.multiple_of(start * NA_KW, NA_KW), win), r - start

    def tile(ref, keys, half):
        return jnp.concatenate([ref[0, 0, 2 * half, keys, :], ref[0, 0, 2 * half + 1, keys, :]], axis=0)

    def half_rows(x, half):
        lo, hi = NA_HALF_ROWS[half]
        return jnp.concatenate([x[lo:hi], x[GRID_W + lo:GRID_W + hi]], axis=0)

    def scores(r):
        keys, _ = window(r)
        q = q_ref[0, 0, pl.ds(pl.multiple_of(r * GRID_W, GRID_W), GRID_W), :] * (NA_HEAD_DIM ** -0.5)
        zero = jnp.zeros_like(q)
        qs = jnp.concatenate([jnp.where(first, q, zero), jnp.where(first, zero, q)], axis=0)
        return [_dot_nt(half_rows(qs, half), tile(k_ref, keys, half)) for half in range(2)]

    def attend(r, s):
        keys, off = window(r)
        p_rows = [[], []]
        inv = []
        for head in range(2):
            for g in range(groups):
                rows8 = slice(g * NA_QGROUP, (g + 1) * NA_QGROUP)
                pieces = {}
                for cb in NA_NEED[g]:
                    half = cb // 2
                    lo, hi = NA_HALF_ROWS[half]
                    at = head * (hi - lo) + g * NA_QGROUP - lo
                    pieces[cb] = (s[half][at:at + NA_QGROUP, (cb % 2) * LANES:(cb % 2 + 1) * LANES]
                                  + bias_ref[head, off, rows8, cb * LANES:(cb + 1) * LANES])
                m = functools.reduce(jnp.maximum, pieces.values())
                m = jnp.max(m, axis=-1, keepdims=True)
                pieces = {cb: jnp.exp(x - m) for cb, x in pieces.items()}
                total = functools.reduce(jnp.add, pieces.values())
                inv.append(1.0 / jnp.sum(total, axis=-1, keepdims=True))
                for half in range(2):
                    lo, hi = NA_HALF_ROWS[half]
                    if lo <= g * NA_QGROUP < hi:
                        zero = jnp.zeros((NA_QGROUP, LANES), F32)
                        p_rows[half].append(jnp.concatenate(
                            [pieces.get(2 * half, zero), pieces.get(2 * half + 1, zero)], axis=1))
        outs = []
        for half in range(2):
            p = jnp.concatenate(p_rows[half], axis=0).astype(BF16)
            outs.append(_dot(p, tile(v_ref, keys, half)))
        heads_o = []
        for head in range(2):
            rows_o = []
            for g in range(groups):
                acc = None
                for half in range(2):
                    lo, hi = NA_HALF_ROWS[half]
                    if lo <= g * NA_QGROUP < hi:
                        at = head * (hi - lo) + g * NA_QGROUP - lo
                        part = outs[half][at:at + NA_QGROUP]
                        acc = part if acc is None else acc + part
                rows_o.append(acc * inv[head * groups + g])
            heads_o.append(jnp.concatenate(rows_o, axis=0))
        o_ref[0, 0, pl.ds(pl.multiple_of(r * GRID_W, GRID_W), GRID_W), :] = jnp.where(
            first, heads_o[0], heads_o[1]).astype(BF16)

    def body(i, carry):
        rs = [i * NA_ROWS_PER_STEP + u for u in range(NA_ROWS_PER_STEP)]
        ss = [scores(r) for r in rs[:NA_LOOKAHEAD]]
        for u, r in enumerate(rs):
            if u + NA_LOOKAHEAD < len(rs):
                ss.append(scores(rs[u + NA_LOOKAHEAD]))
            attend(r, ss[u])
        return carry

    lax.fori_loop(0, rows // NA_ROWS_PER_STEP, body, 0)


def _na_call(na_q, na_kv, bias, *, batch, seq):
    rows = seq // GRID_W
    assert rows >= NA_KH and seq % GRID_W == 0 and rows % NA_ROWS_PER_STEP == 0
    pairs = NA_WIDTH // LANES
    q_spec = pl.BlockSpec((1, 1, seq, LANES), lambda p, b: (b, p, 0, 0))
    kv_spec = lambda part: pl.BlockSpec((1, 1, NA_COL_BLOCKS, seq // NA_COL_BLOCKS, LANES),
                                        lambda p, b: (b, part * pairs + p, 0, 0, 0))
    return pl.pallas_call(
        functools.partial(_na_kernel, rows=rows),
        grid=(pairs, batch),
        in_specs=[q_spec, kv_spec(0), kv_spec(1),
                  pl.BlockSpec((2, NA_KH, GRID_W, NA_KH * GRID_W), lambda p, b: (p, 0, 0, 0))],
        out_specs=q_spec,
        out_shape=jax.ShapeDtypeStruct((batch, pairs, seq, LANES), BF16),
        compiler_params=pltpu.CompilerParams(dimension_semantics=("parallel", "arbitrary"),
                                             vmem_limit_bytes=VMEM_LIMIT),
        name="na",
    )(na_q, na_kv, na_kv, bias)


def _gla_constants(reverse):
    c = GLA_CHUNK
    t = np.arange(c)
    flip = (lambda a: a[::-1, ::-1]) if reverse else (lambda a: a)
    sums = [flip((t[None, :] <= t[:, None]).astype(np.float32))]
    fine = [np.eye(c, dtype=np.float32)]
    coarse = []
    for lvl in range(GLA_LEVELS):
        m = 1 << lvl
        base = (t // (2 * m)) * (2 * m)
        upper = (t % (2 * m)) >= m
        s = t[None, :]
        mask = flip((upper[:, None] & (~upper)[None, :] & (base[:, None] == base[None, :])).astype(np.float32))
        if lvl < GLA_FINE_LEVELS:
            q_rows = upper[:, None] & (s >= (base + m)[:, None]) & (s <= t[:, None])
            k_rows = (~upper)[:, None] & (s > t[:, None]) & (s <= (base + m - 1)[:, None])
            sums.append(flip((q_rows | k_rows).astype(np.float32)))
            fine.append(mask)
        else:
            coarse.append(mask[mask.any(axis=1)])
    two_heads = lambda mk: np.concatenate([mk, mk], axis=0)
    return (jnp.asarray(np.concatenate(sums, axis=0), BF16),
            jnp.asarray(np.stack([two_heads(mk) for mk in fine]), F32),
            jnp.asarray(np.stack([two_heads(mk) for mk in coarse]), F32))


def _stack_heads(x):
    first = lax.broadcasted_iota(jnp.int32, x.shape, 1) < GLA_DK
    zero = jnp.zeros_like(x)
    return jnp.concatenate([jnp.where(first, x, zero), jnp.where(first, zero, x)], axis=0)


class _GlaUnit:
    def __init__(self, q_ref, k_ref, v_ref, la_ref, sums_ref, fine_ref, coarse_ref, state_ref, o_ref, reverse,
                 chunk, pair):
        self.refs = (fine_ref, coarse_ref, state_ref, o_ref)
        self.reverse, self.pair = reverse, pair
        c, w = GLA_CHUNK, LANES
        self.rows = rows = slice(chunk * c, (chunk + 1) * c)
        ks = slice(pair * LANES, (pair + 1) * LANES)
        la = la_ref[0, rows, ks]
        la_hi = la.astype(BF16)
        la_lo = (la - la_hi.astype(F32)).astype(BF16)
        sums = _dot(sums_ref[...], jnp.concatenate([la_hi, la_lo], axis=1))
        sums = sums[:, :w] + sums[:, w:]
        cum = sums[:c]
        last_row = 0 if reverse else c - 1
        cum_last = cum[last_row:last_row + 1]
        q = q_ref[0, rows, ks].astype(F32) * (GLA_DK ** -0.5)
        k = k_ref[0, rows, ks].astype(F32)
        self.v = v_ref[0, rows, pair * 2 * GLA_DV:(pair + 1) * 2 * GLA_DV]
        self.q_lvl = [q.astype(BF16)]
        self.k_lvl = [k.astype(BF16)]
        for lvl in range(GLA_FINE_LEVELS):
            e = jnp.exp2(sums[(lvl + 1) * c:(lvl + 2) * c])
            self.q_lvl.append((q * e).astype(BF16))
            self.k_lvl.append((k * e).astype(BF16))
        for lvl in range(GLA_FINE_LEVELS, GLA_LEVELS):
            m = 1 << lvl
            split = lambda x: x.reshape(c // (2 * m), 2 * m, w)
            cum3, q3, k3 = split(cum), split(q), split(k)
            lo, hi = slice(0, m), slice(m, 2 * m)
            qh, kh = (lo, hi) if reverse else (hi, lo)
            ref = cum3[:, m:m + 1] if reverse else cum3[:, m - 1:m]
            self.q_lvl.append((q3[:, qh] * jnp.exp2(cum3[:, qh] - ref)).reshape(c // 2, w).astype(BF16))
            k_scaled = k3[:, kh] * jnp.exp2(ref - cum3[:, kh])
            k_parts = [k3[:, lo], k_scaled] if reverse else [k_scaled, k3[:, hi]]
            self.k_lvl.append(jnp.concatenate(k_parts, axis=1).reshape(c, w).astype(BF16))
        self.q_dec = (q * jnp.exp2(cum)).astype(BF16)
        self.k_dec = (k * jnp.exp2(cum_last - cum)).astype(BF16)
        self.decay = jnp.exp2(cum_last)

    def scores(self):
        fine_ref, coarse_ref, _, _ = self.refs
        c = GLA_CHUNK
        level = lambda lvl: _dot_nt(_stack_heads(self.q_lvl[lvl]), self.k_lvl[lvl])
        scores = fine_ref[0] * level(0)
        for lvl in range(GLA_FINE_LEVELS):
            scores += fine_ref[lvl + 1] * level(lvl + 1)
        for lvl in range(GLA_FINE_LEVELS, GLA_LEVELS):
            m = 1 << lvl
            x = (coarse_ref[lvl - GLA_FINE_LEVELS] * level(lvl + 1)).reshape(2 * c // (2 * m), m, c)
            zero = jnp.zeros_like(x)
            scores += jnp.concatenate([x, zero] if self.reverse else [zero, x], axis=1).reshape(2 * c, c)
        self.s = scores.astype(BF16)

    def output(self):
        _, _, state_ref, o_ref = self.refs
        c, pair = GLA_CHUNK, self.pair
        state = state_ref[pair]
        v_t = self.v.T
        lhs = jnp.concatenate([self.s, _stack_heads(self.q_dec)], axis=1)
        rhs = jnp.concatenate([v_t, state.astype(BF16)], axis=1)
        o = _dot_nt(lhs, rhs)
        o_ref[0, self.rows, pair * 2 * GLA_DV:pair * 2 * GLA_DV + GLA_DV] = o[:c, :GLA_DV].astype(o_ref.dtype)
        o_ref[0, self.rows, pair * 2 * GLA_DV + GLA_DV:(pair + 1) * 2 * GLA_DV] = o[c:, GLA_DV:].astype(o_ref.dtype)
        upd = _dot(v_t, self.k_dec)
        row = lax.broadcasted_iota(jnp.int32, upd.shape, 0)
        col = lax.broadcasted_iota(jnp.int32, upd.shape, 1)
        own = (row < GLA_DV) == (col < GLA_DK)
        state_ref[pair] = state * self.decay + jnp.where(own, upd, 0.0)


def _gla_kernel(qf_ref, kf_ref, vf_ref, laf_ref, qb_ref, kb_ref, vb_ref, lab_ref,
                sums_f_ref, fine_f_ref, coarse_f_ref, sums_b_ref, fine_b_ref, coarse_b_ref,
                of_ref, ob_ref, state_f_ref, state_b_ref):
    @pl.when(pl.program_id(1) == 0)
    def _():
        state_f_ref[...] = jnp.zeros_like(state_f_ref)
        state_b_ref[...] = jnp.zeros_like(state_b_ref)

    fwd = (qf_ref, kf_ref, vf_ref, laf_ref, sums_f_ref, fine_f_ref, coarse_f_ref, state_f_ref, of_ref, False)
    bwd = (qb_ref, kb_ref, vb_ref, lab_ref, sums_b_ref, fine_b_ref, coarse_b_ref, state_b_ref, ob_ref, True)
    order = []
    for j in range(GLA_CHUNKS_PER_STEP):
        for pair in range(GLA_HEADS // 2):
            order.append(fwd + (j, pair))
            order.append(bwd + (GLA_CHUNKS_PER_STEP - 1 - j, pair))
    units = [_GlaUnit(*order[0])]
    for i in range(len(order)):
        if i + 1 < len(order):
            units.append(_GlaUnit(*order[i + 1]))
        units[i].scores()
        if i > 0:
            units[i - 1].output()
    units[-1].output()


def _gla_call(q, k, v, la_f, la_b, *, batch, seq):
    c = GLA_CHUNK * GLA_CHUNKS_PER_STEP
    assert seq % c == 0
    n = seq // c
    fwd = lambda b, i: i
    bwd = lambda b, i: n - 1 - i
    consts = _gla_constants(False) + _gla_constants(True)
    spec = lambda chunk, w: pl.BlockSpec((1, c, w), lambda b, i: (b, chunk(b, i), 0))
    specs = lambda chunk: [spec(chunk, GLA_KEY_WIDTH)] * 2 + [spec(chunk, GLA_VAL_WIDTH), spec(chunk, GLA_KEY_WIDTH)]
    state = pltpu.VMEM((GLA_HEADS // 2, 2 * GLA_DV, LANES), F32)
    b3 = lambda a: a.reshape(batch, seq, a.shape[-1])
    q, k, v, la_f, la_b = b3(q), b3(k), b3(v), b3(la_f), b3(la_b)
    return pl.pallas_call(
        _gla_kernel,
        grid=(batch, n),
        in_specs=specs(fwd) + specs(bwd) + [_const_spec(a.shape) for a in consts],
        out_specs=[spec(fwd, GLA_VAL_WIDTH), spec(bwd, GLA_VAL_WIDTH)],
        out_shape=[jax.ShapeDtypeStruct((batch, seq, GLA_VAL_WIDTH), BF16)] * 2,
        scratch_shapes=[state, state],
        compiler_params=pltpu.CompilerParams(dimension_semantics=("parallel", "arbitrary"),
                                             vmem_limit_bytes=VMEM_LIMIT),
        name="gla",
    )(q, k, v, la_f, q, k, v, la_b, *consts)


def _tail_kernel(x_ref, nao_ref, of_ref, ob_ref, g_ref, gate_ref, gng_ref, wna_ref, wgla_ref, wout_ref,
                 gm_ref, wup_ref, wdown_ref, gf_ref, y_ref):
    na_o = jnp.concatenate([nao_ref[0, p] for p in range(NA_WIDTH // LANES)], axis=1)
    na_out = _dot(na_o, wna_ref[...])
    o = of_ref[...].astype(F32) + ob_ref[...].astype(F32)
    heads = []
    for h in range(GLA_HEADS):
        heads.append(_rms(o[:, h * GLA_DV:(h + 1) * GLA_DV]) * gng_ref[...])
    g = g_ref[...].astype(F32)
    gla_o = jnp.concatenate(heads, axis=1) * (g * _sigmoid(g))
    gla_out = _dot(gla_o.astype(BF16), wgla_ref[...])
    gate = gate_ref[...].astype(F32)
    merged = gate[:, :D_MODEL] * na_out + gate[:, D_MODEL:] * gla_out
    h = x_ref[...] + _dot(merged.astype(BF16), wout_ref[...])
    u = (_rms(h) * gm_ref[...]).astype(BF16)
    width = D_FF // MLP_SLICES
    for j in range(MLP_SLICES):
        cols = slice(j * width, (j + 1) * width)
        hdn = jnp.square(jnp.maximum(_dot(u, wup_ref[:, cols]), 0.0)).astype(BF16)
        h = h + _dot(hdn, wdown_ref[cols, :])
    y_ref[...] = _rms(h) * gf_ref[...]


def _tail_call(x2, na_o, o_f, o_b, gla_g, gate, gng, wna, wgla, wout, gm, wup, wdown, gf, *, tm):
    n = x2.shape[0]
    tiles = na_o.shape[2] // tm
    row = lambda w: pl.BlockSpec((tm, w), lambda i: (i, 0))
    nao_spec = pl.BlockSpec((1, NA_WIDTH // LANES, tm, LANES), lambda i: (i // tiles, 0, i % tiles, 0))
    consts = (gng, wna, wgla, wout, gm, wup, wdown, gf)
    return pl.pallas_call(
        _tail_kernel,
        grid=(n // tm,),
        in_specs=[row(D_MODEL), nao_spec, row(GLA_VAL_WIDTH), row(GLA_VAL_WIDTH),
                  row(GLA_VAL_WIDTH), row(2 * D_MODEL)] + [_const_spec(c.shape) for c in consts],
        out_specs=row(D_MODEL),
        out_shape=jax.ShapeDtypeStruct((n, D_MODEL), F32),
        compiler_params=pltpu.CompilerParams(dimension_semantics=("parallel",), vmem_limit_bytes=VMEM_LIMIT),
        name="tail",
    )(x2, na_o, o_f, o_b, gla_g, gate, *consts)


def _prepare(norm_mix_g, w_in, b_in, na_rpb, gk_fwd_w, gk_fwd_b, gk_bwd_w, gk_bwd_b, gla_norm_g,
             w_br_na, w_br_gla, w_out, norm_mlp_g, w_up, w_down, norm_final_g):
    edges = np.concatenate([[0], np.cumsum(IN_WIDTHS)])
    wcols = lambda a, b: w_in[:, edges[a]:edges[b]]
    bcols = lambda a, b: b_in[None, edges[a]:edges[b]]
    r = GLA_GATE_RANK
    pad = LR_PAD - 2 * r
    wlr = jnp.pad(wcols(7, 9), ((0, 0), (0, pad)))
    blr = jnp.pad(bcols(7, 9), ((0, 0), (0, pad)))
    gkw = jnp.zeros((LR_PAD, 2 * GLA_KEY_WIDTH), F32)
    gkw = gkw.at[:r, :GLA_KEY_WIDTH].set(gk_fwd_w).at[r:2 * r, GLA_KEY_WIDTH:].set(gk_bwd_w)
    gkb = jnp.concatenate([gk_fwd_b, gk_bwd_b])[None]
    return dict(
        proj=(norm_mix_g[None], wcols(0, 1).astype(BF16), bcols(0, 1), wcols(1, 3).astype(BF16), bcols(1, 3),
              wcols(3, 7).astype(BF16), bcols(3, 7),
              wcols(9, 11).astype(BF16), bcols(9, 11), wlr.astype(BF16), blr, gkw.astype(BF16), gkb),
        na_bias=_na_bias_table(na_rpb),
        tail=(gla_norm_g[None], w_br_na.astype(BF16), w_br_gla.astype(BF16), w_out.astype(BF16),
              norm_mlp_g[None], w_up.astype(BF16), w_down.astype(BF16), norm_final_g[None]),
    )


def _trunk(x, prep, *, tm):
    batch, seq, _ = x.shape
    x2 = x.reshape(batch * seq, D_MODEL)
    na_q, na_kv, gla_q, gla_k, gla_v, gla_g, gate, la_f, la_b = _proj_call(
        x2, *prep["proj"], tm=tm, batch=batch, seq=seq)
    na_o = _na_call(na_q, na_kv, prep["na_bias"], batch=batch, seq=seq)
    o_f, o_b = _gla_call(gla_q, gla_k, gla_v, la_f, la_b, batch=batch, seq=seq)
    y = _tail_call(x2, na_o, o_f.reshape(batch * seq, GLA_VAL_WIDTH), o_b.reshape(batch * seq, GLA_VAL_WIDTH),
                   gla_g, gate, *prep["tail"], tm=tm)
    return y.reshape(batch, seq, D_MODEL)


def kernel(x_prompt, x_sample, norm_mix_g, w_in, b_in, na_rpb, gk_fwd_w, gk_fwd_b, gk_bwd_w, gk_bwd_b, gla_norm_g, w_br_na, w_br_gla, w_out, norm_mlp_g, w_up, w_down, norm_final_g):
    assert norm_mix_g.shape[0] == 1, "single-layer trunk"
    prep = _prepare(norm_mix_g[0], w_in[0], b_in[0], na_rpb[0], gk_fwd_w[0], gk_fwd_b[0], gk_bwd_w[0],
                    gk_bwd_b[0], gla_norm_g[0], w_br_na[0], w_br_gla[0], w_out[0], norm_mlp_g[0], w_up[0],
                    w_down[0], norm_final_g)
    return _trunk(x_prompt, prep, tm=ROW_TILE), _trunk(x_sample, prep, tm=ROW_TILE)
```

```python
import functools

import numpy as np
import jax
import jax.numpy as jnp
from jax import lax
from jax.experimental import pallas as pl
from jax.experimental.pallas import tpu as pltpu

D_MODEL = 1024
GRID_W = 64
NA_HEADS = 8
NA_HEAD_DIM = 64
NA_WIDTH = NA_HEADS * NA_HEAD_DIM
NA_KH = 8
NA_KW = 16
GLA_HEADS = 4
GLA_KEY_WIDTH = 256
GLA_VAL_WIDTH = 512
GLA_DK = 64
GLA_DV = 128
GLA_GATE_RANK = 16
GLA_GATE_NORMALIZER = 16.0
D_FF = 4 * D_MODEL
RMS_EPS = 1e-6
LOG2_E = float(np.log2(np.e))
IN_WIDTHS = (NA_WIDTH, NA_WIDTH, NA_WIDTH, GLA_KEY_WIDTH, GLA_KEY_WIDTH, GLA_VAL_WIDTH, GLA_VAL_WIDTH,
             GLA_GATE_RANK, GLA_GATE_RANK, D_MODEL, D_MODEL)

LANES = 128
BF16_ROWS = 16
GLA_CHUNK = 128
GLA_LEVELS = 7
GLA_CHUNKS_PER_STEP = 8
GLA_FINE_LEVELS = 3
LR_PAD = LANES
NA_COL_BLOCKS = GRID_W // NA_KW
NA_QGROUP = 8
NA_ROWS_PER_STEP = 32
NA_LOOKAHEAD = 6
MLP_SLICES = 4
ROW_TILE = 512
VMEM_LIMIT = 56 * 1024 * 1024

BF16 = jnp.bfloat16
F32 = jnp.float32


def _dot(a, b):
    return jnp.dot(a, b, preferred_element_type=F32)


def _dot_nt(a, b):
    return lax.dot_general(a, b, (((1,), (1,)), ((), ())), preferred_element_type=F32)


def _sigmoid(z):
    return 0.5 * jnp.tanh(0.5 * z) + 0.5


def _rms(x):
    return x * lax.rsqrt(jnp.mean(x * x, axis=-1, keepdims=True) + RMS_EPS)


def _proj_kernel(x_ref, g_ref, wq_ref, bq_ref, wkv_ref, bkv_ref, wgla_ref, bgla_ref, wgate_ref, bgate_ref,
                 wlr_ref, blr_ref, gkw_ref, gkb_ref, gng_ref,
                 q_ref, kv_ref, gq_ref, gk_ref, gv_ref, gg_ref, gate_ref, laf_ref, lab_ref):
    xb = (_rms(x_ref[...]) * g_ref[...]).astype(BF16)
    w = GLA_KEY_WIDTH
    gate_ref[...] = _sigmoid(_dot(xb, wgate_ref[...]) + bgate_ref[...]).astype(BF16)
    lr = (_dot(xb, wlr_ref[...]) + blr_ref[...]).astype(BF16)
    kv = (_dot(xb, wkv_ref[...]) + bkv_ref[...]).astype(BF16)
    kv = kv.reshape(kv.shape[0] // GRID_W, NA_COL_BLOCKS, NA_KW, kv.shape[1])
    for cb in range(NA_COL_BLOCKS):
        block = kv[:, cb].reshape(-1, kv.shape[3])
        for p in range(2 * NA_WIDTH // LANES):
            kv_ref[0, p, cb] = block[:, p * LANES:(p + 1) * LANES]
    z = _dot(lr, gkw_ref[...]) + gkb_ref[...]
    log_a = (jnp.minimum(z, 0.0) - jnp.log(1.0 + jnp.exp(-jnp.abs(z)))) * (LOG2_E / GLA_GATE_NORMALIZER)
    laf_ref[...] = log_a[:, :w]
    lab_ref[...] = log_a[:, w:]
    gla = _dot(xb, wgla_ref[...]) + bgla_ref[...]
    gq_ref[...] = gla[:, :w].astype(BF16)
    gk_ref[...] = gla[:, w:2 * w].astype(BF16)
    gv_ref[...] = gla[:, 2 * w:2 * w + GLA_VAL_WIDTH].astype(BF16)
    g = gla[:, 2 * w + GLA_VAL_WIDTH:]
    gg_ref[...] = (g * _sigmoid(g) * gng_ref[...]).astype(BF16)
    q = (_dot(xb, wq_ref[...]) + bq_ref[...]).astype(BF16)
    for p in range(NA_WIDTH // LANES):
        q_ref[0, p] = q[:, p * LANES:(p + 1) * LANES]


def _const_spec(shape):
    nd = len(shape)
    return pl.BlockSpec(shape, lambda *_: (0,) * nd, pipeline_mode=pl.Buffered(1))


def _proj_call(x2, g, wq, bq, wkv, bkv, wgla, bgla, wgate, bgate, wlr, blr, gkw, gkb, gng, *, tm, batch, seq):
    n = x2.shape[0]
    assert seq % tm == 0 and tm % GRID_W == 0
    tiles = seq // tm
    pairs = NA_WIDTH // LANES
    row = lambda w: pl.BlockSpec((tm, w), lambda i: (i, 0))
    consts = (g, wq, bq, wkv, bkv, wgla, bgla, wgate, bgate, wlr, blr, gkw, gkb, gng)
    q_spec = pl.BlockSpec((1, pairs, tm, LANES), lambda i: (i // tiles, 0, i % tiles, 0))
    kv_spec = pl.BlockSpec((1, 2 * pairs, NA_COL_BLOCKS, tm // NA_COL_BLOCKS, LANES),
                           lambda i: (i // tiles, 0, 0, i % tiles, 0))
    widths = (GLA_KEY_WIDTH, GLA_KEY_WIDTH, GLA_VAL_WIDTH, GLA_VAL_WIDTH, 2 * D_MODEL)
    return pl.pallas_call(
        _proj_kernel,
        grid=(n // tm,),
        in_specs=[row(D_MODEL)] + [_const_spec(c.shape) for c in consts],
        out_specs=[q_spec, kv_spec] + [row(w) for w in widths] + [row(GLA_KEY_WIDTH)] * 2,
        out_shape=[jax.ShapeDtypeStruct((batch, pairs, seq, LANES), BF16),
                   jax.ShapeDtypeStruct((batch, 2 * pairs, NA_COL_BLOCKS, seq // NA_COL_BLOCKS, LANES), BF16)]
        + [jax.ShapeDtypeStruct((n, w), BF16) for w in widths]
        + [jax.ShapeDtypeStruct((n, GLA_KEY_WIDTH), F32)] * 2,
        compiler_params=pltpu.CompilerParams(dimension_semantics=("parallel",), vmem_limit_bytes=VMEM_LIMIT),
        name="proj",
    )(x2, *consts)


def _na_needed_blocks():
    need = []
    for g in range(GRID_W // NA_QGROUP):
        c = np.arange(g * NA_QGROUP, (g + 1) * NA_QGROUP)
        start = np.clip(c - NA_KW // 2, 0, GRID_W - NA_KW)
        need.append(sorted(set((start // NA_KW).tolist()) | set(((start + NA_KW - 1) // NA_KW).tolist())))
    return need


NA_NEED = _na_needed_blocks()
NA_HALF_ROWS = []
for _half in range(2):
    _groups = [g for g, blocks in enumerate(NA_NEED) if any(b // 2 == _half for b in blocks)]
    _lo = (min(_groups) * NA_QGROUP) // BF16_ROWS * BF16_ROWS
    _hi = -(-((max(_groups) + 1) * NA_QGROUP) // BF16_ROWS) * BF16_ROWS
    NA_HALF_ROWS.append((_lo, _hi))


def _na_bias_table(rpb):
    c = np.arange(GRID_W)
    col_start = np.clip(c - NA_KW // 2, 0, GRID_W - NA_KW)
    col_mask = (c[None, :] >= col_start[:, None]) & (c[None, :] < col_start[:, None] + NA_KW)
    dc_idx = np.clip(c[None, :] - c[:, None], -(NA_KW - 1), NA_KW - 1) + (NA_KW - 1)
    onehot = (dc_idx[None] == np.arange(2 * NA_KW - 1)[:, None, None]) & col_mask[None]
    cols = jnp.einsum('hrd,dqk->hrqk', rpb.astype(F32), jnp.asarray(onehot, F32),
                      precision=lax.Precision.HIGHEST)
    cols = jnp.where(col_mask[None, None], cols, -jnp.inf)
    heads = rpb.shape[0]
    cols = cols.reshape(heads, 2 * NA_KH - 1, GRID_W, NA_COL_BLOCKS, NA_KW).transpose(0, 2, 3, 1, 4)
    bias = jnp.stack([cols[:, :, :, NA_KH - 1 - off:2 * NA_KH - 1 - off] for off in range(NA_KH)], axis=1)
    return bias.reshape(heads, NA_KH, GRID_W, NA_KH * GRID_W)


def _na_kernel(q_ref, k_ref, v_ref, bias_ref, o_ref, *, rows):
    lane = lax.broadcasted_iota(jnp.int32, (GRID_W, LANES), 1)
    first = lane < NA_HEAD_DIM
    win = NA_KH * NA_KW
    groups = GRID_W // NA_QGROUP

    def window(r):
        start = jnp.clip(r - NA_KH // 2, 0, rows - NA_KH)
        return pl.ds(pl.multiple_of(start * NA_KW, NA_KW), win), r - start

    def tile(ref, keys, half):
        return jnp.concatenate([ref[0, 0, 2 * half, keys, :], ref[0, 0, 2 * half + 1, keys, :]], axis=0)

    def half_rows(x, half):
        lo, hi = NA_HALF_ROWS[half]
        return jnp.concatenate([x[lo:hi], x[GRID_W + lo:GRID_W + hi]], axis=0)

    def scores(r):
        keys, _ = window(r)
        q = q_ref[0, 0, pl.ds(pl.multiple_of(r * GRID_W, GRID_W), GRID_W), :] * (NA_HEAD_DIM ** -0.5)
        zero = jnp.zeros_like(q)
        qs = jnp.concatenate([jnp.where(first, q, zero), jnp.where(first, zero, q)], axis=0)
        return [_dot_nt(half_rows(qs, half), tile(k_ref, keys, half)) for half in range(2)]

    def attend(r, s):
        keys, off = window(r)
        p_rows = [[], []]
        inv = []
        for head in range(2):
            for g in range(groups):
                rows8 = slice(g * NA_QGROUP, (g + 1) * NA_QGROUP)
                pieces = {}
                for cb in NA_NEED[g]:
                    half = cb // 2
                    lo, hi = NA_HALF_ROWS[half]
                    at = head * (hi - lo) + g * NA_QGROUP - lo
                    pieces[cb] = (s[half][at:at + NA_QGROUP, (cb % 2) * LANES:(cb % 2 + 1) * LANES]
                                  + bias_ref[head, off, rows8, cb * LANES:(cb + 1) * LANES])
                m = functools.reduce(jnp.maximum, pieces.values())
                m = jnp.max(m, axis=-1, keepdims=True)
                pieces = {cb: jnp.exp(x - m) for cb, x in pieces.items()}
                total = functools.reduce(jnp.add, pieces.values())
                inv.append(1.0 / jnp.sum(total, axis=-1, keepdims=True))
                for half in range(2):
                    lo, hi = NA_HALF_ROWS[half]
                    if lo <= g * NA_QGROUP < hi:
                        zero = jnp.zeros((NA_QGROUP, LANES), F32)
                        p_rows[half].append(jnp.concatenate(
                            [pieces.get(2 * half, zero), pieces.get(2 * half + 1, zero)], axis=1))
        outs = []
        for half in range(2):
            p = jnp.concatenate(p_rows[half], axis=0).astype(BF16)
            outs.append(_dot(p, tile(v_ref, keys, half)))
        heads_o = []
        for head in range(2):
            rows_o = []
            for g in range(groups):
                acc = None
                for half in range(2):
                    lo, hi = NA_HALF_ROWS[half]
                    if lo <= g * NA_QGROUP < hi:
                        at = head * (hi - lo) + g * NA_QGROUP - lo
                        part = outs[half][at:at + NA_QGROUP]
                        acc = part if acc is None else acc + part
                rows_o.append(acc * inv[head * groups + g])
            heads_o.append(jnp.concatenate(rows_o, axis=0))
        o_ref[0, 0, pl.ds(pl.multiple_of(r * GRID_W, GRID_W), GRID_W), :] = jnp.where(
            first, heads_o[0], heads_o[1]).astype(BF16)

    def body(i, carry):
        rs = [i * NA_ROWS_PER_STEP + u for u in range(NA_ROWS_PER_STEP)]
        ss = [scores(r) for r in rs[:NA_LOOKAHEAD]]
        for u, r in enumerate(rs):
            if u + NA_LOOKAHEAD < len(rs):
                ss.append(scores(rs[u + NA_LOOKAHEAD]))
            attend(r, ss[u])
        return carry

    lax.fori_loop(0, rows // NA_ROWS_PER_STEP, body, 0)


def _na_call(na_q, na_kv, bias, *, batch, seq):
    rows = seq // GRID_W
    assert rows >= NA_KH and seq % GRID_W == 0 and rows % NA_ROWS_PER_STEP == 0
    pairs = NA_WIDTH // LANES
    q_spec = pl.BlockSpec((1, 1, seq, LANES), lambda p, b: (b, p, 0, 0))
    kv_spec = lambda part: pl.BlockSpec((1, 1, NA_COL_BLOCKS, seq // NA_COL_BLOCKS, LANES),
                                        lambda p, b: (b, part * pairs + p, 0, 0, 0))
    return pl.pallas_call(
        functools.partial(_na_kernel, rows=rows),
        grid=(pairs, batch),
        in_specs=[q_spec, kv_spec(0), kv_spec(1),
                  pl.BlockSpec((2, NA_KH, GRID_W, NA_KH * GRID_W), lambda p, b: (p, 0, 0, 0))],
        out_specs=q_spec,
        out_shape=jax.ShapeDtypeStruct((batch, pairs, seq, LANES), BF16),
        compiler_params=pltpu.CompilerParams(dimension_semantics=("parallel", "arbitrary"),
                                             vmem_limit_bytes=VMEM_LIMIT),
        name="na",
    )(na_q, na_kv, na_kv, bias)


def _gla_constants(reverse):
    c = GLA_CHUNK
    t = np.arange(c)
    flip = (lambda a: a[::-1, ::-1]) if reverse else (lambda a: a)
    sums = [flip((t[None, :] <= t[:, None]).astype(np.float32))]
    fine = [np.eye(c, dtype=np.float32)]
    coarse = []
    for lvl in range(GLA_LEVELS):
        m = 1 << lvl
        base = (t // (2 * m)) * (2 * m)
        upper = (t % (2 * m)) >= m
        s = t[None, :]
        mask = flip((upper[:, None] & (~upper)[None, :] & (base[:, None] == base[None, :])).astype(np.float32))
        if lvl < GLA_FINE_LEVELS:
            q_rows = upper[:, None] & (s >= (base + m)[:, None]) & (s <= t[:, None])
            k_rows = (~upper)[:, None] & (s > t[:, None]) & (s <= (base + m - 1)[:, None])
            sums.append(flip((q_rows | k_rows).astype(np.float32)))
            fine.append(mask)
        else:
            coarse.append(mask[mask.any(axis=1)])
    two_heads = lambda mk: np.concatenate([mk, mk], axis=0)
    return (jnp.asarray(np.concatenate(sums, axis=0), BF16),
            jnp.asarray(np.stack([two_heads(mk) for mk in fine]), F32),
            jnp.asarray(np.stack([two_heads(mk) for mk in coarse]), F32))


def _stack_heads(x):
    first = lax.broadcasted_iota(jnp.int32, x.shape, 1) < GLA_DK
    zero = jnp.zeros_like(x)
    return jnp.concatenate([jnp.where(first, x, zero), jnp.where(first, zero, x)], axis=0)


class _GlaUnit:
    def __init__(self, q_ref, k_ref, v_ref, la_ref, sums_ref, fine_ref, coarse_ref, state_ref, o_ref, reverse,
                 chunk, pair):
        self.refs = (fine_ref, coarse_ref, state_ref, o_ref)
        self.reverse, self.pair = reverse, pair
        c, w = GLA_CHUNK, LANES
        self.rows = rows = slice(chunk * c, (chunk + 1) * c)
        ks = slice(pair * LANES, (pair + 1) * LANES)
        la = la_ref[0, rows, ks]
        la_hi = la.astype(BF16)
        la_lo = (la - la_hi.astype(F32)).astype(BF16)
        sums = _dot(sums_ref[...], jnp.concatenate([la_hi, la_lo], axis=1))
        sums = sums[:, :w] + sums[:, w:]
        cum = sums[:c]
        last_row = 0 if reverse else c - 1
        cum_last = cum[last_row:last_row + 1]
        q = q_ref[0, rows, ks].astype(F32) * (GLA_DK ** -0.5)
        k = k_ref[0, rows, ks].astype(F32)
        self.v = v_ref[0, rows, pair * 2 * GLA_DV:(pair + 1) * 2 * GLA_DV]
        self.q_lvl = [q.astype(BF16)]
        self.k_lvl = [k.astype(BF16)]
        for lvl in range(GLA_FINE_LEVELS):
            e = jnp.exp2(sums[(lvl + 1) * c:(lvl + 2) * c])
            self.q_lvl.append((q * e).astype(BF16))
            self.k_lvl.append((k * e).astype(BF16))
        for lvl in range(GLA_FINE_LEVELS, GLA_LEVELS):
            m = 1 << lvl
            split = lambda x: x.reshape(c // (2 * m), 2 * m, w)
            cum3, q3, k3 = split(cum), split(q), split(k)
            lo, hi = slice(0, m), slice(m, 2 * m)
            qh, kh = (lo, hi) if reverse else (hi, lo)
            ref = cum3[:, m:m + 1] if reverse else cum3[:, m - 1:m]
            self.q_lvl.append((q3[:, qh] * jnp.exp2(cum3[:, qh] - ref)).reshape(c // 2, w).astype(BF16))
            k_scaled = k3[:, kh] * jnp.exp2(ref - cum3[:, kh])
            k_parts = [k3[:, lo], k_scaled] if reverse else [k_scaled, k3[:, hi]]
            self.k_lvl.append(jnp.concatenate(k_parts, axis=1).reshape(c, w).astype(BF16))
        self.q_dec = (q * jnp.exp2(cum)).astype(BF16)
        self.k_dec = (k * jnp.exp2(cum_last - cum)).astype(BF16)
        self.decay = jnp.exp2(cum_last)

    def scores(self):
        fine_ref, coarse_ref, _, _ = self.refs
        c = GLA_CHUNK
        level = lambda lvl: _dot_nt(_stack_heads(self.q_lvl[lvl]), self.k_lvl[lvl])
        scores = fine_ref[0] * level(0)
        for lvl in range(GLA_FINE_LEVELS):
            scores += fine_ref[lvl + 1] * level(lvl + 1)
        for lvl in range(GLA_FINE_LEVELS, GLA_LEVELS):
            m = 1 << lvl
            x = (coarse_ref[lvl - GLA_FINE_LEVELS] * level(lvl + 1)).reshape(2 * c // (2 * m), m, c)
            zero = jnp.zeros_like(x)
            scores += jnp.concatenate([x, zero] if self.reverse else [zero, x], axis=1).reshape(2 * c, c)
        self.s = scores.astype(BF16)

    def output(self):
        _, _, state_ref, o_ref = self.refs
        c, pair = GLA_CHUNK, self.pair
        state = state_ref[pair]
        v_t = self.v.T
        lhs = jnp.concatenate([self.s, _stack_heads(self.q_dec)], axis=1)
        rhs = jnp.concatenate([v_t, state.astype(BF16)], axis=1)
        o = _dot_nt(lhs, rhs)
        o_ref[0, self.rows, pair * 2 * GLA_DV:pair * 2 * GLA_DV + GLA_DV] = o[:c, :GLA_DV].astype(o_ref.dtype)
        o_ref[0, self.rows, pair * 2 * GLA_DV + GLA_DV:(pair + 1) * 2 * GLA_DV] = o[c:, GLA_DV:].astype(o_ref.dtype)
        upd = _dot(v_t, self.k_dec)
        row = lax.broadcasted_iota(jnp.int32, upd.shape, 0)
        col = lax.broadcasted_iota(jnp.int32, upd.shape, 1)
        own = (row < GLA_DV) == (col < GLA_DK)
        state_ref[pair] = state * self.decay + jnp.where(own, upd, 0.0)


def _gla_kernel(qf_ref, kf_ref, vf_ref, laf_ref, qb_ref, kb_ref, vb_ref, lab_ref,
                sums_f_ref, fine_f_ref, coarse_f_ref, sums_b_ref, fine_b_ref, coarse_b_ref,
                of_ref, ob_ref, state_f_ref, state_b_ref):
    @pl.when(pl.program_id(1) == 0)
    def _():
        state_f_ref[...] = jnp.zeros_like(state_f_ref)
        state_b_ref[...] = jnp.zeros_like(state_b_ref)

    fwd = (qf_ref, kf_ref, vf_ref, laf_ref, sums_f_ref, fine_f_ref, coarse_f_ref, state_f_ref, of_ref, False)
    bwd = (qb_ref, kb_ref, vb_ref, lab_ref, sums_b_ref, fine_b_ref, coarse_b_ref, state_b_ref, ob_ref, True)
    order = []
    for j in range(GLA_CHUNKS_PER_STEP):
        for pair in range(GLA_HEADS // 2):
            order.append(fwd + (j, pair))
            order.append(bwd + (GLA_CHUNKS_PER_STEP - 1 - j, pair))
    units = [_GlaUnit(*order[0])]
    for i in range(len(order)):
        if i + 1 < len(order):
            units.append(_GlaUnit(*order[i + 1]))
        units[i].scores()
        if i > 0:
            units[i - 1].output()
    units[-1].output()


def _gla_call(q, k, v, la_f, la_b, *, batch, seq):
    c = GLA_CHUNK * GLA_CHUNKS_PER_STEP
    assert seq % c == 0
    n = seq // c
    fwd = lambda b, i: i
    bwd = lambda b, i: n - 1 - i
    consts = _gla_constants(False) + _gla_constants(True)
    spec = lambda chunk, w: pl.BlockSpec((1, c, w), lambda b, i: (b, chunk(b, i), 0))
    specs = lambda chunk: [spec(chunk, GLA_KEY_WIDTH)] * 2 + [spec(chunk, GLA_VAL_WIDTH), spec(chunk, GLA_KEY_WIDTH)]
    state = pltpu.VMEM((GLA_HEADS // 2, 2 * GLA_DV, LANES), F32)
    b3 = lambda a: a.reshape(batch, seq, a.shape[-1])
    q, k, v, la_f, la_b = b3(q), b3(k), b3(v), b3(la_f), b3(la_b)
    return pl.pallas_call(
        _gla_kernel,
        grid=(batch, n),
        in_specs=specs(fwd) + specs(bwd) + [_const_spec(a.shape) for a in consts],
        out_specs=[spec(fwd, GLA_VAL_WIDTH), spec(bwd, GLA_VAL_WIDTH)],
        out_shape=[jax.ShapeDtypeStruct((batch, seq, GLA_VAL_WIDTH), BF16)] * 2,
        scratch_shapes=[state, state],
        compiler_params=pltpu.CompilerParams(dimension_semantics=("parallel", "arbitrary"),
                                             vmem_limit_bytes=VMEM_LIMIT),
        name="gla",
    )(q, k, v, la_f, q, k, v, la_b, *consts)


def _tail_kernel(x_ref, nao_ref, of_ref, ob_ref, g_ref, gate_ref, wna_ref, wgla_ref, wout_ref,
                 gm_ref, wup_ref, wdown_ref, gf_ref, y_ref):
    na_o = jnp.concatenate([nao_ref[0, p] for p in range(NA_WIDTH // LANES)], axis=1)
    na_out = _dot(na_o, wna_ref[...])
    o = of_ref[...].astype(F32) + ob_ref[...].astype(F32)
    heads = [_rms(o[:, h * GLA_DV:(h + 1) * GLA_DV]) for h in range(GLA_HEADS)]
    gla_o = jnp.concatenate(heads, axis=1) * g_ref[...].astype(F32)
    gla_out = _dot(gla_o.astype(BF16), wgla_ref[...])
    gate = gate_ref[...].astype(F32)
    merged = gate[:, :D_MODEL] * na_out + gate[:, D_MODEL:] * gla_out
    h = x_ref[...] + _dot(merged.astype(BF16), wout_ref[...])
    u = (_rms(h) * gm_ref[...]).astype(BF16)
    width = D_FF // MLP_SLICES
    for j in range(MLP_SLICES):
        cols = slice(j * width, (j + 1) * width)
        hdn = jnp.square(jnp.maximum(_dot(u, wup_ref[:, cols]), 0.0)).astype(BF16)
        h = h + _dot(hdn, wdown_ref[cols, :])
    y_ref[...] = _rms(h) * gf_ref[...]


def _tail_call(x2, na_o, o_f, o_b, gla_g, gate, wna, wgla, wout, gm, wup, wdown, gf, *, tm):
    n = x2.shape[0]
    tiles = na_o.shape[2] // tm
    row = lambda w: pl.BlockSpec((tm, w), lambda i: (i, 0))
    nao_spec = pl.BlockSpec((1, NA_WIDTH // LANES, tm, LANES), lambda i: (i // tiles, 0, i % tiles, 0))
    consts = (wna, wgla, wout, gm, wup, wdown, gf)
    return pl.pallas_call(
        _tail_kernel,
        grid=(n // tm,),
        in_specs=[row(D_MODEL), nao_spec, row(GLA_VAL_WIDTH), row(GLA_VAL_WIDTH),
                  row(GLA_VAL_WIDTH), row(2 * D_MODEL)] + [_const_spec(c.shape) for c in consts],
        out_specs=row(D_MODEL),
        out_shape=jax.ShapeDtypeStruct((n, D_MODEL), F32),
        compiler_params=pltpu.CompilerParams(dimension_semantics=("parallel",), vmem_limit_bytes=VMEM_LIMIT),
        name="tail",
    )(x2, na_o, o_f, o_b, gla_g, gate, *consts)


def _prepare(norm_mix_g, w_in, b_in, na_rpb, gk_fwd_w, gk_fwd_b, gk_bwd_w, gk_bwd_b, gla_norm_g,
             w_br_na, w_br_gla, w_out, norm_mlp_g, w_up, w_down, norm_final_g):
    edges = np.concatenate([[0], np.cumsum(IN_WIDTHS)])
    wcols = lambda a, b: w_in[:, edges[a]:edges[b]]
    bcols = lambda a, b: b_in[None, edges[a]:edges[b]]
    r = GLA_GATE_RANK
    pad = LR_PAD - 2 * r
    wlr = jnp.pad(wcols(7, 9), ((0, 0), (0, pad)))
    blr = jnp.pad(bcols(7, 9), ((0, 0), (0, pad)))
    gkw = jnp.zeros((LR_PAD, 2 * GLA_KEY_WIDTH), F32)
    gkw = gkw.at[:r, :GLA_KEY_WIDTH].set(gk_fwd_w).at[r:2 * r, GLA_KEY_WIDTH:].set(gk_bwd_w)
    gkb = jnp.concatenate([gk_fwd_b, gk_bwd_b])[None]
    return dict(
        proj=(norm_mix_g[None], wcols(0, 1).astype(BF16), bcols(0, 1), wcols(1, 3).astype(BF16), bcols(1, 3),
              wcols(3, 7).astype(BF16), bcols(3, 7),
              wcols(9, 11).astype(BF16), bcols(9, 11), wlr.astype(BF16), blr, gkw.astype(BF16), gkb,
              jnp.tile(gla_norm_g, GLA_HEADS)[None]),
        na_bias=_na_bias_table(na_rpb),
        tail=(w_br_na.astype(BF16), w_br_gla.astype(BF16), w_out.astype(BF16),
              norm_mlp_g[None], w_up.astype(BF16), w_down.astype(BF16), norm_final_g[None]),
    )


def _trunk(x, prep, *, tm):
    batch, seq, _ = x.shape
    x2 = x.reshape(batch * seq, D_MODEL)
    na_q, na_kv, gla_q, gla_k, gla_v, gla_g, gate, la_f, la_b = _proj_call(
        x2, *prep["proj"], tm=tm, batch=batch, seq=seq)
    na_o = _na_call(na_q, na_kv, prep["na_bias"], batch=batch, seq=seq)
    o_f, o_b = _gla_call(gla_q, gla_k, gla_v, la_f, la_b, batch=batch, seq=seq)
    y = _tail_call(x2, na_o, o_f.reshape(batch * seq, GLA_VAL_WIDTH), o_b.reshape(batch * seq, GLA_VAL_WIDTH),
                   gla_g, gate, *prep["tail"], tm=tm)
    return y.reshape(batch, seq, D_MODEL)


def kernel(x_prompt, x_sample, norm_mix_g, w_in, b_in, na_rpb, gk_fwd_w, gk_fwd_b, gk_bwd_w, gk_bwd_b, gla_norm_g, w_br_na, w_br_gla, w_out, norm_mlp_g, w_up, w_down, norm_final_g):
    assert norm_mix_g.shape[0] == 1, "single-layer trunk"
    prep = _prepare(norm_mix_g[0], w_in[0], b_in[0], na_rpb[0], gk_fwd_w[0], gk_fwd_b[0], gk_bwd_w[0],
                    gk_bwd_b[0], gla_norm_g[0], w_br_na[0], w_br_gla[0], w_out[0], norm_mlp_g[0], w_up[0],
                    w_down[0], norm_final_g)
    return _trunk(x_prompt, prep, tm=ROW_TILE), _trunk(x_sample, prep, tm=ROW_TILE)
```

```python
import functools

import numpy as np
import jax
import jax.numpy as jnp
from jax import lax
from jax.experimental import pallas as pl
from jax.experimental.pallas import tpu as pltpu

D_MODEL = 1024
GRID_W = 64
NA_HEADS = 8
NA_HEAD_DIM = 64
NA_WIDTH = NA_HEADS * NA_HEAD_DIM
NA_KH = 8
NA_KW = 16
GLA_HEADS = 4
GLA_KEY_WIDTH = 256
GLA_VAL_WIDTH = 512
GLA_DK = 64
GLA_DV = 128
GLA_GATE_RANK = 16
GLA_GATE_NORMALIZER = 16.0
D_FF = 4 * D_MODEL
RMS_EPS = 1e-6
LOG2_E = float(np.log2(np.e))
IN_WIDTHS = (NA_WIDTH, NA_WIDTH, NA_WIDTH, GLA_KEY_WIDTH, GLA_KEY_WIDTH, GLA_VAL_WIDTH, GLA_VAL_WIDTH,
             GLA_GATE_RANK, GLA_GATE_RANK, D_MODEL, D_MODEL)

LANES = 128
BF16_ROWS = 16
GLA_CHUNK = 128
GLA_LEVELS = 7
GLA_CHUNKS_PER_STEP = 8
GLA_FINE_LEVELS = 3
LR_PAD = LANES
NA_COL_BLOCKS = GRID_W // NA_KW
NA_QGROUP = 8
NA_ROWS_PER_STEP = 32
NA_LOOKAHEAD = 6
MLP_SLICES = 4
ROW_TILE = 512
VMEM_LIMIT = 56 * 1024 * 1024

BF16 = jnp.bfloat16
F32 = jnp.float32


def _dot(a, b):
    return jnp.dot(a, b, preferred_element_type=F32)


def _dot_nt(a, b):
    return lax.dot_general(a, b, (((1,), (1,)), ((), ())), preferred_element_type=F32)


def _sigmoid(z):
    return 0.5 * jnp.tanh(0.5 * z) + 0.5


def _rms(x):
    return x * lax.rsqrt(jnp.mean(x * x, axis=-1, keepdims=True) + RMS_EPS)


def _proj_kernel(x_ref, g_ref, wq_ref, bq_ref, wkv_ref, bkv_ref, wgla_ref, bgla_ref, wgate_ref, bgate_ref,
                 wlr_ref, blr_ref, gkw_ref, gkb_ref, gng_ref,
                 q_ref, kv_ref, gq_ref, gk_ref, gv_ref, gg_ref, gate_ref, laf_ref, lab_ref):
    xb = (_rms(x_ref[...]) * g_ref[...]).astype(BF16)
    w = GLA_KEY_WIDTH
    gate_ref[...] = _sigmoid(_dot(xb, wgate_ref[...]) + bgate_ref[...]).astype(BF16)
    lr = (_dot(xb, wlr_ref[...]) + blr_ref[...]).astype(BF16)
    kv = (_dot(xb, wkv_ref[...]) + bkv_ref[...]).astype(BF16)
    kv = kv.reshape(kv.shape[0] // GRID_W, NA_COL_BLOCKS, NA_KW, kv.shape[1])
    for cb in range(NA_COL_BLOCKS):
        block = kv[:, cb].reshape(-1, kv.shape[3])
        for p in range(2 * NA_WIDTH // LANES):
            kv_ref[0, p, cb] = block[:, p * LANES:(p + 1) * LANES]
    z = _dot(lr, gkw_ref[...]) + gkb_ref[...]
    log_a = (jnp.minimum(z, 0.0) - jnp.log(1.0 + jnp.exp(-jnp.abs(z)))) * (LOG2_E / GLA_GATE_NORMALIZER)
    laf_ref[...] = log_a[:, :w]
    lab_ref[...] = log_a[:, w:]
    gla = _dot(xb, wgla_ref[...]) + bgla_ref[...]
    gq_ref[...] = gla[:, :w].astype(BF16)
    gk_ref[...] = gla[:, w:2 * w].astype(BF16)
    gv_ref[...] = gla[:, 2 * w:2 * w + GLA_VAL_WIDTH].astype(BF16)
    g = gla[:, 2 * w + GLA_VAL_WIDTH:]
    gg_ref[...] = (g * _sigmoid(g) * gng_ref[...]).astype(BF16)
    q = (_dot(xb, wq_ref[...]) + bq_ref[...]).astype(BF16)
    for p in range(NA_WIDTH // LANES):
        q_ref[0, p] = q[:, p * LANES:(p + 1) * LANES]


def _const_spec(shape):
    nd = len(shape)
    return pl.BlockSpec(shape, lambda *_: (0,) * nd, pipeline_mode=pl.Buffered(1))


def _proj_call(x2, g, wq, bq, wkv, bkv, wgla, bgla, wgate, bgate, wlr, blr, gkw, gkb, gng, *, tm, batch, seq):
    n = x2.shape[0]
    assert seq % tm == 0 and tm % GRID_W == 0
    tiles = seq // tm
    pairs = NA_WIDTH // LANES
    row = lambda w: pl.BlockSpec((tm, w), lambda i: (i, 0))
    consts = (g, wq, bq, wkv, bkv, wgla, bgla, wgate, bgate, wlr, blr, gkw, gkb, gng)
    q_spec = pl.BlockSpec((1, pairs, tm, LANES), lambda i: (i // tiles, 0, i % tiles, 0))
    kv_spec = pl.BlockSpec((1, 2 * pairs, NA_COL_BLOCKS, tm // NA_COL_BLOCKS, LANES),
                           lambda i: (i // tiles, 0, 0, i % tiles, 0))
    widths = (GLA_KEY_WIDTH, GLA_KEY_WIDTH, GLA_VAL_WIDTH, GLA_VAL_WIDTH, 2 * D_MODEL)
    return pl.pallas_call(
        _proj_kernel,
        grid=(n // tm,),
        in_specs=[row(D_MODEL)] + [_const_spec(c.shape) for c in consts],
        out_specs=[q_spec, kv_spec] + [row(w) for w in widths] + [row(GLA_KEY_WIDTH)] * 2,
        out_shape=[jax.ShapeDtypeStruct((batch, pairs, seq, LANES), BF16),
                   jax.ShapeDtypeStruct((batch, 2 * pairs, NA_COL_BLOCKS, seq // NA_COL_BLOCKS, LANES), BF16)]
        + [jax.ShapeDtypeStruct((n, w), BF16) for w in widths]
        + [jax.ShapeDtypeStruct((n, GLA_KEY_WIDTH), F32)] * 2,
        compiler_params=pltpu.CompilerParams(dimension_semantics=("parallel",), vmem_limit_bytes=VMEM_LIMIT),
        name="proj",
    )(x2, *consts)


def _na_needed_blocks():
    need = []
    for g in range(GRID_W // NA_QGROUP):
        c = np.arange(g * NA_QGROUP, (g + 1) * NA_QGROUP)
        start = np.clip(c - NA_KW // 2, 0, GRID_W - NA_KW)
        need.append(sorted(set((start // NA_KW).tolist()) | set(((start + NA_KW - 1) // NA_KW).tolist())))
    return need


NA_NEED = _na_needed_blocks()
NA_HALF_ROWS = []
for _half in range(2):
    _groups = [g for g, blocks in enumerate(NA_NEED) if any(b // 2 == _half for b in blocks)]
    _lo = (min(_groups) * NA_QGROUP) // BF16_ROWS * BF16_ROWS
    _hi = -(-((max(_groups) + 1) * NA_QGROUP) // BF16_ROWS) * BF16_ROWS
    NA_HALF_ROWS.append((_lo, _hi))


def _na_bias_table(rpb):
    c = np.arange(GRID_W)
    col_start = np.clip(c - NA_KW // 2, 0, GRID_W - NA_KW)
    col_mask = (c[None, :] >= col_start[:, None]) & (c[None, :] < col_start[:, None] + NA_KW)
    dc_idx = np.clip(c[None, :] - c[:, None], -(NA_KW - 1), NA_KW - 1) + (NA_KW - 1)
    onehot = (dc_idx[None] == np.arange(2 * NA_KW - 1)[:, None, None]) & col_mask[None]
    cols = jnp.einsum('hrd,dqk->hrqk', rpb.astype(F32), jnp.asarray(onehot, F32),
                      precision=lax.Precision.HIGHEST)
    cols = jnp.where(col_mask[None, None], cols, -jnp.inf)
    heads = rpb.shape[0]
    cols = cols.reshape(heads, 2 * NA_KH - 1, GRID_W, NA_COL_BLOCKS, NA_KW).transpose(0, 2, 3, 1, 4)
    bias = jnp.stack([cols[:, :, :, NA_KH - 1 - off:2 * NA_KH - 1 - off] for off in range(NA_KH)], axis=1)
    return bias.reshape(heads, NA_KH, GRID_W, NA_KH * GRID_W)


def _na_kernel(q_ref, k_ref, v_ref, bias_ref, o_ref, *, rows):
    lane = lax.broadcasted_iota(jnp.int32, (GRID_W, LANES), 1)
    first = lane < NA_HEAD_DIM
    win = NA_KH * NA_KW
    groups = GRID_W // NA_QGROUP

    def window(r):
        start = jnp.clip(r - NA_KH // 2, 0, rows - NA_KH)
        return pl.ds(pl.multiple_of(start * NA_KW, NA_KW), win), r - start

    def tile(ref, keys, half):
        return jnp.concatenate([ref[0, 0, 2 * half, keys, :], ref[0, 0, 2 * half + 1, keys, :]], axis=0)

    def half_rows(x, half):
        lo, hi = NA_HALF_ROWS[half]
        return jnp.concatenate([x[lo:hi], x[GRID_W + lo:GRID_W + hi]], axis=0)

    def scores(r):
        keys, _ = window(r)
        q = q_ref[0, 0, pl.ds(pl.multiple_of(r * GRID_W, GRID_W), GRID_W), :] * (NA_HEAD_DIM ** -0.5)
        zero = jnp.zeros_like(q)
        qs = jnp.concatenate([jnp.where(first, q, zero), jnp.where(first, zero, q)], axis=0)
        return [_dot_nt(half_rows(qs, half), tile(k_ref, keys, half)) for half in range(2)]

    def attend(r, s):
        keys, off = window(r)
        p_rows = [[], []]
        inv = []
        for head in range(2):
            for g in range(groups):
                rows8 = slice(g * NA_QGROUP, (g + 1) * NA_QGROUP)
                pieces = {}
                for cb in NA_NEED[g]:
                    half = cb // 2
                    lo, hi = NA_HALF_ROWS[half]
                    at = head * (hi - lo) + g * NA_QGROUP - lo
                    pieces[cb] = (s[half][at:at + NA_QGROUP, (cb % 2) * LANES:(cb % 2 + 1) * LANES]
                                  + bias_ref[head, off, rows8, cb * LANES:(cb + 1) * LANES])
                m = functools.reduce(jnp.maximum, pieces.values())
                m = jnp.max(m, axis=-1, keepdims=True)
                pieces = {cb: jnp.exp(x - m) for cb, x in pieces.items()}
                total = functools.reduce(jnp.add, pieces.values())
                inv.append(1.0 / jnp.sum(total, axis=-1, keepdims=True))
                for half in range(2):
                    lo, hi = NA_HALF_ROWS[half]
                    if lo <= g * NA_QGROUP < hi:
                        zero = jnp.zeros((NA_QGROUP, LANES), F32)
                        p_rows[half].append(jnp.concatenate(
                            [pieces.get(2 * half, zero), pieces.get(2 * half + 1, zero)], axis=1))
        outs = []
        for half in range(2):
            p = jnp.concatenate(p_rows[half], axis=0).astype(BF16)
            outs.append(_dot(p, tile(v_ref, keys, half)))
        heads_o = []
        for head in range(2):
            rows_o = []
            for g in range(groups):
                acc = None
                for half in range(2):
                    lo, hi = NA_HALF_ROWS[half]
                    if lo <= g * NA_QGROUP < hi:
                        at = head * (hi - lo) + g * NA_QGROUP - lo
                        part = outs[half][at:at + NA_QGROUP]
                        acc = part if acc is None else acc + part
                rows_o.append(acc * inv[head * groups + g])
            heads_o.append(jnp.concatenate(rows_o, axis=0))
        o_ref[0, 0, pl.ds(pl.multiple_of(r * GRID_W, GRID_W), GRID_W), :] = jnp.where(
            first, heads_o[0], heads_o[1]).astype(BF16)

    def body(i, carry):
        rs = [i * NA_ROWS_PER_STEP + u for u in range(NA_ROWS_PER_STEP)]
        ss = [scores(r) for r in rs[:NA_LOOKAHEAD]]
        for u, r in enumerate(rs):
            if u + NA_LOOKAHEAD < len(rs):
                ss.append(scores(rs[u + NA_LOOKAHEAD]))
            attend(r, ss[u])
        return carry

    lax.fori_loop(0, rows // NA_ROWS_PER_STEP, body, 0)


def _na_call(na_q, na_kv, bias, *, batch, seq):
    rows = seq // GRID_W
    assert rows >= NA_KH and seq % GRID_W == 0 and rows % NA_ROWS_PER_STEP == 0
    pairs = NA_WIDTH // LANES
    q_spec = pl.BlockSpec((1, 1, seq, LANES), lambda p, b: (b, p, 0, 0))
    kv_spec = lambda part: pl.BlockSpec((1, 1, NA_COL_BLOCKS, seq // NA_COL_BLOCKS, LANES),
                                        lambda p, b: (b, part * pairs + p, 0, 0, 0))
    return pl.pallas_call(
        functools.partial(_na_kernel, rows=rows),
        grid=(pairs, batch),
        in_specs=[q_spec, kv_spec(0), kv_spec(1),
                  pl.BlockSpec((2, NA_KH, GRID_W, NA_KH * GRID_W), lambda p, b: (p, 0, 0, 0))],
        out_specs=q_spec,
        out_shape=jax.ShapeDtypeStruct((batch, pairs, seq, LANES), BF16),
        compiler_params=pltpu.CompilerParams(dimension_semantics=("parallel", "arbitrary"),
                                             vmem_limit_bytes=VMEM_LIMIT),
        name="na",
    )(na_q, na_kv, na_kv, bias)


def _gla_constants(reverse):
    c = GLA_CHUNK
    t = np.arange(c)
    flip = (lambda a: a[::-1, ::-1]) if reverse else (lambda a: a)
    sums = [flip((t[None, :] <= t[:, None]).astype(np.float32))]
    fine = [np.eye(c, dtype=np.float32)]
    coarse = []
    for lvl in range(GLA_LEVELS):
        m = 1 << lvl
        base = (t // (2 * m)) * (2 * m)
        upper = (t % (2 * m)) >= m
        s = t[None, :]
        mask = flip((upper[:, None] & (~upper)[None, :] & (base[:, None] == base[None, :])).astype(np.float32))
        if lvl < GLA_FINE_LEVELS:
            q_rows = upper[:, None] & (s >= (base + m)[:, None]) & (s <= t[:, None])
            k_rows = (~upper)[:, None] & (s > t[:, None]) & (s <= (base + m - 1)[:, None])
            sums.append(flip((q_rows | k_rows).astype(np.float32)))
            fine.append(mask)
        else:
            coarse.append(mask[mask.any(axis=1)])
    two_heads = lambda mk: np.concatenate([mk, mk], axis=0)
    return (jnp.asarray(np.concatenate(sums, axis=0), BF16),
            jnp.asarray(np.stack([two_heads(mk) for mk in fine]), F32),
            jnp.asarray(np.stack([two_heads(mk) for mk in coarse]), F32))


def _stack_heads(x):
    first = lax.broadcasted_iota(jnp.int32, x.shape, 1) < GLA_DK
    zero = jnp.zeros_like(x)
    return jnp.concatenate([jnp.where(first, x, zero), jnp.where(first, zero, x)], axis=0)


class _GlaUnit:
    def __init__(self, q_ref, k_ref, v_ref, la_ref, sums_ref, fine_ref, coarse_ref, state_ref, o_ref, reverse,
                 chunk, pair):
        self.refs = (fine_ref, coarse_ref, state_ref, o_ref)
        self.reverse, self.pair = reverse, pair
        c, w = GLA_CHUNK, LANES
        self.rows = rows = slice(chunk * c, (chunk + 1) * c)
        ks = slice(pair * LANES, (pair + 1) * LANES)
        la = la_ref[0, rows, ks]
        la_hi = la.astype(BF16)
        la_lo = (la - la_hi.astype(F32)).astype(BF16)
        sums = _dot(sums_ref[...], jnp.concatenate([la_hi, la_lo], axis=1))
        sums = sums[:, :w] + sums[:, w:]
        cum = sums[:c]
        last_row = 0 if reverse else c - 1
        cum_last = cum[last_row:last_row + 1]
        q = q_ref[0, rows, ks].astype(F32) * (GLA_DK ** -0.5)
        k = k_ref[0, rows, ks].astype(F32)
        self.v = v_ref[0, rows, pair * 2 * GLA_DV:(pair + 1) * 2 * GLA_DV]
        self.q_lvl = [q.astype(BF16)]
        self.k_lvl = [k.astype(BF16)]
        for lvl in range(GLA_FINE_LEVELS):
            e = jnp.exp2(sums[(lvl + 1) * c:(lvl + 2) * c])
            self.q_lvl.append((q * e).astype(BF16))
            self.k_lvl.append((k * e).astype(BF16))
        for lvl in range(GLA_FINE_LEVELS, GLA_LEVELS):
            m = 1 << lvl
            split = lambda x: x.reshape(c // (2 * m), 2 * m, w)
            cum3, q3, k3 = split(cum), split(q), split(k)
            lo, hi = slice(0, m), slice(m, 2 * m)
            qh, kh = (lo, hi) if reverse else (hi, lo)
            ref = cum3[:, m:m + 1] if reverse else cum3[:, m - 1:m]
            self.q_lvl.append((q3[:, qh] * jnp.exp2(cum3[:, qh] - ref)).reshape(c // 2, w).astype(BF16))
            k_scaled = k3[:, kh] * jnp.exp2(ref - cum3[:, kh])
            k_parts = [k3[:, lo], k_scaled] if reverse else [k_scaled, k3[:, hi]]
            self.k_lvl.append(jnp.concatenate(k_parts, axis=1).reshape(c, w).astype(BF16))
        self.q_dec = (q * jnp.exp2(cum)).astype(BF16)
        self.k_dec = (k * jnp.exp2(cum_last - cum)).astype(BF16)
        self.decay = jnp.exp2(cum_last)

    def scores(self):
        fine_ref, coarse_ref, _, _ = self.refs
        c = GLA_CHUNK
        level = lambda lvl: _dot_nt(_stack_heads(self.q_lvl[lvl]), self.k_lvl[lvl])
        scores = fine_ref[0] * level(0)
        for lvl in range(GLA_FINE_LEVELS):
            scores += fine_ref[lvl + 1] * level(lvl + 1)
        for lvl in range(GLA_FINE_LEVELS, GLA_LEVELS):
            m = 1 << lvl
            x = (coarse_ref[lvl - GLA_FINE_LEVELS] * level(lvl + 1)).reshape(2 * c // (2 * m), m, c)
            zero = jnp.zeros_like(x)
            scores += jnp.concatenate([x, zero] if self.reverse else [zero, x], axis=1).reshape(2 * c, c)
        self.s = scores.astype(BF16)

    def output(self):
        _, _, state_ref, o_ref = self.refs
        c, pair = GLA_CHUNK, self.pair
        state = state_ref[pair]
        v_t = self.v.T
        lhs = jnp.concatenate([self.s, _stack_heads(self.q_dec)], axis=1)
        rhs = jnp.concatenate([v_t, state.astype(BF16)], axis=1)
        o = _dot_nt(lhs, rhs)
        o_ref[0, self.rows, pair * 2 * GLA_DV:pair * 2 * GLA_DV + GLA_DV] = o[:c, :GLA_DV].astype(o_ref.dtype)
        o_ref[0, self.rows, pair * 2 * GLA_DV + GLA_DV:(pair + 1) * 2 * GLA_DV] = o[c:, GLA_DV:].astype(o_ref.dtype)
        upd = _dot(v_t, self.k_dec)
        row = lax.broadcasted_iota(jnp.int32, upd.shape, 0)
        col = lax.broadcasted_iota(jnp.int32, upd.shape, 1)
        own = (row < GLA_DV) == (col < GLA_DK)
        state_ref[pair] = state * self.decay + jnp.where(own, upd, 0.0)


def _gla_kernel(qf_ref, kf_ref, vf_ref, laf_ref, qb_ref, kb_ref, vb_ref, lab_ref,
                sums_f_ref, fine_f_ref, coarse_f_ref, sums_b_ref, fine_b_ref, coarse_b_ref,
                of_ref, ob_ref, state_f_ref, state_b_ref):
    @pl.when(pl.program_id(1) == 0)
    def _():
        state_f_ref[...] = jnp.zeros_like(state_f_ref)
        state_b_ref[...] = jnp.zeros_like(state_b_ref)

    fwd = (qf_ref, kf_ref, vf_ref, laf_ref, sums_f_ref, fine_f_ref, coarse_f_ref, state_f_ref, of_ref, False)
    bwd = (qb_ref, kb_ref, vb_ref, lab_ref, sums_b_ref, fine_b_ref, coarse_b_ref, state_b_ref, ob_ref, True)
    order = []
    for j in range(GLA_CHUNKS_PER_STEP):
        for pair in range(GLA_HEADS // 2):
            order.append(fwd + (j, pair))
            order.append(bwd + (GLA_CHUNKS_PER_STEP - 1 - j, pair))
    units = [_GlaUnit(*order[0])]
    for i in range(len(order)):
        if i + 1 < len(order):
            units.append(_GlaUnit(*order[i + 1]))
        units[i].scores()
        if i > 0:
            units[i - 1].output()
    units[-1].output()


def _gla_call(q, k, v, la_f, la_b, *, batch, seq):
    c = GLA_CHUNK * GLA_CHUNKS_PER_STEP
    assert seq % c == 0
    n = seq // c
    fwd = lambda b, i: i
    bwd = lambda b, i: n - 1 - i
    consts = _gla_constants(False) + _gla_constants(True)
    spec = lambda chunk, w: pl.BlockSpec((1, c, w), lambda b, i: (b, chunk(b, i), 0))
    specs = lambda chunk: [spec(chunk, GLA_KEY_WIDTH)] * 2 + [spec(chunk, GLA_VAL_WIDTH), spec(chunk, GLA_KEY_WIDTH)]
    state = pltpu.VMEM((GLA_HEADS // 2, 2 * GLA_DV, LANES), F32)
    b3 = lambda a: a.reshape(batch, seq, a.shape[-1])
    q, k, v, la_f, la_b = b3(q), b3(k), b3(v), b3(la_f), b3(la_b)
    return pl.pallas_call(
        _gla_kernel,
        grid=(batch, n),
        in_specs=specs(fwd) + specs(bwd) + [_const_spec(a.shape) for a in consts],
        out_specs=[spec(fwd, GLA_VAL_WIDTH), spec(bwd, GLA_VAL_WIDTH)],
        out_shape=[jax.ShapeDtypeStruct((batch, seq, GLA_VAL_WIDTH), BF16)] * 2,
        scratch_shapes=[state, state],
        compiler_params=pltpu.CompilerParams(dimension_semantics=("parallel", "arbitrary"),
                                             vmem_limit_bytes=VMEM_LIMIT),
        name="gla",
    )(q, k, v, la_f, q, k, v, la_b, *consts)


def _tail_kernel(x_ref, nao_ref, of_ref, ob_ref, g_ref, gate_ref, wna_ref, wgla_ref, wout_ref,
                 gm_ref, wup_ref, wdown_ref, gf_ref, y_ref, h_ref):
    step, last = pl.program_id(0), pl.num_programs(0) - 1
    width = D_FF // MLP_SLICES

    def body(finish_previous):
        if finish_previous:
            h_prev = h_ref[...]
        na_o = jnp.concatenate([nao_ref[0, p] for p in range(NA_WIDTH // LANES)], axis=1)
        na_out = _dot(na_o, wna_ref[...])
        o = of_ref[...].astype(F32) + ob_ref[...].astype(F32)
        heads = [_rms(o[:, hd * GLA_DV:(hd + 1) * GLA_DV]) for hd in range(GLA_HEADS)]
        gla_o = jnp.concatenate(heads, axis=1) * g_ref[...].astype(F32)
        gla_out = _dot(gla_o.astype(BF16), wgla_ref[...])
        gate = gate_ref[...].astype(F32)
        merged = gate[:, :D_MODEL] * na_out + gate[:, D_MODEL:] * gla_out
        h = x_ref[...] + _dot(merged.astype(BF16), wout_ref[...])
        u = (_rms(h) * gm_ref[...]).astype(BF16)
        for j in range(MLP_SLICES):
            cols = slice(j * width, (j + 1) * width)
            hdn = jnp.square(jnp.maximum(_dot(u, wup_ref[:, cols]), 0.0)).astype(BF16)
            h = h + _dot(hdn, wdown_ref[cols, :])
            if j == 0 and finish_previous:
                y_ref[...] = _rms(h_prev) * gf_ref[...]
        h_ref[...] = h

    def only_finish():
        y_ref[...] = _rms(h_ref[...]) * gf_ref[...]

    pl.when(step == 0)(lambda: body(False))
    pl.when(jnp.logical_and(step > 0, step < last))(lambda: body(True))
    pl.when(step == last)(only_finish)


def _tail_call(x2, na_o, o_f, o_b, gla_g, gate, wna, wgla, wout, gm, wup, wdown, gf, *, tm):
    n = x2.shape[0]
    steps = n // tm
    tiles = na_o.shape[2] // tm
    tile = lambda i: jnp.minimum(i, steps - 1)
    row = lambda w: pl.BlockSpec((tm, w), lambda i: (tile(i), 0))
    nao_spec = pl.BlockSpec((1, NA_WIDTH // LANES, tm, LANES),
                            lambda i: (tile(i) // tiles, 0, tile(i) % tiles, 0))
    consts = (wna, wgla, wout, gm, wup, wdown, gf)
    return pl.pallas_call(
        _tail_kernel,
        grid=(steps + 1,),
        in_specs=[row(D_MODEL), nao_spec, row(GLA_VAL_WIDTH), row(GLA_VAL_WIDTH),
                  row(GLA_VAL_WIDTH), row(2 * D_MODEL)] + [_const_spec(c.shape) for c in consts],
        out_specs=pl.BlockSpec((tm, D_MODEL), lambda i: (jnp.maximum(i - 1, 0), 0)),
        out_shape=jax.ShapeDtypeStruct((n, D_MODEL), F32),
        scratch_shapes=[pltpu.VMEM((tm, D_MODEL), F32)],
        compiler_params=pltpu.CompilerParams(dimension_semantics=("arbitrary",), vmem_limit_bytes=VMEM_LIMIT),
        name="tail",
    )(x2, na_o, o_f, o_b, gla_g, gate, *consts)


def _prepare(norm_mix_g, w_in, b_in, na_rpb, gk_fwd_w, gk_fwd_b, gk_bwd_w, gk_bwd_b, gla_norm_g,
             w_br_na, w_br_gla, w_out, norm_mlp_g, w_up, w_down, norm_final_g):
    edges = np.concatenate([[0], np.cumsum(IN_WIDTHS)])
    wcols = lambda a, b: w_in[:, edges[a]:edges[b]]
    bcols = lambda a, b: b_in[None, edges[a]:edges[b]]
    r = GLA_GATE_RANK
    pad = LR_PAD - 2 * r
    wlr = jnp.pad(wcols(7, 9), ((0, 0), (0, pad)))
    blr = jnp.pad(bcols(7, 9), ((0, 0), (0, pad)))
    gkw = jnp.zeros((LR_PAD, 2 * GLA_KEY_WIDTH), F32)
    gkw = gkw.at[:r, :GLA_KEY_WIDTH].set(gk_fwd_w).at[r:2 * r, GLA_KEY_WIDTH:].set(gk_bwd_w)
    gkb = jnp.concatenate([gk_fwd_b, gk_bwd_b])[None]
    return dict(
        proj=(norm_mix_g[None], wcols(0, 1).astype(BF16), bcols(0, 1), wcols(1, 3).astype(BF16), bcols(1, 3),
              wcols(3, 7).astype(BF16), bcols(3, 7),
              wcols(9, 11).astype(BF16), bcols(9, 11), wlr.astype(BF16), blr, gkw.astype(BF16), gkb,
              jnp.tile(gla_norm_g, GLA_HEADS)[None]),
        na_bias=_na_bias_table(na_rpb),
        tail=(w_br_na.astype(BF16), w_br_gla.astype(BF16), w_out.astype(BF16),
              norm_mlp_g[None], w_up.astype(BF16), w_down.astype(BF16), norm_final_g[None]),
    )


def _trunk(x, prep, *, tm):
    batch, seq, _ = x.shape
    x2 = x.reshape(batch * seq, D_MODEL)
    na_q, na_kv, gla_q, gla_k, gla_v, gla_g, gate, la_f, la_b = _proj_call(
        x2, *prep["proj"], tm=tm, batch=batch, seq=seq)
    na_o = _na_call(na_q, na_kv, prep["na_bias"], batch=batch, seq=seq)
    o_f, o_b = _gla_call(gla_q, gla_k, gla_v, la_f, la_b, batch=batch, seq=seq)
    y = _tail_call(x2, na_o, o_f.reshape(batch * seq, GLA_VAL_WIDTH), o_b.reshape(batch * seq, GLA_VAL_WIDTH),
                   gla_g, gate, *prep["tail"], tm=tm)
    return y.reshape(batch, seq, D_MODEL)


def kernel(x_prompt, x_sample, norm_mix_g, w_in, b_in, na_rpb, gk_fwd_w, gk_fwd_b, gk_bwd_w, gk_bwd_b, gla_norm_g, w_br_na, w_br_gla, w_out, norm_mlp_g, w_up, w_down, norm_final_g):
    assert norm_mix_g.shape[0] == 1, "single-layer trunk"
    prep = _prepare(norm_mix_g[0], w_in[0], b_in[0], na_rpb[0], gk_fwd_w[0], gk_fwd_b[0], gk_bwd_w[0],
                    gk_bwd_b[0], gla_norm_g[0], w_br_na[0], w_br_gla[0], w_out[0], norm_mlp_g[0], w_up[0],
                    w_down[0], norm_final_g)
    return _trunk(x_prompt, prep, tm=ROW_TILE), _trunk(x_sample, prep, tm=ROW_TILE)
```

```python
import functools

import numpy as np
import jax
import jax.numpy as jnp
from jax import lax
from jax.experimental import pallas as pl
from jax.experimental.pallas import tpu as pltpu

D_MODEL = 1024
GRID_W = 64
NA_HEADS = 8
NA_HEAD_DIM = 64
NA_WIDTH = NA_HEADS * NA_HEAD_DIM
NA_KH = 8
NA_KW = 16
GLA_HEADS = 4
GLA_KEY_WIDTH = 256
GLA_VAL_WIDTH = 512
GLA_DK = 64
GLA_DV = 128
GLA_GATE_RANK = 16
GLA_GATE_NORMALIZER = 16.0
D_FF = 4 * D_MODEL
RMS_EPS = 1e-6
LOG2_E = float(np.log2(np.e))
IN_WIDTHS = (NA_WIDTH, NA_WIDTH, NA_WIDTH, GLA_KEY_WIDTH, GLA_KEY_WIDTH, GLA_VAL_WIDTH, GLA_VAL_WIDTH,
             GLA_GATE_RANK, GLA_GATE_RANK, D_MODEL, D_MODEL)

LANES = 128
BF16_ROWS = 16
GLA_CHUNK = 128
GLA_LEVELS = 7
GLA_CHUNKS_PER_STEP = 8
GLA_FINE_LEVELS = 3
LR_PAD = LANES
NA_COL_BLOCKS = GRID_W // NA_KW
NA_QGROUP = 8
NA_ROWS_PER_STEP = 32
NA_LOOKAHEAD = 6
MLP_SLICES = 4
ROW_TILE = 512
VMEM_LIMIT = 56 * 1024 * 1024

BF16 = jnp.bfloat16
F32 = jnp.float32


def _dot(a, b):
    return jnp.dot(a, b, preferred_element_type=F32)


def _dot_nt(a, b):
    return lax.dot_general(a, b, (((1,), (1,)), ((), ())), preferred_element_type=F32)


def _sigmoid(z):
    return 0.5 * jnp.tanh(0.5 * z) + 0.5


def _rms(x):
    return x * lax.rsqrt(jnp.mean(x * x, axis=-1, keepdims=True) + RMS_EPS)


def _proj_kernel(x_ref, g_ref, wq_ref, bq_ref, wkv_ref, bkv_ref, wgla_ref, bgla_ref, wgate_ref, bgate_ref,
                 wlr_ref, blr_ref, gkw_ref, gkb_ref, gng_ref,
                 q_ref, kv_ref, gq_ref, gk_ref, gv_ref, gg_ref, gate_ref, laf_ref, lab_ref):
    xb = (_rms(x_ref[...]) * g_ref[...]).astype(BF16)
    w = GLA_KEY_WIDTH
    gate_ref[...] = _sigmoid(_dot(xb, wgate_ref[...]) + bgate_ref[...]).astype(BF16)
    lr = (_dot(xb, wlr_ref[...]) + blr_ref[...]).astype(BF16)
    kv = (_dot(xb, wkv_ref[...]) + bkv_ref[...]).astype(BF16)
    kv = kv.reshape(kv.shape[0] // GRID_W, NA_COL_BLOCKS, NA_KW, kv.shape[1])
    for cb in range(NA_COL_BLOCKS):
        block = kv[:, cb].reshape(-1, kv.shape[3])
        for p in range(2 * NA_WIDTH // LANES):
            kv_ref[0, p, cb] = block[:, p * LANES:(p + 1) * LANES]
    z = _dot(lr, gkw_ref[...]) + gkb_ref[...]
    log_a = (jnp.minimum(z, 0.0) - jnp.log(1.0 + jnp.exp(-jnp.abs(z)))) * (LOG2_E / GLA_GATE_NORMALIZER)
    laf_ref[...] = log_a[:, :w]
    lab_ref[...] = log_a[:, w:]
    gla = _dot(xb, wgla_ref[...]) + bgla_ref[...]
    gq_ref[...] = gla[:, :w].astype(BF16)
    gk_ref[...] = gla[:, w:2 * w].astype(BF16)
    gv_ref[...] = gla[:, 2 * w:2 * w + GLA_VAL_WIDTH].astype(BF16)
    g = gla[:, 2 * w + GLA_VAL_WIDTH:]
    gg_ref[...] = (g * _sigmoid(g) * gng_ref[...]).astype(BF16)
    q = (_dot(xb, wq_ref[...]) + bq_ref[...]).astype(BF16)
    for p in range(NA_WIDTH // LANES):
        q_ref[0, p] = q[:, p * LANES:(p + 1) * LANES]


def _const_spec(shape):
    nd = len(shape)
    return pl.BlockSpec(shape, lambda *_: (0,) * nd, pipeline_mode=pl.Buffered(1))


def _proj_call(x2, g, wq, bq, wkv, bkv, wgla, bgla, wgate, bgate, wlr, blr, gkw, gkb, gng, *, tm, batch, seq):
    n = x2.shape[0]
    assert seq % tm == 0 and tm % GRID_W == 0
    tiles = seq // tm
    pairs = NA_WIDTH // LANES
    row = lambda w: pl.BlockSpec((tm, w), lambda i: (i, 0))
    consts = (g, wq, bq, wkv, bkv, wgla, bgla, wgate, bgate, wlr, blr, gkw, gkb, gng)
    q_spec = pl.BlockSpec((1, pairs, tm, LANES), lambda i: (i // tiles, 0, i % tiles, 0))
    kv_spec = pl.BlockSpec((1, 2 * pairs, NA_COL_BLOCKS, tm // NA_COL_BLOCKS, LANES),
                           lambda i: (i // tiles, 0, 0, i % tiles, 0))
    widths = (GLA_KEY_WIDTH, GLA_KEY_WIDTH, GLA_VAL_WIDTH, GLA_VAL_WIDTH, 2 * D_MODEL)
    return pl.pallas_call(
        _proj_kernel,
        grid=(n // tm,),
        in_specs=[row(D_MODEL)] + [_const_spec(c.shape) for c in consts],
        out_specs=[q_spec, kv_spec] + [row(w) for w in widths] + [row(GLA_KEY_WIDTH)] * 2,
        out_shape=[jax.ShapeDtypeStruct((batch, pairs, seq, LANES), BF16),
                   jax.ShapeDtypeStruct((batch, 2 * pairs, NA_COL_BLOCKS, seq // NA_COL_BLOCKS, LANES), BF16)]
        + [jax.ShapeDtypeStruct((n, w), BF16) for w in widths]
        + [jax.ShapeDtypeStruct((n, GLA_KEY_WIDTH), F32)] * 2,
        compiler_params=pltpu.CompilerParams(dimension_semantics=("parallel",), vmem_limit_bytes=VMEM_LIMIT),
        name="proj",
    )(x2, *consts)


def _na_needed_blocks():
    need = []
    for g in range(GRID_W // NA_QGROUP):
        c = np.arange(g * NA_QGROUP, (g + 1) * NA_QGROUP)
        start = np.clip(c - NA_KW // 2, 0, GRID_W - NA_KW)
        need.append(sorted(set((start // NA_KW).tolist()) | set(((start + NA_KW - 1) // NA_KW).tolist())))
    return need


NA_NEED = _na_needed_blocks()
NA_HALF_ROWS = []
for _half in range(2):
    _groups = [g for g, blocks in enumerate(NA_NEED) if any(b // 2 == _half for b in blocks)]
    _lo = (min(_groups) * NA_QGROUP) // BF16_ROWS * BF16_ROWS
    _hi = -(-((max(_groups) + 1) * NA_QGROUP) // BF16_ROWS) * BF16_ROWS
    NA_HALF_ROWS.append((_lo, _hi))


def _na_bias_table(rpb):
    c = np.arange(GRID_W)
    col_start = np.clip(c - NA_KW // 2, 0, GRID_W - NA_KW)
    col_mask = (c[None, :] >= col_start[:, None]) & (c[None, :] < col_start[:, None] + NA_KW)
    dc_idx = np.clip(c[None, :] - c[:, None], -(NA_KW - 1), NA_KW - 1) + (NA_KW - 1)
    onehot = (dc_idx[None] == np.arange(2 * NA_KW - 1)[:, None, None]) & col_mask[None]
    cols = jnp.einsum('hrd,dqk->hrqk', rpb.astype(F32), jnp.asarray(onehot, F32),
                      precision=lax.Precision.HIGHEST)
    cols = jnp.where(col_mask[None, None], cols, -jnp.inf)
    heads = rpb.shape[0]
    cols = cols.reshape(heads, 2 * NA_KH - 1, GRID_W, NA_COL_BLOCKS, NA_KW).transpose(0, 2, 3, 1, 4)
    bias = jnp.stack([cols[:, :, :, NA_KH - 1 - off:2 * NA_KH - 1 - off] for off in range(NA_KH)], axis=1)
    return bias.reshape(heads, NA_KH, GRID_W, NA_KH * GRID_W)


def _na_kernel(q_ref, k_ref, v_ref, bias_ref, o_ref, *, rows):
    lane = lax.broadcasted_iota(jnp.int32, (GRID_W, LANES), 1)
    first = lane < NA_HEAD_DIM
    win = NA_KH * NA_KW
    groups = GRID_W // NA_QGROUP

    def window(r):
        start = jnp.clip(r - NA_KH // 2, 0, rows - NA_KH)
        return pl.ds(pl.multiple_of(start * NA_KW, NA_KW), win), r - start

    def tile(ref, keys, half):
        return jnp.concatenate([ref[0, 0, 2 * half, keys, :], ref[0, 0, 2 * half + 1, keys, :]], axis=0)

    def half_rows(x, half):
        lo, hi = NA_HALF_ROWS[half]
        return jnp.concatenate([x[lo:hi], x[GRID_W + lo:GRID_W + hi]], axis=0)

    def scores(r):
        keys, _ = window(r)
        q = q_ref[0, 0, pl.ds(pl.multiple_of(r * GRID_W, GRID_W), GRID_W), :] * (NA_HEAD_DIM ** -0.5)
        zero = jnp.zeros_like(q)
        qs = jnp.concatenate([jnp.where(first, q, zero), jnp.where(first, zero, q)], axis=0)
        return [_dot_nt(half_rows(qs, half), tile(k_ref, keys, half)) for half in range(2)]

    def attend(r, s):
        keys, off = window(r)
        p_rows = [[], []]
        inv = []
        for head in range(2):
            for g in range(groups):
                rows8 = slice(g * NA_QGROUP, (g + 1) * NA_QGROUP)
                pieces = {}
                for cb in NA_NEED[g]:
                    half = cb // 2
                    lo, hi = NA_HALF_ROWS[half]
                    at = head * (hi - lo) + g * NA_QGROUP - lo
                    pieces[cb] = (s[half][at:at + NA_QGROUP, (cb % 2) * LANES:(cb % 2 + 1) * LANES]
                                  + bias_ref[head, off, rows8, cb * LANES:(cb + 1) * LANES])
                m = functools.reduce(jnp.maximum, pieces.values())
                m = jnp.max(m, axis=-1, keepdims=True)
                pieces = {cb: jnp.exp(x - m) for cb, x in pieces.items()}
                total = functools.reduce(jnp.add, pieces.values())
                inv.append(1.0 / jnp.sum(total, axis=-1, keepdims=True))
                for half in range(2):
                    lo, hi = NA_HALF_ROWS[half]
                    if lo <= g * NA_QGROUP < hi:
                        zero = jnp.zeros((NA_QGROUP, LANES), F32)
                        p_rows[half].append(jnp.concatenate(
                            [pieces.get(2 * half, zero), pieces.get(2 * half + 1, zero)], axis=1))
        outs = []
        for half in range(2):
            p = jnp.concatenate(p_rows[half], axis=0).astype(BF16)
            outs.append(_dot(p, tile(v_ref, keys, half)))
        heads_o = []
        for head in range(2):
            rows_o = []
            for g in range(groups):
                acc = None
                for half in range(2):
                    lo, hi = NA_HALF_ROWS[half]
                    if lo <= g * NA_QGROUP < hi:
                        at = head * (hi - lo) + g * NA_QGROUP - lo
                        part = outs[half][at:at + NA_QGROUP]
                        acc = part if acc is None else acc + part
                rows_o.append(acc * inv[head * groups + g])
            heads_o.append(jnp.concatenate(rows_o, axis=0))
        o_ref[0, 0, pl.ds(pl.multiple_of(r * GRID_W, GRID_W), GRID_W), :] = jnp.where(
            first, heads_o[0], heads_o[1]).astype(BF16)

    def body(i, carry):
        rs = [i * NA_ROWS_PER_STEP + u for u in range(NA_ROWS_PER_STEP)]
        ss = [scores(r) for r in rs[:NA_LOOKAHEAD]]
        for u, r in enumerate(rs):
            if u + NA_LOOKAHEAD < len(rs):
                ss.append(scores(rs[u + NA_LOOKAHEAD]))
            attend(r, ss[u])
        return carry

    lax.fori_loop(0, rows // NA_ROWS_PER_STEP, body, 0)


def _na_call(na_q, na_kv, bias, *, batch, seq):
    rows = seq // GRID_W
    assert rows >= NA_KH and seq % GRID_W == 0 and rows % NA_ROWS_PER_STEP == 0
    pairs = NA_WIDTH // LANES
    q_spec = pl.BlockSpec((1, 1, seq, LANES), lambda p, b: (b, p, 0, 0))
    kv_spec = lambda part: pl.BlockSpec((1, 1, NA_COL_BLOCKS, seq // NA_COL_BLOCKS, LANES),
                                        lambda p, b: (b, part * pairs + p, 0, 0, 0))
    return pl.pallas_call(
        functools.partial(_na_kernel, rows=rows),
        grid=(pairs, batch),
        in_specs=[q_spec, kv_spec(0), kv_spec(1),
                  pl.BlockSpec((2, NA_KH, GRID_W, NA_KH * GRID_W), lambda p, b: (p, 0, 0, 0))],
        out_specs=q_spec,
        out_shape=jax.ShapeDtypeStruct((batch, pairs, seq, LANES), BF16),
        compiler_params=pltpu.CompilerParams(dimension_semantics=("parallel", "arbitrary"),
                                             vmem_limit_bytes=VMEM_LIMIT),
        name="na",
    )(na_q, na_kv, na_kv, bias)


def _gla_constants(reverse):
    c = GLA_CHUNK
    t = np.arange(c)
    flip = (lambda a: a[::-1, ::-1]) if reverse else (lambda a: a)
    sums = [flip((t[None, :] <= t[:, None]).astype(np.float32))]
    fine = [np.eye(c, dtype=np.float32)]
    coarse = []
    for lvl in range(GLA_LEVELS):
        m = 1 << lvl
        base = (t // (2 * m)) * (2 * m)
        upper = (t % (2 * m)) >= m
        s = t[None, :]
        mask = flip((upper[:, None] & (~upper)[None, :] & (base[:, None] == base[None, :])).astype(np.float32))
        if lvl < GLA_FINE_LEVELS:
            q_rows = upper[:, None] & (s >= (base + m)[:, None]) & (s <= t[:, None])
            k_rows = (~upper)[:, None] & (s > t[:, None]) & (s <= (base + m - 1)[:, None])
            sums.append(flip((q_rows | k_rows).astype(np.float32)))
            fine.append(mask)
        else:
            coarse.append(mask[mask.any(axis=1)])
    two_heads = lambda mk: np.concatenate([mk, mk], axis=0)
    return (jnp.asarray(np.concatenate(sums, axis=0), BF16),
            jnp.asarray(np.stack([two_heads(mk) for mk in fine]), F32),
            jnp.asarray(np.stack([two_heads(mk) for mk in coarse]), F32))


def _stack_heads(x):
    first = lax.broadcasted_iota(jnp.int32, x.shape, 1) < GLA_DK
    zero = jnp.zeros_like(x)
    return jnp.concatenate([jnp.where(first, x, zero), jnp.where(first, zero, x)], axis=0)


class _GlaUnit:
    def __init__(self, q_ref, k_ref, v_ref, la_ref, sums_ref, fine_ref, coarse_ref, state_ref, o_ref, reverse,
                 chunk, pair):
        self.refs = (fine_ref, coarse_ref, state_ref, o_ref)
        self.reverse, self.pair = reverse, pair
        c, w = GLA_CHUNK, LANES
        self.rows = rows = slice(chunk * c, (chunk + 1) * c)
        ks = slice(pair * LANES, (pair + 1) * LANES)
        la = la_ref[0, rows, ks]
        la_hi = la.astype(BF16)
        la_lo = (la - la_hi.astype(F32)).astype(BF16)
        sums = _dot(sums_ref[...], jnp.concatenate([la_hi, la_lo], axis=1))
        sums = sums[:, :w] + sums[:, w:]
        cum = sums[:c]
        last_row = 0 if reverse else c - 1
        cum_last = cum[last_row:last_row + 1]
        q = q_ref[0, rows, ks].astype(F32) * (GLA_DK ** -0.5)
        k = k_ref[0, rows, ks].astype(F32)
        self.v = v_ref[0, rows, pair * 2 * GLA_DV:(pair + 1) * 2 * GLA_DV]
        self.q_lvl = [q.astype(BF16)]
        self.k_lvl = [k.astype(BF16)]
        for lvl in range(GLA_FINE_LEVELS):
            e = jnp.exp2(sums[(lvl + 1) * c:(lvl + 2) * c])
            self.q_lvl.append((q * e).astype(BF16))
            self.k_lvl.append((k * e).astype(BF16))
        for lvl in range(GLA_FINE_LEVELS, GLA_LEVELS):
            m = 1 << lvl
            split = lambda x: x.reshape(c // (2 * m), 2 * m, w)
            cum3, q3, k3 = split(cum), split(q), split(k)
            lo, hi = slice(0, m), slice(m, 2 * m)
            qh, kh = (lo, hi) if reverse else (hi, lo)
            ref = cum3[:, m:m + 1] if reverse else cum3[:, m - 1:m]
            self.q_lvl.append((q3[:, qh] * jnp.exp2(cum3[:, qh] - ref)).reshape(c // 2, w).astype(BF16))
            k_scaled = k3[:, kh] * jnp.exp2(ref - cum3[:, kh])
            k_parts = [k3[:, lo], k_scaled] if reverse else [k_scaled, k3[:, hi]]
            self.k_lvl.append(jnp.concatenate(k_parts, axis=1).reshape(c, w).astype(BF16))
        self.q_dec = (q * jnp.exp2(cum)).astype(BF16)
        self.k_dec = (k * jnp.exp2(cum_last - cum)).astype(BF16)
        self.decay = jnp.exp2(cum_last)

    def scores(self):
        fine_ref, coarse_ref, _, _ = self.refs
        c = GLA_CHUNK
        level = lambda lvl: _dot_nt(_stack_heads(self.q_lvl[lvl]), self.k_lvl[lvl])
        scores = fine_ref[0] * level(0)
        for lvl in range(GLA_FINE_LEVELS):
            scores += fine_ref[lvl + 1] * level(lvl + 1)
        for lvl in range(GLA_FINE_LEVELS, GLA_LEVELS):
            m = 1 << lvl
            x = (coarse_ref[lvl - GLA_FINE_LEVELS] * level(lvl + 1)).reshape(2 * c // (2 * m), m, c)
            zero = jnp.zeros_like(x)
            scores += jnp.concatenate([x, zero] if self.reverse else [zero, x], axis=1).reshape(2 * c, c)
        self.s = scores.astype(BF16)

    def output(self):
        _, _, state_ref, o_ref = self.refs
        c, pair = GLA_CHUNK, self.pair
        state = state_ref[pair]
        v_t = self.v.T
        lhs = jnp.concatenate([self.s, _stack_heads(self.q_dec)], axis=1)
        rhs = jnp.concatenate([v_t, state.astype(BF16)], axis=1)
        o = _dot_nt(lhs, rhs)
        o_ref[0, self.rows, pair * 2 * GLA_DV:pair * 2 * GLA_DV + GLA_DV] = o[:c, :GLA_DV].astype(o_ref.dtype)
        o_ref[0, self.rows, pair * 2 * GLA_DV + GLA_DV:(pair + 1) * 2 * GLA_DV] = o[c:, GLA_DV:].astype(o_ref.dtype)
        upd = _dot(v_t, self.k_dec)
        row = lax.broadcasted_iota(jnp.int32, upd.shape, 0)
        col = lax.broadcasted_iota(jnp.int32, upd.shape, 1)
        own = (row < GLA_DV) == (col < GLA_DK)
        state_ref[pair] = state * self.decay + jnp.where(own, upd, 0.0)


def _gla_kernel(qf_ref, kf_ref, vf_ref, laf_ref, qb_ref, kb_ref, vb_ref, lab_ref,
                sums_f_ref, fine_f_ref, coarse_f_ref, sums_b_ref, fine_b_ref, coarse_b_ref,
                of_ref, ob_ref, state_f_ref, state_b_ref):
    @pl.when(pl.program_id(1) == 0)
    def _():
        state_f_ref[...] = jnp.zeros_like(state_f_ref)
        state_b_ref[...] = jnp.zeros_like(state_b_ref)

    fwd = (qf_ref, kf_ref, vf_ref, laf_ref, sums_f_ref, fine_f_ref, coarse_f_ref, state_f_ref, of_ref, False)
    bwd = (qb_ref, kb_ref, vb_ref, lab_ref, sums_b_ref, fine_b_ref, coarse_b_ref, state_b_ref, ob_ref, True)
    order = []
    for j in range(GLA_CHUNKS_PER_STEP):
        for pair in range(GLA_HEADS // 2):
            order.append(fwd + (j, pair))
            order.append(bwd + (GLA_CHUNKS_PER_STEP - 1 - j, pair))
    units = [_GlaUnit(*order[0])]
    for i in range(len(order)):
        if i + 1 < len(order):
            units.append(_GlaUnit(*order[i + 1]))
        units[i].scores()
        if i > 0:
            units[i - 1].output()
    units[-1].output()


def _gla_call(q, k, v, la_f, la_b, *, batch, seq):
    c = GLA_CHUNK * GLA_CHUNKS_PER_STEP
    assert seq % c == 0
    n = seq // c
    fwd = lambda b, i: i
    bwd = lambda b, i: n - 1 - i
    consts = _gla_constants(False) + _gla_constants(True)
    spec = lambda chunk, w: pl.BlockSpec((1, c, w), lambda b, i: (b, chunk(b, i), 0))
    specs = lambda chunk: [spec(chunk, GLA_KEY_WIDTH)] * 2 + [spec(chunk, GLA_VAL_WIDTH), spec(chunk, GLA_KEY_WIDTH)]
    state = pltpu.VMEM((GLA_HEADS // 2, 2 * GLA_DV, LANES), F32)
    b3 = lambda a: a.reshape(batch, seq, a.shape[-1])
    q, k, v, la_f, la_b = b3(q), b3(k), b3(v), b3(la_f), b3(la_b)
    return pl.pallas_call(
        _gla_kernel,
        grid=(batch, n),
        in_specs=specs(fwd) + specs(bwd) + [_const_spec(a.shape) for a in consts],
        out_specs=[spec(fwd, GLA_VAL_WIDTH), spec(bwd, GLA_VAL_WIDTH)],
        out_shape=[jax.ShapeDtypeStruct((batch, seq, GLA_VAL_WIDTH), BF16)] * 2,
        scratch_shapes=[state, state],
        compiler_params=pltpu.CompilerParams(dimension_semantics=("parallel", "arbitrary"),
                                             vmem_limit_bytes=VMEM_LIMIT),
        name="gla",
    )(q, k, v, la_f, q, k, v, la_b, *consts)


def _tail_kernel(x_ref, nao_ref, of_ref, ob_ref, g_ref, gate_ref, wna_ref, wgla_ref, wout_ref,
                 gm_ref, wup_ref, wdown_ref, gf_ref, y_ref):
    na_o = jnp.concatenate([nao_ref[0, p] for p in range(NA_WIDTH // LANES)], axis=1)
    na_out = _dot(na_o, wna_ref[...])
    o = of_ref[...].astype(F32) + ob_ref[...].astype(F32)
    heads = [_rms(o[:, h * GLA_DV:(h + 1) * GLA_DV]) for h in range(GLA_HEADS)]
    gla_o = jnp.concatenate(heads, axis=1) * g_ref[...].astype(F32)
    gla_out = _dot(gla_o.astype(BF16), wgla_ref[...])
    merged = gate_ref[:, :D_MODEL] * na_out.astype(BF16) + gate_ref[:, D_MODEL:] * gla_out.astype(BF16)
    h = x_ref[...] + _dot(merged, wout_ref[...])
    u = (_rms(h) * gm_ref[...]).astype(BF16)
    width = D_FF // MLP_SLICES
    for j in range(MLP_SLICES):
        cols = slice(j * width, (j + 1) * width)
        hdn = jnp.square(jnp.maximum(_dot(u, wup_ref[:, cols]), 0.0)).astype(BF16)
        h = h + _dot(hdn, wdown_ref[cols, :])
    y_ref[...] = _rms(h) * gf_ref[...]


def _tail_call(x2, na_o, o_f, o_b, gla_g, gate, wna, wgla, wout, gm, wup, wdown, gf, *, tm):
    n = x2.shape[0]
    tiles = na_o.shape[2] // tm
    row = lambda w: pl.BlockSpec((tm, w), lambda i: (i, 0))
    nao_spec = pl.BlockSpec((1, NA_WIDTH // LANES, tm, LANES), lambda i: (i // tiles, 0, i % tiles, 0))
    consts = (wna, wgla, wout, gm, wup, wdown, gf)
    return pl.pallas_call(
        _tail_kernel,
        grid=(n // tm,),
        in_specs=[row(D_MODEL), nao_spec, row(GLA_VAL_WIDTH), row(GLA_VAL_WIDTH),
                  row(GLA_VAL_WIDTH), row(2 * D_MODEL)] + [_const_spec(c.shape) for c in consts],
        out_specs=row(D_MODEL),
        out_shape=jax.ShapeDtypeStruct((n, D_MODEL), F32),
        compiler_params=pltpu.CompilerParams(dimension_semantics=("parallel",), vmem_limit_bytes=VMEM_LIMIT),
        name="tail",
    )(x2, na_o, o_f, o_b, gla_g, gate, *consts)


def _prepare(norm_mix_g, w_in, b_in, na_rpb, gk_fwd_w, gk_fwd_b, gk_bwd_w, gk_bwd_b, gla_norm_g,
             w_br_na, w_br_gla, w_out, norm_mlp_g, w_up, w_down, norm_final_g):
    edges = np.concatenate([[0], np.cumsum(IN_WIDTHS)])
    wcols = lambda a, b: w_in[:, edges[a]:edges[b]]
    bcols = lambda a, b: b_in[None, edges[a]:edges[b]]
    r = GLA_GATE_RANK
    pad = LR_PAD - 2 * r
    wlr = jnp.pad(wcols(7, 9), ((0, 0), (0, pad)))
    blr = jnp.pad(bcols(7, 9), ((0, 0), (0, pad)))
    gkw = jnp.zeros((LR_PAD, 2 * GLA_KEY_WIDTH), F32)
    gkw = gkw.at[:r, :GLA_KEY_WIDTH].set(gk_fwd_w).at[r:2 * r, GLA_KEY_WIDTH:].set(gk_bwd_w)
    gkb = jnp.concatenate([gk_fwd_b, gk_bwd_b])[None]
    return dict(
        proj=(norm_mix_g[None], wcols(0, 1).astype(BF16), bcols(0, 1), wcols(1, 3).astype(BF16), bcols(1, 3),
              wcols(3, 7).astype(BF16), bcols(3, 7),
              wcols(9, 11).astype(BF16), bcols(9, 11), wlr.astype(BF16), blr, gkw.astype(BF16), gkb,
              jnp.tile(gla_norm_g, GLA_HEADS)[None]),
        na_bias=_na_bias_table(na_rpb),
        tail=(w_br_na.astype(BF16), w_br_gla.astype(BF16), w_out.astype(BF16),
              norm_mlp_g[None], w_up.astype(BF16), w_down.astype(BF16), norm_final_g[None]),
    )


def _trunk(x, prep, *, tm):
    batch, seq, _ = x.shape
    x2 = x.reshape(batch * seq, D_MODEL)
    na_q, na_kv, gla_q, gla_k, gla_v, gla_g, gate, la_f, la_b = _proj_call(
        x2, *prep["proj"], tm=tm, batch=batch, seq=seq)
    na_o = _na_call(na_q, na_kv, prep["na_bias"], batch=batch, seq=seq)
    o_f, o_b = _gla_call(gla_q, gla_k, gla_v, la_f, la_b, batch=batch, seq=seq)
    y = _tail_call(x2, na_o, o_f.reshape(batch * seq, GLA_VAL_WIDTH), o_b.reshape(batch * seq, GLA_VAL_WIDTH),
                   gla_g, gate, *prep["tail"], tm=tm)
    return y.reshape(batch, seq, D_MODEL)


def kernel(x_prompt, x_sample, norm_mix_g, w_in, b_in, na_rpb, gk_fwd_w, gk_fwd_b, gk_bwd_w, gk_bwd_b, gla_norm_g, w_br_na, w_br_gla, w_out, norm_mlp_g, w_up, w_down, norm_final_g):
    assert norm_mix_g.shape[0] == 1, "single-layer trunk"
    prep = _prepare(norm_mix_g[0], w_in[0], b_in[0], na_rpb[0], gk_fwd_w[0], gk_fwd_b[0], gk_bwd_w[0],
                    gk_bwd_b[0], gla_norm_g[0], w_br_na[0], w_br_gla[0], w_out[0], norm_mlp_g[0], w_up[0],
                    w_down[0], norm_final_g)
    return _trunk(x_prompt, prep, tm=ROW_TILE), _trunk(x_sample, prep, tm=ROW_TILE)
```

```python
import functools

import numpy as np
import jax
import jax.numpy as jnp
from jax import lax
from jax.experimental import pallas as pl
from jax.experimental.pallas import tpu as pltpu

D_MODEL = 1024
GRID_W = 64
NA_HEADS = 8
NA_HEAD_DIM = 64
NA_WIDTH = NA_HEADS * NA_HEAD_DIM
NA_KH = 8
NA_KW = 16
GLA_HEADS = 4
GLA_KEY_WIDTH = 256
GLA_VAL_WIDTH = 512
GLA_DK = 64
GLA_DV = 128
GLA_GATE_RANK = 16
GLA_GATE_NORMALIZER = 16.0
D_FF = 4 * D_MODEL
RMS_EPS = 1e-6
LOG2_E = float(np.log2(np.e))
IN_WIDTHS = (NA_WIDTH, NA_WIDTH, NA_WIDTH, GLA_KEY_WIDTH, GLA_KEY_WIDTH, GLA_VAL_WIDTH, GLA_VAL_WIDTH,
             GLA_GATE_RANK, GLA_GATE_RANK, D_MODEL, D_MODEL)

LANES = 128
BF16_ROWS = 16
GLA_CHUNK = 128
GLA_LEVELS = 7
GLA_CHUNKS_PER_STEP = 8
GLA_FINE_LEVELS = 3
LR_PAD = LANES
NA_COL_BLOCKS = GRID_W // NA_KW
NA_QGROUP = 8
NA_ROWS_PER_STEP = 32
NA_LOOKAHEAD = 6
TAIL_SUBTILES = 2
MLP_SLICES = 4
ROW_TILE = 512
VMEM_LIMIT = 56 * 1024 * 1024

BF16 = jnp.bfloat16
F32 = jnp.float32


def _dot(a, b):
    return jnp.dot(a, b, preferred_element_type=F32)


def _dot_nt(a, b):
    return lax.dot_general(a, b, (((1,), (1,)), ((), ())), preferred_element_type=F32)


def _sigmoid(z):
    return 0.5 * jnp.tanh(0.5 * z) + 0.5


def _rms(x):
    return x * lax.rsqrt(jnp.mean(x * x, axis=-1, keepdims=True) + RMS_EPS)


def _proj_kernel(x_ref, g_ref, wq_ref, bq_ref, wkv_ref, bkv_ref, wgla_ref, bgla_ref, wgate_ref, bgate_ref,
                 wlr_ref, blr_ref, gkw_ref, gkb_ref, gng_ref,
                 q_ref, kv_ref, gq_ref, gk_ref, gv_ref, gg_ref, gate_ref, laf_ref, lab_ref):
    xb = (_rms(x_ref[...]) * g_ref[...]).astype(BF16)
    w = GLA_KEY_WIDTH
    gate_ref[...] = _sigmoid(_dot(xb, wgate_ref[...]) + bgate_ref[...]).astype(BF16)
    lr = (_dot(xb, wlr_ref[...]) + blr_ref[...]).astype(BF16)
    kv = (_dot(xb, wkv_ref[...]) + bkv_ref[...]).astype(BF16)
    kv = kv.reshape(kv.shape[0] // GRID_W, NA_COL_BLOCKS, NA_KW, kv.shape[1])
    for cb in range(NA_COL_BLOCKS):
        block = kv[:, cb].reshape(-1, kv.shape[3])
        for p in range(2 * NA_WIDTH // LANES):
            kv_ref[0, p, cb] = block[:, p * LANES:(p + 1) * LANES]
    z = _dot(lr, gkw_ref[...]) + gkb_ref[...]
    log_a = (jnp.minimum(z, 0.0) - jnp.log(1.0 + jnp.exp(-jnp.abs(z)))) * (LOG2_E / GLA_GATE_NORMALIZER)
    laf_ref[...] = log_a[:, :w]
    lab_ref[...] = log_a[:, w:]
    gla = _dot(xb, wgla_ref[...]) + bgla_ref[...]
    gq_ref[...] = gla[:, :w].astype(BF16)
    gk_ref[...] = gla[:, w:2 * w].astype(BF16)
    gv_ref[...] = gla[:, 2 * w:2 * w + GLA_VAL_WIDTH].astype(BF16)
    g = gla[:, 2 * w + GLA_VAL_WIDTH:]
    gg_ref[...] = (g * _sigmoid(g) * gng_ref[...]).astype(BF16)
    q = (_dot(xb, wq_ref[...]) + bq_ref[...]).astype(BF16)
    for p in range(NA_WIDTH // LANES):
        q_ref[0, p] = q[:, p * LANES:(p + 1) * LANES]


def _const_spec(shape):
    nd = len(shape)
    return pl.BlockSpec(shape, lambda *_: (0,) * nd, pipeline_mode=pl.Buffered(1))


def _proj_call(x2, g, wq, bq, wkv, bkv, wgla, bgla, wgate, bgate, wlr, blr, gkw, gkb, gng, *, tm, batch, seq):
    n = x2.shape[0]
    assert seq % tm == 0 and tm % GRID_W == 0
    tiles = seq // tm
    pairs = NA_WIDTH // LANES
    row = lambda w: pl.BlockSpec((tm, w), lambda i: (i, 0))
    consts = (g, wq, bq, wkv, bkv, wgla, bgla, wgate, bgate, wlr, blr, gkw, gkb, gng)
    q_spec = pl.BlockSpec((1, pairs, tm, LANES), lambda i: (i // tiles, 0, i % tiles, 0))
    kv_spec = pl.BlockSpec((1, 2 * pairs, NA_COL_BLOCKS, tm // NA_COL_BLOCKS, LANES),
                           lambda i: (i // tiles, 0, 0, i % tiles, 0))
    widths = (GLA_KEY_WIDTH, GLA_KEY_WIDTH, GLA_VAL_WIDTH, GLA_VAL_WIDTH, 2 * D_MODEL)
    return pl.pallas_call(
        _proj_kernel,
        grid=(n // tm,),
        in_specs=[row(D_MODEL)] + [_const_spec(c.shape) for c in consts],
        out_specs=[q_spec, kv_spec] + [row(w) for w in widths] + [row(GLA_KEY_WIDTH)] * 2,
        out_shape=[jax.ShapeDtypeStruct((batch, pairs, seq, LANES), BF16),
                   jax.ShapeDtypeStruct((batch, 2 * pairs, NA_COL_BLOCKS, seq // NA_COL_BLOCKS, LANES), BF16)]
        + [jax.ShapeDtypeStruct((n, w), BF16) for w in widths]
        + [jax.ShapeDtypeStruct((n, GLA_KEY_WIDTH), F32)] * 2,
        compiler_params=pltpu.CompilerParams(dimension_semantics=("parallel",), vmem_limit_bytes=VMEM_LIMIT),
        name="proj",
    )(x2, *consts)


def _na_needed_blocks():
    need = []
    for g in range(GRID_W // NA_QGROUP):
        c = np.arange(g * NA_QGROUP, (g + 1) * NA_QGROUP)
        start = np.clip(c - NA_KW // 2, 0, GRID_W - NA_KW)
        need.append(sorted(set((start // NA_KW).tolist()) | set(((start + NA_KW - 1) // NA_KW).tolist())))
    return need


NA_NEED = _na_needed_blocks()
NA_HALF_ROWS = []
for _half in range(2):
    _groups = [g for g, blocks in enumerate(NA_NEED) if any(b // 2 == _half for b in blocks)]
    _lo = (min(_groups) * NA_QGROUP) // BF16_ROWS * BF16_ROWS
    _hi = -(-((max(_groups) + 1) * NA_QGROUP) // BF16_ROWS) * BF16_ROWS
    NA_HALF_ROWS.append((_lo, _hi))


def _na_bias_table(rpb):
    c = np.arange(GRID_W)
    col_start = np.clip(c - NA_KW // 2, 0, GRID_W - NA_KW)
    col_mask = (c[None, :] >= col_start[:, None]) & (c[None, :] < col_start[:, None] + NA_KW)
    dc_idx = np.clip(c[None, :] - c[:, None], -(NA_KW - 1), NA_KW - 1) + (NA_KW - 1)
    onehot = (dc_idx[None] == np.arange(2 * NA_KW - 1)[:, None, None]) & col_mask[None]
    cols = jnp.einsum('hrd,dqk->hrqk', rpb.astype(F32), jnp.asarray(onehot, F32),
                      precision=lax.Precision.HIGHEST)
    cols = jnp.where(col_mask[None, None], cols, -jnp.inf)
    heads = rpb.shape[0]
    cols = cols.reshape(heads, 2 * NA_KH - 1, GRID_W, NA_COL_BLOCKS, NA_KW).transpose(0, 2, 3, 1, 4)
    bias = jnp.stack([cols[:, :, :, NA_KH - 1 - off:2 * NA_KH - 1 - off] for off in range(NA_KH)], axis=1)
    return bias.reshape(heads, NA_KH, GRID_W, NA_KH * GRID_W)


def _na_kernel(q_ref, k_ref, v_ref, bias_ref, o_ref, *, rows):
    lane = lax.broadcasted_iota(jnp.int32, (GRID_W, LANES), 1)
    first = lane < NA_HEAD_DIM
    win = NA_KH * NA_KW
    groups = GRID_W // NA_QGROUP

    def window(r):
        start = jnp.clip(r - NA_KH // 2, 0, rows - NA_KH)
        return pl.ds(pl.multiple_of(start * NA_KW, NA_KW), win), r - start

    def tile(ref, keys, half):
        return jnp.concatenate([ref[0, 0, 2 * half, keys, :], ref[0, 0, 2 * half + 1, keys, :]], axis=0)

    def half_rows(x, half):
        lo, hi = NA_HALF_ROWS[half]
        return jnp.concatenate([x[lo:hi], x[GRID_W + lo:GRID_W + hi]], axis=0)

    def scores(r):
        keys, _ = window(r)
        q = q_ref[0, 0, pl.ds(pl.multiple_of(r * GRID_W, GRID_W), GRID_W), :] * (NA_HEAD_DIM ** -0.5)
        zero = jnp.zeros_like(q)
        qs = jnp.concatenate([jnp.where(first, q, zero), jnp.where(first, zero, q)], axis=0)
        return [_dot_nt(half_rows(qs, half), tile(k_ref, keys, half)) for half in range(2)]

    def attend(r, s):
        keys, off = window(r)
        p_rows = [[], []]
        inv = []
        for head in range(2):
            for g in range(groups):
                rows8 = slice(g * NA_QGROUP, (g + 1) * NA_QGROUP)
                pieces = {}
                for cb in NA_NEED[g]:
                    half = cb // 2
                    lo, hi = NA_HALF_ROWS[half]
                    at = head * (hi - lo) + g * NA_QGROUP - lo
                    pieces[cb] = (s[half][at:at + NA_QGROUP, (cb % 2) * LANES:(cb % 2 + 1) * LANES]
                                  + bias_ref[head, off, rows8, cb * LANES:(cb + 1) * LANES])
                m = functools.reduce(jnp.maximum, pieces.values())
                m = jnp.max(m, axis=-1, keepdims=True)
                pieces = {cb: jnp.exp(x - m) for cb, x in pieces.items()}
                total = functools.reduce(jnp.add, pieces.values())
                inv.append(1.0 / jnp.sum(total, axis=-1, keepdims=True))
                for half in range(2):
                    lo, hi = NA_HALF_ROWS[half]
                    if lo <= g * NA_QGROUP < hi:
                        zero = jnp.zeros((NA_QGROUP, LANES), F32)
                        p_rows[half].append(jnp.concatenate(
                            [pieces.get(2 * half, zero), pieces.get(2 * half + 1, zero)], axis=1))
        outs = []
        for half in range(2):
            p = jnp.concatenate(p_rows[half], axis=0).astype(BF16)
            outs.append(_dot(p, tile(v_ref, keys, half)))
        heads_o = []
        for head in range(2):
            rows_o = []
            for g in range(groups):
                acc = None
                for half in range(2):
                    lo, hi = NA_HALF_ROWS[half]
                    if lo <= g * NA_QGROUP < hi:
                        at = head * (hi - lo) + g * NA_QGROUP - lo
                        part = outs[half][at:at + NA_QGROUP]
                        acc = part if acc is None else acc + part
                rows_o.append(acc * inv[head * groups + g])
            heads_o.append(jnp.concatenate(rows_o, axis=0))
        o_ref[0, 0, pl.ds(pl.multiple_of(r * GRID_W, GRID_W), GRID_W), :] = jnp.where(
            first, heads_o[0], heads_o[1]).astype(BF16)

    def body(i, carry):
        rs = [i * NA_ROWS_PER_STEP + u for u in range(NA_ROWS_PER_STEP)]
        ss = [scores(r) for r in rs[:NA_LOOKAHEAD]]
        for u, r in enumerate(rs):
            if u + NA_LOOKAHEAD < len(rs):
                ss.append(scores(rs[u + NA_LOOKAHEAD]))
            attend(r, ss[u])
        return carry

    lax.fori_loop(0, rows // NA_ROWS_PER_STEP, body, 0)


def _na_call(na_q, na_kv, bias, *, batch, seq):
    rows = seq // GRID_W
    assert rows >= NA_KH and seq % GRID_W == 0 and rows % NA_ROWS_PER_STEP == 0
    pairs = NA_WIDTH // LANES
    q_spec = pl.BlockSpec((1, 1, seq, LANES), lambda p, b: (b, p, 0, 0))
    kv_spec = lambda part: pl.BlockSpec((1, 1, NA_COL_BLOCKS, seq // NA_COL_BLOCKS, LANES),
                                        lambda p, b: (b, part * pairs + p, 0, 0, 0))
    return pl.pallas_call(
        functools.partial(_na_kernel, rows=rows),
        grid=(pairs, batch),
        in_specs=[q_spec, kv_spec(0), kv_spec(1),
                  pl.BlockSpec((2, NA_KH, GRID_W, NA_KH * GRID_W), lambda p, b: (p, 0, 0, 0))],
        out_specs=q_spec,
        out_shape=jax.ShapeDtypeStruct((batch, pairs, seq, LANES), BF16),
        compiler_params=pltpu.CompilerParams(dimension_semantics=("parallel", "arbitrary"),
                                             vmem_limit_bytes=VMEM_LIMIT),
        name="na",
    )(na_q, na_kv, na_kv, bias)


def _gla_constants(reverse):
    c = GLA_CHUNK
    t = np.arange(c)
    flip = (lambda a: a[::-1, ::-1]) if reverse else (lambda a: a)
    sums = [flip((t[None, :] <= t[:, None]).astype(np.float32))]
    fine = [np.eye(c, dtype=np.float32)]
    coarse = []
    for lvl in range(GLA_LEVELS):
        m = 1 << lvl
        base = (t // (2 * m)) * (2 * m)
        upper = (t % (2 * m)) >= m
        s = t[None, :]
        mask = flip((upper[:, None] & (~upper)[None, :] & (base[:, None] == base[None, :])).astype(np.float32))
        if lvl < GLA_FINE_LEVELS:
            q_rows = upper[:, None] & (s >= (base + m)[:, None]) & (s <= t[:, None])
            k_rows = (~upper)[:, None] & (s > t[:, None]) & (s <= (base + m - 1)[:, None])
            sums.append(flip((q_rows | k_rows).astype(np.float32)))
            fine.append(mask)
        else:
            coarse.append(mask[mask.any(axis=1)])
    two_heads = lambda mk: np.concatenate([mk, mk], axis=0)
    return (jnp.asarray(np.concatenate(sums, axis=0), BF16),
            jnp.asarray(np.stack([two_heads(mk) for mk in fine]), F32),
            jnp.asarray(np.stack([two_heads(mk) for mk in coarse]), F32))


def _stack_heads(x):
    first = lax.broadcasted_iota(jnp.int32, x.shape, 1) < GLA_DK
    zero = jnp.zeros_like(x)
    return jnp.concatenate([jnp.where(first, x, zero), jnp.where(first, zero, x)], axis=0)


class _GlaUnit:
    def __init__(self, q_ref, k_ref, v_ref, la_ref, sums_ref, fine_ref, coarse_ref, state_ref, o_ref, reverse,
                 chunk, pair):
        self.refs = (fine_ref, coarse_ref, state_ref, o_ref)
        self.reverse, self.pair = reverse, pair
        c, w = GLA_CHUNK, LANES
        self.rows = rows = slice(chunk * c, (chunk + 1) * c)
        ks = slice(pair * LANES, (pair + 1) * LANES)
        la = la_ref[0, rows, ks]
        la_hi = la.astype(BF16)
        la_lo = (la - la_hi.astype(F32)).astype(BF16)
        sums = _dot(sums_ref[...], jnp.concatenate([la_hi, la_lo], axis=1))
        sums = sums[:, :w] + sums[:, w:]
        cum = sums[:c]
        last_row = 0 if reverse else c - 1
        cum_last = cum[last_row:last_row + 1]
        q = q_ref[0, rows, ks].astype(F32) * (GLA_DK ** -0.5)
        k = k_ref[0, rows, ks].astype(F32)
        self.v = v_ref[0, rows, pair * 2 * GLA_DV:(pair + 1) * 2 * GLA_DV]
        self.q_lvl = [q.astype(BF16)]
        self.k_lvl = [k.astype(BF16)]
        for lvl in range(GLA_FINE_LEVELS):
            e = jnp.exp2(sums[(lvl + 1) * c:(lvl + 2) * c])
            self.q_lvl.append((q * e).astype(BF16))
            self.k_lvl.append((k * e).astype(BF16))
        for lvl in range(GLA_FINE_LEVELS, GLA_LEVELS):
            m = 1 << lvl
            split = lambda x: x.reshape(c // (2 * m), 2 * m, w)
            cum3, q3, k3 = split(cum), split(q), split(k)
            lo, hi = slice(0, m), slice(m, 2 * m)
            qh, kh = (lo, hi) if reverse else (hi, lo)
            ref = cum3[:, m:m + 1] if reverse else cum3[:, m - 1:m]
            self.q_lvl.append((q3[:, qh] * jnp.exp2(cum3[:, qh] - ref)).reshape(c // 2, w).astype(BF16))
            k_scaled = k3[:, kh] * jnp.exp2(ref - cum3[:, kh])
            k_parts = [k3[:, lo], k_scaled] if reverse else [k_scaled, k3[:, hi]]
            self.k_lvl.append(jnp.concatenate(k_parts, axis=1).reshape(c, w).astype(BF16))
        self.q_dec = (q * jnp.exp2(cum)).astype(BF16)
        self.k_dec = (k * jnp.exp2(cum_last - cum)).astype(BF16)
        self.decay = jnp.exp2(cum_last)

    def scores(self):
        fine_ref, coarse_ref, _, _ = self.refs
        c = GLA_CHUNK
        level = lambda lvl: _dot_nt(_stack_heads(self.q_lvl[lvl]), self.k_lvl[lvl])
        scores = fine_ref[0] * level(0)
        for lvl in range(GLA_FINE_LEVELS):
            scores += fine_ref[lvl + 1] * level(lvl + 1)
        for lvl in range(GLA_FINE_LEVELS, GLA_LEVELS):
            m = 1 << lvl
            x = (coarse_ref[lvl - GLA_FINE_LEVELS] * level(lvl + 1)).reshape(2 * c // (2 * m), m, c)
            zero = jnp.zeros_like(x)
            scores += jnp.concatenate([x, zero] if self.reverse else [zero, x], axis=1).reshape(2 * c, c)
        self.s = scores.astype(BF16)

    def output(self):
        _, _, state_ref, o_ref = self.refs
        c, pair = GLA_CHUNK, self.pair
        state = state_ref[pair]
        v_t = self.v.T
        lhs = jnp.concatenate([self.s, _stack_heads(self.q_dec)], axis=1)
        rhs = jnp.concatenate([v_t, state.astype(BF16)], axis=1)
        o = _dot_nt(lhs, rhs)
        o_ref[0, self.rows, pair * 2 * GLA_DV:pair * 2 * GLA_DV + GLA_DV] = o[:c, :GLA_DV].astype(o_ref.dtype)
        o_ref[0, self.rows, pair * 2 * GLA_DV + GLA_DV:(pair + 1) * 2 * GLA_DV] = o[c:, GLA_DV:].astype(o_ref.dtype)
        upd = _dot(v_t, self.k_dec)
        row = lax.broadcasted_iota(jnp.int32, upd.shape, 0)
        col = lax.broadcasted_iota(jnp.int32, upd.shape, 1)
        own = (row < GLA_DV) == (col < GLA_DK)
        state_ref[pair] = state * self.decay + jnp.where(own, upd, 0.0)


def _gla_kernel(qf_ref, kf_ref, vf_ref, laf_ref, qb_ref, kb_ref, vb_ref, lab_ref,
                sums_f_ref, fine_f_ref, coarse_f_ref, sums_b_ref, fine_b_ref, coarse_b_ref,
                of_ref, ob_ref, state_f_ref, state_b_ref):
    @pl.when(pl.program_id(1) == 0)
    def _():
        state_f_ref[...] = jnp.zeros_like(state_f_ref)
        state_b_ref[...] = jnp.zeros_like(state_b_ref)

    fwd = (qf_ref, kf_ref, vf_ref, laf_ref, sums_f_ref, fine_f_ref, coarse_f_ref, state_f_ref, of_ref, False)
    bwd = (qb_ref, kb_ref, vb_ref, lab_ref, sums_b_ref, fine_b_ref, coarse_b_ref, state_b_ref, ob_ref, True)
    order = []
    for j in range(GLA_CHUNKS_PER_STEP):
        for pair in range(GLA_HEADS // 2):
            order.append(fwd + (j, pair))
            order.append(bwd + (GLA_CHUNKS_PER_STEP - 1 - j, pair))
    units = [_GlaUnit(*order[0])]
    for i in range(len(order)):
        if i + 1 < len(order):
            units.append(_GlaUnit(*order[i + 1]))
        units[i].scores()
        if i > 0:
            units[i - 1].output()
    units[-1].output()


def _gla_call(q, k, v, la_f, la_b, *, batch, seq):
    c = GLA_CHUNK * GLA_CHUNKS_PER_STEP
    assert seq % c == 0
    n = seq // c
    fwd = lambda b, i: i
    bwd = lambda b, i: n - 1 - i
    consts = _gla_constants(False) + _gla_constants(True)
    spec = lambda chunk, w: pl.BlockSpec((1, c, w), lambda b, i: (b, chunk(b, i), 0))
    specs = lambda chunk: [spec(chunk, GLA_KEY_WIDTH)] * 2 + [spec(chunk, GLA_VAL_WIDTH), spec(chunk, GLA_KEY_WIDTH)]
    state = pltpu.VMEM((GLA_HEADS // 2, 2 * GLA_DV, LANES), F32)
    b3 = lambda a: a.reshape(batch, seq, a.shape[-1])
    q, k, v, la_f, la_b = b3(q), b3(k), b3(v), b3(la_f), b3(la_b)
    return pl.pallas_call(
        _gla_kernel,
        grid=(batch, n),
        in_specs=specs(fwd) + specs(bwd) + [_const_spec(a.shape) for a in consts],
        out_specs=[spec(fwd, GLA_VAL_WIDTH), spec(bwd, GLA_VAL_WIDTH)],
        out_shape=[jax.ShapeDtypeStruct((batch, seq, GLA_VAL_WIDTH), BF16)] * 2,
        scratch_shapes=[state, state],
        compiler_params=pltpu.CompilerParams(dimension_semantics=("parallel", "arbitrary"),
                                             vmem_limit_bytes=VMEM_LIMIT),
        name="gla",
    )(q, k, v, la_f, q, k, v, la_b, *consts)


def _tail_kernel(x_ref, nao_ref, of_ref, ob_ref, g_ref, gate_ref, wna_ref, wgla_ref, wout_ref,
                 gm_ref, wup_ref, wdown_ref, gf_ref, y_ref):
    sub = x_ref.shape[0] // TAIL_SUBTILES
    tiles = [slice(i * sub, (i + 1) * sub) for i in range(TAIL_SUBTILES)]
    width = D_FF // MLP_SLICES
    na_out = [_dot(jnp.concatenate([nao_ref[0, p, rows, :] for p in range(NA_WIDTH // LANES)], axis=1),
                   wna_ref[...]) for rows in tiles]
    gla_out = []
    for rows in tiles:
        o = of_ref[rows, :].astype(F32) + ob_ref[rows, :].astype(F32)
        heads = [_rms(o[:, h * GLA_DV:(h + 1) * GLA_DV]) for h in range(GLA_HEADS)]
        gla_o = jnp.concatenate(heads, axis=1) * g_ref[rows, :].astype(F32)
        gla_out.append(_dot(gla_o.astype(BF16), wgla_ref[...]))
    hs = []
    for rows, a, b in zip(tiles, na_out, gla_out):
        merged = gate_ref[rows, :D_MODEL] * a.astype(BF16) + gate_ref[rows, D_MODEL:] * b.astype(BF16)
        hs.append(x_ref[rows, :] + _dot(merged, wout_ref[...]))
    us = [(_rms(h) * gm_ref[...]).astype(BF16) for h in hs]
    for j in range(MLP_SLICES):
        cols = slice(j * width, (j + 1) * width)
        hdn = [jnp.square(jnp.maximum(_dot(u, wup_ref[:, cols]), 0.0)).astype(BF16) for u in us]
        hs = [h + _dot(x, wdown_ref[cols, :]) for h, x in zip(hs, hdn)]
    for rows, h in zip(tiles, hs):
        y_ref[rows, :] = _rms(h) * gf_ref[...]


def _tail_call(x2, na_o, o_f, o_b, gla_g, gate, wna, wgla, wout, gm, wup, wdown, gf, *, tm):
    n = x2.shape[0]
    tiles = na_o.shape[2] // tm
    row = lambda w: pl.BlockSpec((tm, w), lambda i: (i, 0))
    nao_spec = pl.BlockSpec((1, NA_WIDTH // LANES, tm, LANES), lambda i: (i // tiles, 0, i % tiles, 0))
    consts = (wna, wgla, wout, gm, wup, wdown, gf)
    return pl.pallas_call(
        _tail_kernel,
        grid=(n // tm,),
        in_specs=[row(D_MODEL), nao_spec, row(GLA_VAL_WIDTH), row(GLA_VAL_WIDTH),
                  row(GLA_VAL_WIDTH), row(2 * D_MODEL)] + [_const_spec(c.shape) for c in consts],
        out_specs=row(D_MODEL),
        out_shape=jax.ShapeDtypeStruct((n, D_MODEL), F32),
        compiler_params=pltpu.CompilerParams(dimension_semantics=("parallel",), vmem_limit_bytes=VMEM_LIMIT),
        name="tail",
    )(x2, na_o, o_f, o_b, gla_g, gate, *consts)


def _prepare(norm_mix_g, w_in, b_in, na_rpb, gk_fwd_w, gk_fwd_b, gk_bwd_w, gk_bwd_b, gla_norm_g,
             w_br_na, w_br_gla, w_out, norm_mlp_g, w_up, w_down, norm_final_g):
    edges = np.concatenate([[0], np.cumsum(IN_WIDTHS)])
    wcols = lambda a, b: w_in[:, edges[a]:edges[b]]
    bcols = lambda a, b: b_in[None, edges[a]:edges[b]]
    r = GLA_GATE_RANK
    pad = LR_PAD - 2 * r
    wlr = jnp.pad(wcols(7, 9), ((0, 0), (0, pad)))
    blr = jnp.pad(bcols(7, 9), ((0, 0), (0, pad)))
    gkw = jnp.zeros((LR_PAD, 2 * GLA_KEY_WIDTH), F32)
    gkw = gkw.at[:r, :GLA_KEY_WIDTH].set(gk_fwd_w).at[r:2 * r, GLA_KEY_WIDTH:].set(gk_bwd_w)
    gkb = jnp.concatenate([gk_fwd_b, gk_bwd_b])[None]
    return dict(
        proj=(norm_mix_g[None], wcols(0, 1).astype(BF16), bcols(0, 1), wcols(1, 3).astype(BF16), bcols(1, 3),
              wcols(3, 7).astype(BF16), bcols(3, 7),
              wcols(9, 11).astype(BF16), bcols(9, 11), wlr.astype(BF16), blr, gkw.astype(BF16), gkb,
              jnp.tile(gla_norm_g, GLA_HEADS)[None]),
        na_bias=_na_bias_table(na_rpb),
        tail=(w_br_na.astype(BF16), w_br_gla.astype(BF16), w_out.astype(BF16),
              norm_mlp_g[None], w_up.astype(BF16), w_down.astype(BF16), norm_final_g[None]),
    )


def _trunk(x, prep, *, tm):
    batch, seq, _ = x.shape
    x2 = x.reshape(batch * seq, D_MODEL)
    na_q, na_kv, gla_q, gla_k, gla_v, gla_g, gate, la_f, la_b = _proj_call(
        x2, *prep["proj"], tm=tm, batch=batch, seq=seq)
    na_o = _na_call(na_q, na_kv, prep["na_bias"], batch=batch, seq=seq)
    o_f, o_b = _gla_call(gla_q, gla_k, gla_v, la_f, la_b, batch=batch, seq=seq)
    y = _tail_call(x2, na_o, o_f.reshape(batch * seq, GLA_VAL_WIDTH), o_b.reshape(batch * seq, GLA_VAL_WIDTH),
                   gla_g, gate, *prep["tail"], tm=tm)
    return y.reshape(batch, seq, D_MODEL)


def kernel(x_prompt, x_sample, norm_mix_g, w_in, b_in, na_rpb, gk_fwd_w, gk_fwd_b, gk_bwd_w, gk_bwd_b, gla_norm_g, w_br_na, w_br_gla, w_out, norm_mlp_g, w_up, w_down, norm_final_g):
    assert norm_mix_g.shape[0] == 1, "single-layer trunk"
    prep = _prepare(norm_mix_g[0], w_in[0], b_in[0], na_rpb[0], gk_fwd_w[0], gk_fwd_b[0], gk_bwd_w[0],
                    gk_bwd_b[0], gla_norm_g[0], w_br_na[0], w_br_gla[0], w_out[0], norm_mlp_g[0], w_up[0],
                    w_down[0], norm_final_g)
    return _trunk(x_prompt, prep, tm=ROW_TILE), _trunk(x_sample, prep, tm=ROW_TILE)
```
